```python
import jax, jax.numpy as jnp
from jax import lax
import numpy as np

D_MODEL = 1024
BATCH = 8
SEQ = 2048
DEPTH = 1

CHUNK = 64
PLE_DIM = 256
EPS = 1e-6
GMLP_GROUPS = 8
GMLP_GROUP_DIM = 128
GMLP_WIDTH = GMLP_GROUPS * GMLP_GROUP_DIM
GMLP_BLOCK = 128
FOX_HEADS = 16
FOX_HEAD_DIM = 64
FOX_WIDTH = FOX_HEADS * FOX_HEAD_DIM
Q_BLOCK = 128
D_FF = 2816
CONV_WIDTH = 3
N_BRANCH = 2
IN_COLS = 2 * GMLP_WIDTH + 3 * FOX_WIDTH + FOX_HEADS + N_BRANCH * D_MODEL

kernel_name = "hybrid_gmlp_fox_convffn_block"


def rmsnorm(x, g):
    x32 = x.astype(jnp.float32)
    y = x32 * lax.rsqrt(jnp.mean(x32 * x32, axis=-1, keepdims=True) + EPS)
    return (y * g.astype(jnp.float32)).astype(x.dtype)


def layernorm(x, g, b):
    x32 = x.astype(jnp.float32)
    mu = jnp.mean(x32, axis=-1, keepdims=True)
    xc = x32 - mu
    y = xc * lax.rsqrt(jnp.mean(xc * xc, axis=-1, keepdims=True) + EPS)
    return (y * g.astype(jnp.float32) + b.astype(jnp.float32)).astype(x.dtype)


def gmlp_spatial_gating(z_u, z_v, ln_g, ln_b, w_s, b_s):
    B, S, _ = z_u.shape
    v = layernorm(z_v, ln_g, ln_b)
    v = v.reshape(B, S // GMLP_BLOCK, GMLP_BLOCK, GMLP_GROUPS, GMLP_GROUP_DIM)
    pos = jnp.arange(GMLP_BLOCK)
    mask = (pos[None, :] // CHUNK) <= (pos[:, None] // CHUNK)
    w = jnp.where(mask[None], w_s, jnp.zeros_like(w_s))
    mixed = jnp.einsum('gts,bnsgc->bntgc', w, v) + b_s.T[None, None, :, :, None]
    return z_u * mixed.reshape(B, S, GMLP_WIDTH)


def forgetting_attention(q, k, v, f_logit, b_f):
    B, S, _ = q.shape
    def heads(t):
        return t.reshape(B, S, FOX_HEADS, FOX_HEAD_DIM).transpose(0, 2, 1, 3)
    q, k, v = heads(q), heads(k), heads(v)
    log_f = jax.nn.log_sigmoid(f_logit.astype(jnp.float32) + b_f.astype(jnp.float32))
    cum = jnp.cumsum(log_f, axis=1).transpose(0, 2, 1)
    scale = FOX_HEAD_DIM ** -0.5
    outs = []
    for i in range(S // Q_BLOCK):
        lo, hi = i * Q_BLOCK, (i + 1) * Q_BLOCK
        s = jnp.einsum('bhqd,bhkd->bhqk', q[:, :, lo:hi], k[:, :, :hi]).astype(jnp.float32) * scale
        s = s + cum[:, :, lo:hi, None] - cum[:, :, None, :hi]
        qpos = jnp.arange(lo, hi)
        kpos = jnp.arange(hi)
        s = jnp.where(kpos[None, :] <= qpos[:, None], s, -1e30)
        prob = jax.nn.softmax(s, axis=-1).astype(v.dtype)
        outs.append(jnp.einsum('bhqk,bhkd->bhqd', prob, v[:, :, :hi]))
    o = jnp.concatenate(outs, axis=2)
    return o.transpose(0, 2, 1, 3).reshape(B, S, FOX_WIDTH)


def causal_depthwise_conv(u, w, b):
    S = u.shape[1]
    up = jnp.pad(u, ((0, 0), (CONV_WIDTH - 1, 0), (0, 0)))
    out = b + w[0] * up[:, 0:S]
    for j in range(1, CONV_WIDTH):
        out = out + w[j] * up[:, j:j + S]
    return out


def setup_inputs(seed: int = 0) -> dict:
    key = jax.random.key(seed)
    ks = jax.random.split(key, 24)
    f32 = jnp.float32
    def nrm(k, shape, scale):
        return jax.random.normal(k, shape, f32) * scale
    L = DEPTH
    return {
        "x": nrm(ks[0], (BATCH, SEQ, D_MODEL), 1.0),
        "p": nrm(ks[1], (DEPTH, BATCH, SEQ, PLE_DIM), 1.0),
        "norm_mix_g": 1.0 + nrm(ks[2], (L, D_MODEL), 0.02),
        "w_in": nrm(ks[3], (L, D_MODEL, IN_COLS), D_MODEL ** -0.5),
        "b_f": 2.0 + nrm(ks[4], (L, FOX_HEADS), 0.5),
        "gmlp_ln_g": 1.0 + nrm(ks[5], (L, GMLP_WIDTH), 0.02),
        "gmlp_ln_b": nrm(ks[6], (L, GMLP_WIDTH), 0.02),
        "gmlp_w_s": nrm(ks[7], (L, GMLP_GROUPS, GMLP_BLOCK, GMLP_BLOCK), GMLP_BLOCK ** -0.5),
        "gmlp_b_s": 1.0 + nrm(ks[8], (L, GMLP_GROUPS, GMLP_BLOCK), 0.1),
        "w_branch_a": nrm(ks[9], (L, GMLP_WIDTH, D_MODEL), GMLP_WIDTH ** -0.5),
        "w_branch_b": nrm(ks[10], (L, FOX_WIDTH, D_MODEL), FOX_WIDTH ** -0.5),
        "w_out": nrm(ks[11], (L, D_MODEL, D_MODEL), D_MODEL ** -0.5),
        "norm_ffn_g": 1.0 + nrm(ks[12], (L, D_MODEL), 0.02),
        "w_up": nrm(ks[13], (L, D_MODEL, 2 * D_FF), D_MODEL ** -0.5),
        "conv_w": nrm(ks[14], (L, CONV_WIDTH, 2 * D_FF), CONV_WIDTH ** -0.5),
        "conv_b": nrm(ks[15], (L, 2 * D_FF), 0.02),
        "w_down": nrm(ks[16], (L, D_FF, D_MODEL), D_FF ** -0.5),
        "norm_ple_g": 1.0 + nrm(ks[17], (L, D_MODEL), 0.02),
        "w_ple": nrm(ks[18], (L, PLE_DIM, D_MODEL), PLE_DIM ** -0.5),
        "w_ple_gate": nrm(ks[19], (L, D_MODEL, D_MODEL), D_MODEL ** -0.5),
        "norm_final_g": 1.0 + nrm(ks[20], (D_MODEL,), 0.02),
    }


def reference(x, p, norm_mix_g, w_in, b_f, gmlp_ln_g, gmlp_ln_b, gmlp_w_s, gmlp_b_s,
              w_branch_a, w_branch_b, w_out, norm_ffn_g, w_up, conv_w, conv_b, w_down,
              norm_ple_g, w_ple, w_ple_gate, norm_final_g):
    o1 = 2 * GMLP_WIDTH
    o2 = o1 + 3 * FOX_WIDTH
    o3 = o2 + FOX_HEADS
    for i in range(DEPTH):
        h = rmsnorm(x, norm_mix_g[i])
        z = jnp.einsum('bsd,dc->bsc', h, w_in[i])
        uv = jax.nn.gelu(z[..., :o1])
        z_u, z_v = uv[..., :GMLP_WIDTH], uv[..., GMLP_WIDTH:]
        q = z[..., o1:o1 + FOX_WIDTH]
        k = z[..., o1 + FOX_WIDTH:o1 + 2 * FOX_WIDTH]
        v = z[..., o1 + 2 * FOX_WIDTH:o2]
        f_logit = z[..., o2:o3]
        gates = jax.nn.sigmoid(z[..., o3:])
        gate_a, gate_b = gates[..., :D_MODEL], gates[..., D_MODEL:]

        a = gmlp_spatial_gating(z_u, z_v, gmlp_ln_g[i], gmlp_ln_b[i], gmlp_w_s[i], gmlp_b_s[i])
        b = forgetting_attention(q, k, v, f_logit, b_f[i])
        y_a = jnp.einsum('bsc,cd->bsd', a, w_branch_a[i])
        y_b = jnp.einsum('bsc,cd->bsd', b, w_branch_b[i])
        merged = gate_a * y_a + gate_b * y_b
        x = x + jnp.einsum('bsd,de->bse', merged, w_out[i])

        h2 = rmsnorm(x, norm_ffn_g[i])
        up = jnp.einsum('bsd,df->bsf', h2, w_up[i])
        up = causal_depthwise_conv(up, conv_w[i], conv_b[i])
        act = jax.nn.gelu(up[..., :D_FF]) * up[..., D_FF:]
        x = x + jnp.einsum('bsf,fd->bsd', act, w_down[i])

        h3 = rmsnorm(x, norm_ple_g[i])
        ple = jnp.einsum('bse,ed->bsd', p[i], w_ple[i])
        x = x + ple * jax.nn.sigmoid(jnp.einsum('bsd,de->bse', h3, w_ple_gate[i]))
    return rmsnorm(x, norm_final_g)
```

```python
import functools

import numpy as np
import jax
import jax.numpy as jnp
from jax import lax
from jax.experimental import pallas as pl
from jax.experimental.pallas import tpu as pltpu

F32 = jnp.float32
BF16 = jnp.bfloat16

D_MODEL = 1024
CHUNK = 64
PLE_DIM = 256
EPS = 1e-6
GMLP_GROUPS = 8
GMLP_BLOCK = 128
GMLP_WIDTH = 1024
FOX_HEADS = 16
FOX_HEAD_DIM = 64
FOX_WIDTH = 1024
D_FF = 2816
CONV_WIDTH = 3

LANES = 128
HEAD_PAIRS = FOX_HEADS // 2
N_PARTS = 3

ZB_U, ZB_V, ZB_Q, ZB_K, ZB_VA, ZB_GA, ZB_GB = range(7)
Z_COLS = 7 * D_MODEL

TM_IN = 1024
RC_IN = 256
CUM_BLK = 256
TQ = 256
TM_MIX = 256
TM_FFN = 512
FC = D_FF // 2
HALO = 16

VMEM_LIMIT = 56 * 1024 * 1024


def _rms(x, g):
    return x * lax.rsqrt(jnp.mean(x * x, axis=-1, keepdims=True) + EPS) * g


def _split3(x):
    hi = x.astype(BF16)
    r1 = x - hi.astype(F32)
    mid = r1.astype(BF16)
    lo = (r1 - mid.astype(F32)).astype(BF16)
    return hi, mid, lo


def _log_sigmoid(x):
    return jnp.minimum(x, 0.0) - jnp.log1p(jnp.exp(-jnp.abs(x)))


def _inproj_kernel(tiles_per_batch, x_ref, g_ref, w_ref, wf_ref, bf_ref, tri_ref,
                   z_ref, cp_ref, h_scr, carry_scr):
    i = pl.program_id(0)
    j = pl.program_id(1)

    @pl.when(j == 0)
    def _():
        hb = _rms(x_ref[...], g_ref[...]).astype(BF16)
        h_scr[...] = hb
        zf = jnp.dot(hb, wf_ref[...], preferred_element_type=F32)
        logf = _log_sigmoid(zf + bf_ref[...])

        @pl.when(i % tiles_per_batch == 0)
        def _():
            carry_scr[...] = jnp.zeros_like(carry_scr)

        carry = carry_scr[0:1, :]
        for r in range(TM_IN // CUM_BLK):
            rows = slice(r * CUM_BLK, (r + 1) * CUM_BLK)
            parts = jnp.concatenate(_split3(logf[rows]), axis=1)
            cs = jnp.dot(tri_ref[...], parts, preferred_element_type=F32)
            cum = (cs[:, :LANES] + cs[:, LANES:2 * LANES]) + cs[:, 2 * LANES:] + carry
            carry = cum[CUM_BLK - 1:CUM_BLK, :]
            cp_ref[rows, :] = jnp.concatenate(_split3(cum), axis=1)
        carry_scr[...] = jnp.broadcast_to(carry, carry_scr.shape)

    def project(fn):
        for r in range(TM_IN // RC_IN):
            rows = slice(r * RC_IN, (r + 1) * RC_IN)
            acc = jnp.dot(h_scr[rows, :], w_ref[...], preferred_element_type=F32)
            z_ref[rows, :] = fn(acc).astype(BF16)

    @pl.when(j <= ZB_V)
    def _():
        project(jax.nn.gelu)

    @pl.when(jnp.logical_and(j > ZB_V, j < ZB_GA))
    def _():
        project(lambda a: a)

    @pl.when(j >= ZB_GA)
    def _():
        project(jax.nn.sigmoid)


def _inproj(x2, g, w_main, w_f, b_f, tri, seq):
    n = x2.shape[0]
    grid = (n // TM_IN, Z_COLS // D_MODEL)
    return pl.pallas_call(
        functools.partial(_inproj_kernel, seq // TM_IN),
        grid=grid,
        in_specs=[
            pl.BlockSpec((TM_IN, D_MODEL), lambda i, j: (i, 0)),
            pl.BlockSpec((1, D_MODEL), lambda i, j: (0, 0)),
            pl.BlockSpec((D_MODEL, D_MODEL), lambda i, j: (0, j)),
            pl.BlockSpec((D_MODEL, LANES), lambda i, j: (0, 0)),
            pl.BlockSpec((1, LANES), lambda i, j: (0, 0)),
            pl.BlockSpec((CUM_BLK, CUM_BLK), lambda i, j: (0, 0)),
        ],
        out_specs=[
            pl.BlockSpec((TM_IN, D_MODEL), lambda i, j: (i, j)),
            pl.BlockSpec((TM_IN, N_PARTS * LANES), lambda i, j: (i, 0)),
        ],
        out_shape=[
            jax.ShapeDtypeStruct((n, Z_COLS), BF16),
            jax.ShapeDtypeStruct((n, N_PARTS * LANES), BF16),
        ],
        scratch_shapes=[
            pltpu.VMEM((TM_IN, D_MODEL), BF16),
            pltpu.VMEM((8, LANES), F32),
        ],
        compiler_params=pltpu.CompilerParams(
            dimension_semantics=("arbitrary", "arbitrary"),
            vmem_limit_bytes=VMEM_LIMIT),
        name="inproj",
    )(x2, g, w_main, w_f, b_f, tri)


def _attn_kernel(n_kt, q_ref, k_ref, v_ref, cp_ref, pk_ref, pq_ref, crow_ref, eye_ref,
                 o_ref, kaug_scr, vt_scr):
    qi = pl.program_id(2)
    nt = (((1,), (1,)), ((), ()))

    @pl.when(qi == 0)
    def _():
        augk = jnp.dot(cp_ref[...], pk_ref[0], preferred_element_type=F32) + crow_ref[0:1, :]
        kaug_scr[:, :LANES] = k_ref[...]
        kaug_scr[:, LANES:] = augk.astype(BF16)
        for kt in range(n_kt):
            vt = lax.dot_general(eye_ref[...], v_ref[kt * TQ:(kt + 1) * TQ, :], nt,
                                 preferred_element_type=F32)
            vt_scr[kt] = vt.astype(BF16)

    q = q_ref[...] * jnp.asarray(FOX_HEAD_DIM ** -0.5, BF16)
    lane = lax.broadcasted_iota(jnp.int32, (TQ, LANES), 1)
    zero = jnp.zeros_like(q)
    cq = cp_ref[pl.ds(pl.multiple_of(qi * TQ, TQ), TQ), :]
    rows = []
    for hh in range(2):
        own = (lane >= FOX_HEAD_DIM) if hh else (lane < FOX_HEAD_DIM)
        qh = jnp.where(own, q, zero)
        aq = jnp.dot(cq, pq_ref[0, hh], preferred_element_type=F32) + crow_ref[1 + hh:2 + hh, :]
        rows.append(jnp.concatenate([qh, aq.astype(BF16)], axis=1))
    qaug = jnp.concatenate(rows, axis=0)

    def step(ki, carry, masked):
        m, l, acc0, acc1 = carry
        kblk = kaug_scr[pl.ds(pl.multiple_of(ki * TQ, TQ), TQ), :]
        s = lax.dot_general(kblk, qaug, nt, preferred_element_type=F32)
        if masked:
            kpos = lax.broadcasted_iota(jnp.int32, s.shape, 0)
            col = lax.broadcasted_iota(jnp.int32, s.shape, 1)
            qpos = jnp.where(col >= TQ, col - TQ, col)
            s = jnp.where(kpos <= qpos, s, -1e30)
        m_new = jnp.maximum(m, jnp.max(s, axis=0, keepdims=True))
        alpha = jnp.exp(m - m_new)
        p = jnp.exp(s - m_new)
        l = alpha * l + jnp.sum(p, axis=0, keepdims=True)
        pb = p.astype(BF16)
        vt = vt_scr[ki]
        acc0 = alpha[:, :TQ] * acc0 + jnp.dot(vt[:FOX_HEAD_DIM], pb[:, :TQ],
                                              preferred_element_type=F32)
        acc1 = alpha[:, TQ:] * acc1 + jnp.dot(vt[FOX_HEAD_DIM:], pb[:, TQ:],
                                              preferred_element_type=F32)
        return m_new, l, acc0, acc1

    init = (jnp.full((1, 2 * TQ), -1e30, F32), jnp.zeros((1, 2 * TQ), F32),
            jnp.zeros((FOX_HEAD_DIM, TQ), F32), jnp.zeros((FOX_HEAD_DIM, TQ), F32))
    carry = lax.fori_loop(0, qi, lambda ki, c: step(ki, c, False), init)
    m, l, acc0, acc1 = step(qi, carry, True)
    inv = 1.0 / l
    ot = jnp.concatenate([acc0 * inv[:, :TQ], acc1 * inv[:, TQ:]], axis=0)
    o_ref[...] = ot.T.astype(BF16)


def _attn_consts():
    pk = np.zeros((HEAD_PAIRS, N_PARTS * LANES, LANES), np.float32)
    pq = np.zeros((HEAD_PAIRS, 2, N_PARTS * LANES, LANES), np.float32)
    crow = np.zeros((8, LANES), np.float32)
    for p in range(HEAD_PAIRS):
        for a in range(N_PARTS):
            pk[p, a * LANES + 2 * p, 0 + a] = -1.0
            pk[p, a * LANES + 2 * p + 1, 6 + a] = -1.0
            pq[p, 0, a * LANES + 2 * p, 3 + a] = 1.0
            pq[p, 1, a * LANES + 2 * p + 1, 9 + a] = 1.0
    crow[0, 3:6] = 1.0
    crow[0, 9:12] = 1.0
    crow[1, 0:3] = 1.0
    crow[2, 6:9] = 1.0
    eye = np.eye(LANES, dtype=np.float32)
    return (jnp.asarray(pk, BF16), jnp.asarray(pq, BF16), jnp.asarray(crow, F32),
            jnp.asarray(eye, BF16))


def _attn(z, cparts, batch, seq):
    n = z.shape[0]
    n_qt = seq // TQ
    pk, pq, crow, eye = _attn_consts()
    qcol = ZB_Q * D_MODEL // LANES
    kcol = ZB_K * D_MODEL // LANES
    vcol = ZB_VA * D_MODEL // LANES
    return pl.pallas_call(
        functools.partial(_attn_kernel, n_qt),
        grid=(batch, HEAD_PAIRS, n_qt),
        in_specs=[
            pl.BlockSpec((TQ, LANES), lambda b, p, qi: (b * n_qt + qi, qcol + p)),
            pl.BlockSpec((seq, LANES), lambda b, p, qi: (b, kcol + p)),
            pl.BlockSpec((seq, LANES), lambda b, p, qi: (b, vcol + p)),
            pl.BlockSpec((seq, N_PARTS * LANES), lambda b, p, qi: (b, 0)),
            pl.BlockSpec((1, N_PARTS * LANES, LANES), lambda b, p, qi: (p, 0, 0)),
            pl.BlockSpec((1, 2, N_PARTS * LANES, LANES), lambda b, p, qi: (p, 0, 0, 0)),
            pl.BlockSpec((8, LANES), lambda b, p, qi: (0, 0)),
            pl.BlockSpec((LANES, LANES), lambda b, p, qi: (0, 0)),
        ],
        out_specs=pl.BlockSpec((TQ, LANES), lambda b, p, qi: (b * n_qt + qi, p)),
        out_shape=jax.ShapeDtypeStruct((n, FOX_WIDTH), BF16),
        scratch_shapes=[
            pltpu.VMEM((seq, 2 * LANES), BF16),
            pltpu.VMEM((n_qt, LANES, TQ), BF16),
        ],
        compiler_params=pltpu.CompilerParams(
            dimension_semantics=("arbitrary", "arbitrary", "arbitrary"),
            vmem_limit_bytes=VMEM_LIMIT),
        name="attn",
    )(z, z, z, cparts, pk, pq, crow, eye)


def _mix_kernel(u_ref, v_ref, ga_ref, gb_ref, o_ref, x_ref, lng_ref, lnb_ref, ws_ref,
                bias_ref, wa_ref, wb_ref, wo_ref, g2_ref, x1_ref, h2_ref, vn_scr, a_scr):
    v = v_ref[...].astype(F32)
    mu = jnp.mean(v, axis=-1, keepdims=True)
    vc = v - mu
    var = jnp.mean(vc * vc, axis=-1, keepdims=True)
    vn_scr[...] = (vc * lax.rsqrt(var + EPS) * lng_ref[...] + lnb_ref[...]).astype(BF16)

    t_idx = lax.broadcasted_iota(jnp.int32, (GMLP_BLOCK, GMLP_BLOCK), 0)
    s_idx = lax.broadcasted_iota(jnp.int32, (GMLP_BLOCK, GMLP_BLOCK), 1)
    causal = (s_idx // CHUNK) <= (t_idx // CHUNK)
    for g in range(GMLP_GROUPS):
        cols = slice(g * GMLP_BLOCK, (g + 1) * GMLP_BLOCK)
        wg = jnp.where(causal, ws_ref[g], jnp.zeros_like(ws_ref[g]))
        for r in range(TM_MIX // GMLP_BLOCK):
            rows = slice(r * GMLP_BLOCK, (r + 1) * GMLP_BLOCK)
            mixed = jnp.dot(wg, vn_scr[rows, cols], preferred_element_type=F32) + bias_ref[:, cols]
            a_scr[rows, cols] = (u_ref[rows, cols].astype(F32) * mixed).astype(BF16)

    ya = jnp.dot(a_scr[...], wa_ref[...], preferred_element_type=F32)
    yb = jnp.dot(o_ref[...], wb_ref[...], preferred_element_type=F32)
    merged = ga_ref[...].astype(F32) * ya + gb_ref[...].astype(F32) * yb
    x1 = x_ref[...] + jnp.dot(merged.astype(BF16), wo_ref[...], preferred_element_type=F32)
    x1_ref[...] = x1
    h2_ref[...] = _rms(x1, g2_ref[...]).astype(BF16)


def _mix(z, o, x2, ln_g, ln_b, w_s, bias_full, w_a, w_b, w_o, g2):
    n = x2.shape[0]
    row = lambda c: pl.BlockSpec((TM_MIX, D_MODEL), lambda i, c=c: (i, c))
    const = lambda shape: pl.BlockSpec(shape, lambda i: (0,) * len(shape))
    return pl.pallas_call(
        _mix_kernel,
        grid=(n // TM_MIX,),
        in_specs=[
            row(ZB_U), row(ZB_V), row(ZB_GA), row(ZB_GB), row(0), row(0),
            const((1, GMLP_WIDTH)), const((1, GMLP_WIDTH)),
            const((GMLP_GROUPS, GMLP_BLOCK, GMLP_BLOCK)),
            const((GMLP_BLOCK, GMLP_WIDTH)),
            const((GMLP_WIDTH, D_MODEL)), const((FOX_WIDTH, D_MODEL)),
            const((D_MODEL, D_MODEL)), const((1, D_MODEL)),
        ],
        out_specs=[row(0), row(0)],
        out_shape=[
            jax.ShapeDtypeStruct((n, D_MODEL), F32),
            jax.ShapeDtypeStruct((n, D_MODEL), BF16),
        ],
        scratch_shapes=[
            pltpu.VMEM((TM_MIX, GMLP_WIDTH), BF16),
            pltpu.VMEM((TM_MIX, GMLP_WIDTH), BF16),
        ],
        compiler_params=pltpu.CompilerParams(
            dimension_semantics=("arbitrary",),
            vmem_limit_bytes=VMEM_LIMIT),
        name="mix",
    )(z, z, z, z, o, x2, ln_g, ln_b, w_s, bias_full, w_a, w_b, w_o, g2)


def _ffn_kernel(tiles_per_batch, n_fc, h2_ref, halo_ref, x1_ref, p_ref, wug_ref, wul_ref,
                cwg_ref, cwl_ref, cbg_ref, cbl_ref, wd_ref, g3_ref, wpg_ref, wple_ref,
                gf_ref, out_ref, ug_scr, ul_scr, act_scr, acc_scr):
    i = pl.program_id(0)
    j = pl.program_id(1)

    halo = halo_ref[...]
    halo = jnp.where(i % tiles_per_batch == 0, jnp.zeros_like(halo), halo)
    hext = jnp.concatenate([halo, h2_ref[...]], axis=0)
    ug_scr[...] = jnp.dot(hext, wug_ref[...], preferred_element_type=F32)
    ul_scr[...] = jnp.dot(hext, wul_ref[...], preferred_element_type=F32)

    def conv(scr, cw_ref, cb_ref, cols):
        out = cb_ref[:, cols] + cw_ref[0:1, cols] * scr[pl.ds(HALO - 2, TM_FFN), cols]
        for t in range(1, CONV_WIDTH):
            out = out + cw_ref[t:t + 1, cols] * scr[pl.ds(HALO - 2 + t, TM_FFN), cols]
        return out

    for c in range(FC // LANES):
        cols = slice(c * LANES, (c + 1) * LANES)
        gate = conv(ug_scr, cwg_ref, cbg_ref, cols)
        lin = conv(ul_scr, cwl_ref, cbl_ref, cols)
        act_scr[:, cols] = (jax.nn.gelu(gate) * lin).astype(BF16)

    down = jnp.dot(act_scr[...], wd_ref[...], preferred_element_type=F32)

    @pl.when(j == 0)
    def _():
        acc_scr[...] = down

    @pl.when(j > 0)
    def _():
        acc_scr[...] += down

    @pl.when(j == n_fc - 1)
    def _():
        x2 = x1_ref[...] + acc_scr[...]
        h3 = _rms(x2, g3_ref[...]).astype(BF16)
        gate = jax.nn.sigmoid(jnp.dot(h3, wpg_ref[...], preferred_element_type=F32))
        ple = jnp.dot(p_ref[...].astype(BF16), wple_ref[...], preferred_element_type=F32)
        x3 = x2 + ple * gate
        out_ref[...] = _rms(x3, gf_ref[...])


def _ffn(h2, x1, p2, w_up, conv_w, conv_b, w_down, g3, w_pg, w_ple, gf, seq):
    n = x1.shape[0]
    n_fc = D_FF // FC
    halo_blocks = TM_FFN // HALO
    return pl.pallas_call(
        functools.partial(_ffn_kernel, seq // TM_FFN, n_fc),
        grid=(n // TM_FFN, n_fc),
        in_specs=[
            pl.BlockSpec((TM_FFN, D_MODEL), lambda i, j: (i, 0)),
            pl.BlockSpec((HALO, D_MODEL), lambda i, j: (jnp.maximum(i * halo_blocks - 1, 0), 0)),
            pl.BlockSpec((TM_FFN, D_MODEL), lambda i, j: (i, 0)),
            pl.BlockSpec((TM_FFN, PLE_DIM), lambda i, j: (i, 0)),
            pl.BlockSpec((D_MODEL, FC), lambda i, j: (0, j)),
            pl.BlockSpec((D_MODEL, FC), lambda i, j: (0, n_fc + j)),
            pl.BlockSpec((CONV_WIDTH, FC), lambda i, j: (0, j)),
            pl.BlockSpec((CONV_WIDTH, FC), lambda i, j: (0, n_fc + j)),
            pl.BlockSpec((1, FC), lambda i, j: (0, j)),
            pl.BlockSpec((1, FC), lambda i, j: (0, n_fc + j)),
            pl.BlockSpec((FC, D_MODEL), lambda i, j: (j, 0)),
            pl.BlockSpec((1, D_MODEL), lambda i, j: (0, 0)),
            pl.BlockSpec((D_MODEL, D_MODEL), lambda i, j: (0, 0)),
            pl.BlockSpec((PLE_DIM, D_MODEL), lambda i, j: (0, 0)),
            pl.BlockSpec((1, D_MODEL), lambda i, j: (0, 0)),
        ],
        out_specs=pl.BlockSpec((TM_FFN, D_MODEL), lambda i, j: (i, 0)),
        out_shape=jax.ShapeDtypeStruct((n, D_MODEL), F32),
        scratch_shapes=[
            pltpu.VMEM((HALO + TM_FFN, FC), F32),
            pltpu.VMEM((HALO + TM_FFN, FC), F32),
            pltpu.VMEM((TM_FFN, FC), BF16),
            pltpu.VMEM((TM_FFN, D_MODEL), F32),
        ],
        compiler_params=pltpu.CompilerParams(
            dimension_semantics=("arbitrary", "arbitrary"),
            vmem_limit_bytes=VMEM_LIMIT),
        name="ffn",
    )(h2, h2, x1, p2, w_up, w_up, conv_w, conv_w, conv_b, conv_b, w_down, g3, w_pg, w_ple, gf)


def _layer(x2, p2, batch, seq, norm_mix_g, w_in, b_f, gmlp_ln_g, gmlp_ln_b, gmlp_w_s, gmlp_b_s,
           w_branch_a, w_branch_b, w_out, norm_ffn_g, w_up, conv_w, conv_b, w_down,
           norm_ple_g, w_ple, w_ple_gate, out_g):
    o_f = 2 * GMLP_WIDTH + 3 * FOX_WIDTH
    w_main = jnp.concatenate([w_in[:, :o_f], w_in[:, o_f + FOX_HEADS:]], axis=1).astype(BF16)
    w_f = jnp.pad(w_in[:, o_f:o_f + FOX_HEADS], ((0, 0), (0, LANES - FOX_HEADS))).astype(BF16)
    b_fp = jnp.pad(b_f, (0, LANES - FOX_HEADS)).reshape(1, LANES)
    tri = jnp.asarray(np.tril(np.ones((CUM_BLK, CUM_BLK), np.float32)), BF16)

    z, cparts = _inproj(x2, norm_mix_g.reshape(1, -1), w_main, w_f, b_fp, tri, seq)
    o = _attn(z, cparts, batch, seq)

    bias_full = jnp.repeat(gmlp_b_s.T, GMLP_BLOCK, axis=1)
    x1, h2 = _mix(z, o, x2, gmlp_ln_g.reshape(1, -1), gmlp_ln_b.reshape(1, -1),
                  gmlp_w_s.astype(BF16), bias_full, w_branch_a.astype(BF16),
                  w_branch_b.astype(BF16), w_out.astype(BF16), norm_ffn_g.reshape(1, -1))

    return _ffn(h2, x1, p2, w_up.astype(BF16), conv_w, conv_b.reshape(1, -1),
                w_down.astype(BF16), norm_ple_g.reshape(1, -1), w_ple_gate.astype(BF16),
                w_ple.astype(BF16), out_g.reshape(1, -1), seq)


def kernel(x, p, norm_mix_g, w_in, b_f, gmlp_ln_g, gmlp_ln_b, gmlp_w_s, gmlp_b_s, w_branch_a,
           w_branch_b, w_out, norm_ffn_g, w_up, conv_w, conv_b, w_down, norm_ple_g, w_ple,
           w_ple_gate, norm_final_g):
    batch, seq, d = x.shape
    depth = p.shape[0]
    assert d == D_MODEL and depth == 1, "the fused final norm assumes a single layer"
    assert seq % TM_IN == 0 and seq % TM_FFN == 0 and seq % TQ == 0
    x2 = x.reshape(batch * seq, d)
    out = _layer(x2, p[0].reshape(batch * seq, PLE_DIM), batch, seq,
                 norm_mix_g[0], w_in[0], b_f[0], gmlp_ln_g[0], gmlp_ln_b[0], gmlp_w_s[0],
                 gmlp_b_s[0], w_branch_a[0], w_branch_b[0], w_out[0], norm_ffn_g[0], w_up[0],
                 conv_w[0], conv_b[0], w_down[0], norm_ple_g[0], w_ple[0], w_ple_gate[0],
                 norm_final_g)
    return out.reshape(batch, seq, d)
```

```python
import functools

import numpy as np
import jax
import jax.numpy as jnp
from jax import lax
from jax.experimental import pallas as pl
from jax.experimental.pallas import tpu as pltpu

F32 = jnp.float32
BF16 = jnp.bfloat16

D_MODEL = 1024
CHUNK = 64
PLE_DIM = 256
EPS = 1e-6
GMLP_GROUPS = 8
GMLP_BLOCK = 128
GMLP_WIDTH = 1024
FOX_HEADS = 16
FOX_HEAD_DIM = 64
FOX_WIDTH = 1024
D_FF = 2816
CONV_WIDTH = 3

LANES = 128
HEAD_PAIRS = FOX_HEADS // 2
N_PARTS = 3

ZB_U, ZB_V, ZB_Q, ZB_K, ZB_VA, ZB_GA, ZB_GB = range(7)
Z_COLS = 7 * D_MODEL

TM_IN = 1024
RC_IN = 256
CUM_BLK = 256
TQ = 256
ATTN_PAIRS = 1
TM_MIX = 256
TM_FFN = 512
FC = D_FF // 2
HALO = 16

VMEM_LIMIT = 56 * 1024 * 1024


def _rms(x, g):
    return x * lax.rsqrt(jnp.mean(x * x, axis=-1, keepdims=True) + EPS) * g


def _split3(x):
    hi = x.astype(BF16)
    r1 = x - hi.astype(F32)
    mid = r1.astype(BF16)
    lo = (r1 - mid.astype(F32)).astype(BF16)
    return hi, mid, lo


def _log_sigmoid(x):
    return jnp.minimum(x, 0.0) - jnp.log1p(jnp.exp(-jnp.abs(x)))


def _inproj_kernel(tiles_per_batch, x_ref, g_ref, w_ref, wf_ref, bf_ref, tri_ref,
                   z_ref, cp_ref, h_scr, carry_scr):
    i = pl.program_id(0)
    j = pl.program_id(1)

    @pl.when(j == 0)
    def _():
        hb = _rms(x_ref[...], g_ref[...]).astype(BF16)
        h_scr[...] = hb
        zf = jnp.dot(hb, wf_ref[...], preferred_element_type=F32)
        logf = _log_sigmoid(zf + bf_ref[...])

        @pl.when(i % tiles_per_batch == 0)
        def _():
            carry_scr[...] = jnp.zeros_like(carry_scr)

        carry = carry_scr[0:1, :]
        for r in range(TM_IN // CUM_BLK):
            rows = slice(r * CUM_BLK, (r + 1) * CUM_BLK)
            parts = jnp.concatenate(_split3(logf[rows]), axis=1)
            cs = jnp.dot(tri_ref[...], parts, preferred_element_type=F32)
            cum = (cs[:, :LANES] + cs[:, LANES:2 * LANES]) + cs[:, 2 * LANES:] + carry
            carry = cum[CUM_BLK - 1:CUM_BLK, :]
            cp_ref[rows, :] = jnp.concatenate(_split3(cum), axis=1)
        carry_scr[...] = jnp.broadcast_to(carry, carry_scr.shape)

    def project(fn):
        for r in range(TM_IN // RC_IN):
            rows = slice(r * RC_IN, (r + 1) * RC_IN)
            acc = jnp.dot(h_scr[rows, :], w_ref[...], preferred_element_type=F32)
            z_ref[rows, :] = fn(acc).astype(BF16)

    @pl.when(j <= ZB_V)
    def _():
        project(jax.nn.gelu)

    @pl.when(jnp.logical_and(j > ZB_V, j < ZB_GA))
    def _():
        project(lambda a: a)

    @pl.when(j >= ZB_GA)
    def _():
        project(jax.nn.sigmoid)


def _inproj(x2, g, w_main, w_f, b_f, tri, seq):
    n = x2.shape[0]
    grid = (n // TM_IN, Z_COLS // D_MODEL)
    return pl.pallas_call(
        functools.partial(_inproj_kernel, seq // TM_IN),
        grid=grid,
        in_specs=[
            pl.BlockSpec((TM_IN, D_MODEL), lambda i, j: (i, 0)),
            pl.BlockSpec((1, D_MODEL), lambda i, j: (0, 0)),
            pl.BlockSpec((D_MODEL, D_MODEL), lambda i, j: (0, j)),
            pl.BlockSpec((D_MODEL, LANES), lambda i, j: (0, 0)),
            pl.BlockSpec((1, LANES), lambda i, j: (0, 0)),
            pl.BlockSpec((CUM_BLK, CUM_BLK), lambda i, j: (0, 0)),
        ],
        out_specs=[
            pl.BlockSpec((TM_IN, D_MODEL), lambda i, j: (i, j)),
            pl.BlockSpec((TM_IN, N_PARTS * LANES), lambda i, j: (i, 0)),
        ],
        out_shape=[
            jax.ShapeDtypeStruct((n, Z_COLS), BF16),
            jax.ShapeDtypeStruct((n, N_PARTS * LANES), BF16),
        ],
        scratch_shapes=[
            pltpu.VMEM((TM_IN, D_MODEL), BF16),
            pltpu.VMEM((8, LANES), F32),
        ],
        compiler_params=pltpu.CompilerParams(
            dimension_semantics=("arbitrary", "arbitrary"),
            vmem_limit_bytes=VMEM_LIMIT),
        name="inproj",
    )(x2, g, w_main, w_f, b_f, tri)


def _attn_kernel(n_kt, q_ref, k_ref, v_ref, cp_ref, pk_ref, pq_ref, crow_ref, eye_ref,
                 o_ref, kaug_scr, vt_scr, s_scr, p_scr):
    qi = pl.program_id(2)
    nt = (((1,), (1,)), ((), ()))

    @pl.when(qi == 0)
    def _():
        for c in range(ATTN_PAIRS):
            lanes = slice(c * LANES, (c + 1) * LANES)
            augk = jnp.dot(cp_ref[...], pk_ref[c], preferred_element_type=F32) + crow_ref[0:1, :]
            kaug_scr[c, :, :LANES] = k_ref[:, lanes]
            kaug_scr[c, :, LANES:] = augk.astype(BF16)
            for kt in range(n_kt):
                vt = lax.dot_general(eye_ref[:LANES, :LANES], v_ref[kt * TQ:(kt + 1) * TQ, lanes], nt,
                                     preferred_element_type=F32)
                vt_scr[c, kt] = vt.astype(BF16)

    lane = lax.broadcasted_iota(jnp.int32, (TQ, LANES), 1)
    cq = cp_ref[pl.ds(pl.multiple_of(qi * TQ, TQ), TQ), :]
    qaug = []
    for c in range(ATTN_PAIRS):
        q = q_ref[:, c * LANES:(c + 1) * LANES] * jnp.asarray(FOX_HEAD_DIM ** -0.5, BF16)
        zero = jnp.zeros_like(q)
        rows = []
        for hh in range(2):
            own = (lane >= FOX_HEAD_DIM) if hh else (lane < FOX_HEAD_DIM)
            qh = jnp.where(own, q, zero)
            aq = (jnp.dot(cq, pq_ref[c, hh], preferred_element_type=F32)
                  + crow_ref[1 + hh:2 + hh, :])
            rows.append(jnp.concatenate([qh, aq.astype(BF16)], axis=1))
        qrows = jnp.concatenate(rows, axis=0)
        qt = lax.dot_general(eye_ref[...], qrows, nt, preferred_element_type=F32)
        qaug.append(qt.astype(BF16))

    def scores(c, ki, masked):
        kblk = kaug_scr[c, pl.ds(pl.multiple_of(ki * TQ, TQ), TQ), :]
        s = jnp.dot(kblk, qaug[c], preferred_element_type=F32)
        if masked:
            kpos = lax.broadcasted_iota(jnp.int32, s.shape, 0)
            col = lax.broadcasted_iota(jnp.int32, s.shape, 1)
            qpos = jnp.where(col >= TQ, col - TQ, col)
            s = jnp.where(kpos <= qpos, s, -1e30)
        return s, jnp.max(s, axis=0, keepdims=True)

    def softmax(s, smax, m, l):
        m_new = jnp.maximum(m, smax)
        alpha = jnp.exp(m - m_new)
        p = jnp.exp(s - m_new)
        l = alpha * l + jnp.sum(p, axis=0, keepdims=True)
        return p.astype(BF16), alpha, m_new, l

    def pv(c, ki, pb):
        vt = vt_scr[c, ki]
        return (jnp.dot(vt[:FOX_HEAD_DIM], pb[:, :TQ], preferred_element_type=F32),
                jnp.dot(vt[FOX_HEAD_DIM:], pb[:, TQ:], preferred_element_type=F32))

    def body(t, carry):
        prev_tile = jnp.where(t <= 1, qi, t - 2)
        chains = range(ATTN_PAIRS)
        nxt = [scores(c, t, False) for c in chains]
        delta = [pv(c, prev_tile, p_scr[c]) for c in chains]
        soft = []
        for c in chains:
            smax, alpha_prev, m, l, acc0, acc1 = carry[c]
            pb, alpha, m, l = softmax(s_scr[c], smax, m, l)
            p_scr[c] = pb
            soft.append((alpha, m, l))
        out = []
        for c in chains:
            smax, alpha_prev, m, l, acc0, acc1 = carry[c]
            s_scr[c] = nxt[c][0]
            acc0 = alpha_prev[:, :TQ] * acc0 + delta[c][0]
            acc1 = alpha_prev[:, TQ:] * acc1 + delta[c][1]
            out.append((nxt[c][1],) + soft[c] + (acc0, acc1))
        return tuple(out)

    init = []
    for c in range(ATTN_PAIRS):
        s, smax = scores(c, qi, True)
        s_scr[c] = s
        p_scr[c] = jnp.zeros((TQ, 2 * TQ), BF16)
        init.append((smax, jnp.ones((1, 2 * TQ), F32),
                     jnp.full((1, 2 * TQ), -1e30, F32), jnp.zeros((1, 2 * TQ), F32),
                     jnp.zeros((FOX_HEAD_DIM, TQ), F32), jnp.zeros((FOX_HEAD_DIM, TQ), F32)))
    carry = lax.fori_loop(0, qi, body, tuple(init))
    for c in range(ATTN_PAIRS):
        smax, alpha_prev, m, l, acc0, acc1 = carry[c]
        prev_tile = jnp.where(qi <= 1, qi, qi - 2)
        d0, d1 = pv(c, prev_tile, p_scr[c])
        pb, alpha, m, l = softmax(s_scr[c], smax, m, l)
        last_tile = jnp.where(qi == 0, qi, qi - 1)
        acc0 = alpha_prev[:, :TQ] * acc0 + d0
        acc1 = alpha_prev[:, TQ:] * acc1 + d1
        e0, e1 = pv(c, last_tile, pb)
        inv = 1.0 / l
        acc0 = (alpha[:, :TQ] * acc0 + e0) * inv[:, :TQ]
        acc1 = (alpha[:, TQ:] * acc1 + e1) * inv[:, TQ:]
        ot = jnp.concatenate([acc0, acc1], axis=0)
        o_ref[:, c * LANES:(c + 1) * LANES] = ot.T.astype(BF16)


def _attn_consts():
    pk = np.zeros((HEAD_PAIRS, N_PARTS * LANES, LANES), np.float32)
    pq = np.zeros((HEAD_PAIRS, 2, N_PARTS * LANES, LANES), np.float32)
    crow = np.zeros((8, LANES), np.float32)
    for p in range(HEAD_PAIRS):
        for a in range(N_PARTS):
            pk[p, a * LANES + 2 * p, 0 + a] = -1.0
            pk[p, a * LANES + 2 * p + 1, 6 + a] = -1.0
            pq[p, 0, a * LANES + 2 * p, 3 + a] = 1.0
            pq[p, 1, a * LANES + 2 * p + 1, 9 + a] = 1.0
    crow[0, 3:6] = 1.0
    crow[0, 9:12] = 1.0
    crow[1, 0:3] = 1.0
    crow[2, 6:9] = 1.0
    eye = np.eye(2 * LANES, dtype=np.float32)
    return (jnp.asarray(pk, BF16), jnp.asarray(pq, BF16), jnp.asarray(crow, F32),
            jnp.asarray(eye, BF16))


def _attn(z, cparts, batch, seq):
    n = z.shape[0]
    n_qt = seq // TQ
    pk, pq, crow, eye = _attn_consts()
    width = ATTN_PAIRS * LANES
    qcol = ZB_Q * D_MODEL // width
    kcol = ZB_K * D_MODEL // width
    vcol = ZB_VA * D_MODEL // width
    return pl.pallas_call(
        functools.partial(_attn_kernel, n_qt),
        grid=(batch, HEAD_PAIRS // ATTN_PAIRS, n_qt),
        in_specs=[
            pl.BlockSpec((TQ, width), lambda b, g, qi: (b * n_qt + qi, qcol + g)),
            pl.BlockSpec((seq, width), lambda b, g, qi: (b, kcol + g)),
            pl.BlockSpec((seq, width), lambda b, g, qi: (b, vcol + g)),
            pl.BlockSpec((seq, N_PARTS * LANES), lambda b, g, qi: (b, 0)),
            pl.BlockSpec((ATTN_PAIRS, N_PARTS * LANES, LANES), lambda b, g, qi: (g, 0, 0)),
            pl.BlockSpec((ATTN_PAIRS, 2, N_PARTS * LANES, LANES), lambda b, g, qi: (g, 0, 0, 0)),
            pl.BlockSpec((8, LANES), lambda b, g, qi: (0, 0)),
            pl.BlockSpec((2 * LANES, 2 * LANES), lambda b, g, qi: (0, 0)),
        ],
        out_specs=pl.BlockSpec((TQ, width), lambda b, g, qi: (b * n_qt + qi, g)),
        out_shape=jax.ShapeDtypeStruct((n, FOX_WIDTH), BF16),
        scratch_shapes=[
            pltpu.VMEM((ATTN_PAIRS, seq, 2 * LANES), BF16),
            pltpu.VMEM((ATTN_PAIRS, n_qt, LANES, TQ), BF16),
            pltpu.VMEM((ATTN_PAIRS, TQ, 2 * TQ), F32),
            pltpu.VMEM((ATTN_PAIRS, TQ, 2 * TQ), BF16),
        ],
        compiler_params=pltpu.CompilerParams(
            dimension_semantics=("arbitrary", "arbitrary", "arbitrary"),
            vmem_limit_bytes=VMEM_LIMIT),
        name="attn",
    )(z, z, z, cparts, pk, pq, crow, eye)


def _mix_kernel(u_ref, v_ref, ga_ref, gb_ref, o_ref, x_ref, lng_ref, lnb_ref, ws_ref,
                bias_ref, wa_ref, wb_ref, wo_ref, g2_ref, x1_ref, h2_ref, vn_scr, a_scr):
    v = v_ref[...].astype(F32)
    mu = jnp.mean(v, axis=-1, keepdims=True)
    vc = v - mu
    var = jnp.mean(vc * vc, axis=-1, keepdims=True)
    vn_scr[...] = (vc * lax.rsqrt(var + EPS) * lng_ref[...] + lnb_ref[...]).astype(BF16)

    t_idx = lax.broadcasted_iota(jnp.int32, (GMLP_BLOCK, GMLP_BLOCK), 0)
    s_idx = lax.broadcasted_iota(jnp.int32, (GMLP_BLOCK, GMLP_BLOCK), 1)
    causal = (s_idx // CHUNK) <= (t_idx // CHUNK)
    for g in range(GMLP_GROUPS):
        cols = slice(g * GMLP_BLOCK, (g + 1) * GMLP_BLOCK)
        wg = jnp.where(causal, ws_ref[g], jnp.zeros_like(ws_ref[g]))
        for r in range(TM_MIX // GMLP_BLOCK):
            rows = slice(r * GMLP_BLOCK, (r + 1) * GMLP_BLOCK)
            mixed = jnp.dot(wg, vn_scr[rows, cols], preferred_element_type=F32) + bias_ref[:, cols]
            a_scr[rows, cols] = (u_ref[rows, cols].astype(F32) * mixed).astype(BF16)

    ya = jnp.dot(a_scr[...], wa_ref[...], preferred_element_type=F32)
    yb = jnp.dot(o_ref[...], wb_ref[...], preferred_element_type=F32)
    merged = ga_ref[...].astype(F32) * ya + gb_ref[...].astype(F32) * yb
    x1 = x_ref[...] + jnp.dot(merged.astype(BF16), wo_ref[...], preferred_element_type=F32)
    x1_ref[...] = x1
    h2_ref[...] = _rms(x1, g2_ref[...]).astype(BF16)


def _mix(z, o, x2, ln_g, ln_b, w_s, bias_full, w_a, w_b, w_o, g2):
    n = x2.shape[0]
    row = lambda c: pl.BlockSpec((TM_MIX, D_MODEL), lambda i, c=c: (i, c))
    const = lambda shape: pl.BlockSpec(shape, lambda i: (0,) * len(shape))
    return pl.pallas_call(
        _mix_kernel,
        grid=(n // TM_MIX,),
        in_specs=[
            row(ZB_U), row(ZB_V), row(ZB_GA), row(ZB_GB), row(0), row(0),
            const((1, GMLP_WIDTH)), const((1, GMLP_WIDTH)),
            const((GMLP_GROUPS, GMLP_BLOCK, GMLP_BLOCK)),
            const((GMLP_BLOCK, GMLP_WIDTH)),
            const((GMLP_WIDTH, D_MODEL)), const((FOX_WIDTH, D_MODEL)),
            const((D_MODEL, D_MODEL)), const((1, D_MODEL)),
        ],
        out_specs=[row(0), row(0)],
        out_shape=[
            jax.ShapeDtypeStruct((n, D_MODEL), F32),
            jax.ShapeDtypeStruct((n, D_MODEL), BF16),
        ],
        scratch_shapes=[
            pltpu.VMEM((TM_MIX, GMLP_WIDTH), BF16),
            pltpu.VMEM((TM_MIX, GMLP_WIDTH), BF16),
        ],
        compiler_params=pltpu.CompilerParams(
            dimension_semantics=("arbitrary",),
            vmem_limit_bytes=VMEM_LIMIT),
        name="mix",
    )(z, z, z, z, o, x2, ln_g, ln_b, w_s, bias_full, w_a, w_b, w_o, g2)


def _ffn_kernel(tiles_per_batch, n_fc, h2_ref, halo_ref, x1_ref, p_ref, wug_ref, wul_ref,
                cwg_ref, cwl_ref, cbg_ref, cbl_ref, wd_ref, g3_ref, wpg_ref, wple_ref,
                gf_ref, out_ref, ug_scr, ul_scr, act_scr, acc_scr):
    i = pl.program_id(0)
    j = pl.program_id(1)

    halo = halo_ref[...]
    halo = jnp.where(i % tiles_per_batch == 0, jnp.zeros_like(halo), halo)
    hext = jnp.concatenate([halo, h2_ref[...]], axis=0)
    ug_scr[...] = jnp.dot(hext, wug_ref[...], preferred_element_type=F32)
    ul_scr[...] = jnp.dot(hext, wul_ref[...], preferred_element_type=F32)

    def conv(scr, cw_ref, cb_ref, cols):
        out = cb_ref[:, cols] + cw_ref[0:1, cols] * scr[pl.ds(HALO - 2, TM_FFN), cols]
        for t in range(1, CONV_WIDTH):
            out = out + cw_ref[t:t + 1, cols] * scr[pl.ds(HALO - 2 + t, TM_FFN), cols]
        return out

    for c in range(FC // LANES):
        cols = slice(c * LANES, (c + 1) * LANES)
        gate = conv(ug_scr, cwg_ref, cbg_ref, cols)
        lin = conv(ul_scr, cwl_ref, cbl_ref, cols)
        act_scr[:, cols] = (jax.nn.gelu(gate) * lin).astype(BF16)

    down = jnp.dot(act_scr[...], wd_ref[...], preferred_element_type=F32)

    @pl.when(j == 0)
    def _():
        acc_scr[...] = down

    @pl.when(j > 0)
    def _():
        acc_scr[...] += down

    @pl.when(j == n_fc - 1)
    def _():
        x2 = x1_ref[...] + acc_scr[...]
        h3 = _rms(x2, g3_ref[...]).astype(BF16)
        gate = jax.nn.sigmoid(jnp.dot(h3, wpg_ref[...], preferred_element_type=F32))
        ple = jnp.dot(p_ref[...].astype(BF16), wple_ref[...], preferred_element_type=F32)
        x3 = x2 + ple * gate
        out_ref[...] = _rms(x3, gf_ref[...])


def _ffn(h2, x1, p2, w_up, conv_w, conv_b, w_down, g3, w_pg, w_ple, gf, seq):
    n = x1.shape[0]
    n_fc = D_FF // FC
    halo_blocks = TM_FFN // HALO
    return pl.pallas_call(
        functools.partial(_ffn_kernel, seq // TM_FFN, n_fc),
        grid=(n // TM_FFN, n_fc),
        in_specs=[
            pl.BlockSpec((TM_FFN, D_MODEL), lambda i, j: (i, 0)),
            pl.BlockSpec((HALO, D_MODEL), lambda i, j: (jnp.maximum(i * halo_blocks - 1, 0), 0)),
            pl.BlockSpec((TM_FFN, D_MODEL), lambda i, j: (i, 0)),
            pl.BlockSpec((TM_FFN, PLE_DIM), lambda i, j: (i, 0)),
            pl.BlockSpec((D_MODEL, FC), lambda i, j: (0, j)),
            pl.BlockSpec((D_MODEL, FC), lambda i, j: (0, n_fc + j)),
            pl.BlockSpec((CONV_WIDTH, FC), lambda i, j: (0, j)),
            pl.BlockSpec((CONV_WIDTH, FC), lambda i, j: (0, n_fc + j)),
            pl.BlockSpec((1, FC), lambda i, j: (0, j)),
            pl.BlockSpec((1, FC), lambda i, j: (0, n_fc + j)),
            pl.BlockSpec((FC, D_MODEL), lambda i, j: (j, 0)),
            pl.BlockSpec((1, D_MODEL), lambda i, j: (0, 0)),
            pl.BlockSpec((D_MODEL, D_MODEL), lambda i, j: (0, 0)),
            pl.BlockSpec((PLE_DIM, D_MODEL), lambda i, j: (0, 0)),
            pl.BlockSpec((1, D_MODEL), lambda i, j: (0, 0)),
        ],
        out_specs=pl.BlockSpec((TM_FFN, D_MODEL), lambda i, j: (i, 0)),
        out_shape=jax.ShapeDtypeStruct((n, D_MODEL), F32),
        scratch_shapes=[
            pltpu.VMEM((HALO + TM_FFN, FC), F32),
            pltpu.VMEM((HALO + TM_FFN, FC), F32),
            pltpu.VMEM((TM_FFN, FC), BF16),
            pltpu.VMEM((TM_FFN, D_MODEL), F32),
        ],
        compiler_params=pltpu.CompilerParams(
            dimension_semantics=("arbitrary", "arbitrary"),
            vmem_limit_bytes=VMEM_LIMIT),
        name="ffn",
    )(h2, h2, x1, p2, w_up, w_up, conv_w, conv_w, conv_b, conv_b, w_down, g3, w_pg, w_ple, gf)


def _layer(x2, p2, batch, seq, norm_mix_g, w_in, b_f, gmlp_ln_g, gmlp_ln_b, gmlp_w_s, gmlp_b_s,
           w_branch_a, w_branch_b, w_out, norm_ffn_g, w_up, conv_w, conv_b, w_down,
           norm_ple_g, w_ple, w_ple_gate, out_g):
    o_f = 2 * GMLP_WIDTH + 3 * FOX_WIDTH
    w_main = jnp.concatenate([w_in[:, :o_f], w_in[:, o_f + FOX_HEADS:]], axis=1).astype(BF16)
    w_f = jnp.pad(w_in[:, o_f:o_f + FOX_HEADS], ((0, 0), (0, LANES - FOX_HEADS))).astype(BF16)
    b_fp = jnp.pad(b_f, (0, LANES - FOX_HEADS)).reshape(1, LANES)
    tri = jnp.asarray(np.tril(np.ones((CUM_BLK, CUM_BLK), np.float32)), BF16)

    z, cparts = _inproj(x2, norm_mix_g.reshape(1, -1), w_main, w_f, b_fp, tri, seq)
    o = _attn(z, cparts, batch, seq)

    bias_full = jnp.repeat(gmlp_b_s.T, GMLP_BLOCK, axis=1)
    x1, h2 = _mix(z, o, x2, gmlp_ln_g.reshape(1, -1), gmlp_ln_b.reshape(1, -1),
                  gmlp_w_s.astype(BF16), bias_full, w_branch_a.astype(BF16),
                  w_branch_b.astype(BF16), w_out.astype(BF16), norm_ffn_g.reshape(1, -1))

    return _ffn(h2, x1, p2, w_up.astype(BF16), conv_w, conv_b.reshape(1, -1),
                w_down.astype(BF16), norm_ple_g.reshape(1, -1), w_ple_gate.astype(BF16),
                w_ple.astype(BF16), out_g.reshape(1, -1), seq)


def kernel(x, p, norm_mix_g, w_in, b_f, gmlp_ln_g, gmlp_ln_b, gmlp_w_s, gmlp_b_s, w_branch_a,
           w_branch_b, w_out, norm_ffn_g, w_up, conv_w, conv_b, w_down, norm_ple_g, w_ple,
           w_ple_gate, norm_final_g):
    batch, seq, d = x.shape
    depth = p.shape[0]
    assert d == D_MODEL and depth == 1, "the fused final norm assumes a single layer"
    assert seq % TM_IN == 0 and seq % TM_FFN == 0 and seq % TQ == 0
    x2 = x.reshape(batch * seq, d)
    out = _layer(x2, p[0].reshape(batch * seq, PLE_DIM), batch, seq,
                 norm_mix_g[0], w_in[0], b_f[0], gmlp_ln_g[0], gmlp_ln_b[0], gmlp_w_s[0],
                 gmlp_b_s[0], w_branch_a[0], w_branch_b[0], w_out[0], norm_ffn_g[0], w_up[0],
                 conv_w[0], conv_b[0], w_down[0], norm_ple_g[0], w_ple[0], w_ple_gate[0],
                 norm_final_g)
    return out.reshape(batch, seq, d)
```

```python
import functools

import numpy as np
import jax
import jax.numpy as jnp
from jax import lax
from jax.experimental import pallas as pl
from jax.experimental.pallas import tpu as pltpu

F32 = jnp.float32
BF16 = jnp.bfloat16

D_MODEL = 1024
CHUNK = 64
PLE_DIM = 256
EPS = 1e-6
GMLP_GROUPS = 8
GMLP_BLOCK = 128
GMLP_WIDTH = 1024
FOX_HEADS = 16
FOX_HEAD_DIM = 64
FOX_WIDTH = 1024
D_FF = 2816
CONV_WIDTH = 3

LANES = 128
HEAD_PAIRS = FOX_HEADS // 2
N_PARTS = 3
LOG2E = 1.4426950408889634
Q_SCALE = FOX_HEAD_DIM ** -0.5 * LOG2E
PV_ROWS = FOX_HEAD_DIM + 16

ZB_U, ZB_V, ZB_Q, ZB_K, ZB_VA, ZB_GA, ZB_GB = range(7)
Z_COLS = 7 * D_MODEL

TM_IN = 1024
RC_IN = 256
CUM_BLK = 256
TQ = 256
ATTN_PAIRS = 2
TM_MIX = 256
TM_FFN = 512
FC = D_FF // 2
HALO = 16

VMEM_LIMIT = 56 * 1024 * 1024


def _rms(x, g):
    return x * lax.rsqrt(jnp.mean(x * x, axis=-1, keepdims=True) + EPS) * g


def _split3(x):
    hi = x.astype(BF16)
    r1 = x - hi.astype(F32)
    mid = r1.astype(BF16)
    lo = (r1 - mid.astype(F32)).astype(BF16)
    return hi, mid, lo


def _log_sigmoid(x):
    return jnp.minimum(x, 0.0) - jnp.log1p(jnp.exp(-jnp.abs(x)))


def _inproj_kernel(tiles_per_batch, x_ref, g_ref, w_ref, wf_ref, bf_ref, tri_ref,
                   z_ref, cp_ref, h_scr, carry_scr):
    i = pl.program_id(0)
    j = pl.program_id(1)

    @pl.when(j == 0)
    def _():
        hb = _rms(x_ref[...], g_ref[...]).astype(BF16)
        h_scr[...] = hb
        zf = jnp.dot(hb, wf_ref[...], preferred_element_type=F32)
        logf = _log_sigmoid(zf + bf_ref[...])

        @pl.when(i % tiles_per_batch == 0)
        def _():
            carry_scr[...] = jnp.zeros_like(carry_scr)

        carry = carry_scr[0:1, :]
        for r in range(TM_IN // CUM_BLK):
            rows = slice(r * CUM_BLK, (r + 1) * CUM_BLK)
            parts = jnp.concatenate(_split3(logf[rows]), axis=1)
            cs = jnp.dot(tri_ref[...], parts, preferred_element_type=F32)
            cum = (cs[:, :LANES] + cs[:, LANES:2 * LANES]) + cs[:, 2 * LANES:] + carry
            carry = cum[CUM_BLK - 1:CUM_BLK, :]
            cp_ref[rows, :] = jnp.concatenate(_split3(cum * LOG2E), axis=1)
        carry_scr[...] = jnp.broadcast_to(carry, carry_scr.shape)

    def project(fn):
        for r in range(TM_IN // RC_IN):
            rows = slice(r * RC_IN, (r + 1) * RC_IN)
            acc = jnp.dot(h_scr[rows, :], w_ref[...], preferred_element_type=F32)
            z_ref[rows, :] = fn(acc).astype(BF16)

    @pl.when(j <= ZB_V)
    def _():
        project(jax.nn.gelu)

    @pl.when(j == ZB_Q)
    def _():
        project(lambda a: a * Q_SCALE)

    @pl.when(jnp.logical_or(j == ZB_K, j == ZB_VA))
    def _():
        project(lambda a: a)

    @pl.when(j >= ZB_GA)
    def _():
        project(jax.nn.sigmoid)


def _inproj(x2, g, w_main, w_f, b_f, tri, seq):
    n = x2.shape[0]
    grid = (n // TM_IN, Z_COLS // D_MODEL)
    return pl.pallas_call(
        functools.partial(_inproj_kernel, seq // TM_IN),
        grid=grid,
        in_specs=[
            pl.BlockSpec((TM_IN, D_MODEL), lambda i, j: (i, 0)),
            pl.BlockSpec((1, D_MODEL), lambda i, j: (0, 0)),
            pl.BlockSpec((D_MODEL, D_MODEL), lambda i, j: (0, j)),
            pl.BlockSpec((D_MODEL, LANES), lambda i, j: (0, 0)),
            pl.BlockSpec((1, LANES), lambda i, j: (0, 0)),
            pl.BlockSpec((CUM_BLK, CUM_BLK), lambda i, j: (0, 0)),
        ],
        out_specs=[
            pl.BlockSpec((TM_IN, D_MODEL), lambda i, j: (i, j)),
            pl.BlockSpec((TM_IN, N_PARTS * LANES), lambda i, j: (i, 0)),
        ],
        out_shape=[
            jax.ShapeDtypeStruct((n, Z_COLS), BF16),
            jax.ShapeDtypeStruct((n, N_PARTS * LANES), BF16),
        ],
        scratch_shapes=[
            pltpu.VMEM((TM_IN, D_MODEL), BF16),
            pltpu.VMEM((8, LANES), F32),
        ],
        compiler_params=pltpu.CompilerParams(
            dimension_semantics=("arbitrary", "arbitrary"),
            vmem_limit_bytes=VMEM_LIMIT),
        name="inproj",
    )(x2, g, w_main, w_f, b_f, tri)


def _attn_kernel(n_qt, q_ref, k_ref, v_ref, cp_ref, pk_ref, pq_ref, crow_ref, eye_ref,
                 o_ref, kaug_scr, vt_scr, qt_scr, s_scr, p_scr, acc_scr):
    nt = (((1,), (1,)), ((), ()))
    chains = range(ATTN_PAIRS)
    lane = lax.broadcasted_iota(jnp.int32, (TQ, LANES), 1)

    for c in chains:
        lanes = slice(c * LANES, (c + 1) * LANES)
        augk = jnp.dot(cp_ref[...], pk_ref[c], preferred_element_type=F32) + crow_ref[0:1, :]
        kaug_scr[c, :, :LANES] = k_ref[:, lanes]
        kaug_scr[c, :, LANES:] = augk.astype(BF16)
        augq = [(jnp.dot(cp_ref[...], pq_ref[c, hh], preferred_element_type=F32)
                 + crow_ref[1 + hh:2 + hh, :]).astype(BF16) for hh in range(2)]
        ones = jnp.ones((PV_ROWS - FOX_HEAD_DIM, TQ), BF16)
        for t in range(n_qt):
            rows = slice(t * TQ, (t + 1) * TQ)
            vt = lax.dot_general(eye_ref[:LANES, :LANES], v_ref[rows, lanes], nt,
                                 preferred_element_type=F32).astype(BF16)
            for hh in range(2):
                vt_scr[c, t, hh, :FOX_HEAD_DIM, :] = vt[hh * FOX_HEAD_DIM:(hh + 1) * FOX_HEAD_DIM]
                vt_scr[c, t, hh, FOX_HEAD_DIM:, :] = ones
            q = q_ref[rows, lanes]
            zero = jnp.zeros_like(q)
            qrows = jnp.concatenate(
                [jnp.concatenate([jnp.where(lane < FOX_HEAD_DIM, q, zero), augq[0][rows]], axis=1),
                 jnp.concatenate([jnp.where(lane >= FOX_HEAD_DIM, q, zero), augq[1][rows]], axis=1)],
                axis=0)
            qt = lax.dot_general(eye_ref[...], qrows, nt, preferred_element_type=F32)
            qt_scr[c, t] = qt.astype(BF16)

    def scores(c, qi, ki, masked):
        start = ki * TQ if isinstance(ki, int) else pl.multiple_of(ki * TQ, TQ)
        kblk = kaug_scr[c, pl.ds(start, TQ), :]
        s = jnp.dot(kblk, qt_scr[c, qi], preferred_element_type=F32)
        if masked:
            kpos = lax.broadcasted_iota(jnp.int32, s.shape, 0)
            col = lax.broadcasted_iota(jnp.int32, s.shape, 1)
            qpos = jnp.where(col >= TQ, col - TQ, col)
            s = jnp.where(kpos <= qpos, s, -1e30)
        s_scr[c] = s
        return jnp.max(s, axis=0, keepdims=True)

    def softmax(c, smax, m):
        m_new = jnp.maximum(m, smax)
        alpha = jnp.exp2(m - m_new)
        p_scr[c] = jnp.exp2(s_scr[c] - m_new).astype(BF16)
        return alpha, m_new

    def pv(c, ki, alpha):
        for hh in range(2):
            cols = slice(hh * TQ, (hh + 1) * TQ)
            d = jnp.dot(vt_scr[c, ki, hh], p_scr[c, :, cols], preferred_element_type=F32)
            acc_scr[c, hh] = alpha[:, cols] * acc_scr[c, hh] + d

    for qi in range(n_qt):
        tile = lambda j: qi if j == 0 else j - 1
        n = qi + 1
        st = []
        for c in chains:
            acc_scr[c] = jnp.zeros(acc_scr.shape[1:], F32)
            st.append((scores(c, qi, tile(0), True), jnp.full((1, 2 * TQ), -1e30, F32), None))

        def stage(prev_tile, next_tile, st):
            if prev_tile is not None:
                for c in chains:
                    pv(c, prev_tile, st[c][2])
            new = []
            for c in chains:
                alpha, m = softmax(c, st[c][0], st[c][1])
                smax = None if next_tile is None else scores(c, qi, next_tile, False)
                new.append((smax, m, alpha))
            return new

        if n == 1:
            st = stage(None, None, st)
        else:
            st = stage(None, tile(1), st)
            if n > 2:
                def body(j, carry):
                    return tuple(stage(jnp.where(j == 1, qi, j - 2), j, list(carry)))
                st = list(lax.fori_loop(1, n - 1, body, tuple(st)))
            st = stage(tile(n - 2), None, st)
        for c in chains:
            pv(c, tile(n - 1), st[c][2])
            heads = []
            for hh in range(2):
                a = acc_scr[c, hh]
                heads.append(a[:FOX_HEAD_DIM] * (1.0 / a[FOX_HEAD_DIM:FOX_HEAD_DIM + 1]))
            ot = jnp.concatenate(heads, axis=0)
            o_ref[qi * TQ:(qi + 1) * TQ, c * LANES:(c + 1) * LANES] = ot.T.astype(BF16)


def _attn_consts():
    pk = np.zeros((HEAD_PAIRS, N_PARTS * LANES, LANES), np.float32)
    pq = np.zeros((HEAD_PAIRS, 2, N_PARTS * LANES, LANES), np.float32)
    crow = np.zeros((8, LANES), np.float32)
    for p in range(HEAD_PAIRS):
        for a in range(N_PARTS):
            pk[p, a * LANES + 2 * p, 0 + a] = -1.0
            pk[p, a * LANES + 2 * p + 1, 6 + a] = -1.0
            pq[p, 0, a * LANES + 2 * p, 3 + a] = 1.0
            pq[p, 1, a * LANES + 2 * p + 1, 9 + a] = 1.0
    crow[0, 3:6] = 1.0
    crow[0, 9:12] = 1.0
    crow[1, 0:3] = 1.0
    crow[2, 6:9] = 1.0
    eye = np.eye(2 * LANES, dtype=np.float32)
    return (jnp.asarray(pk, BF16), jnp.asarray(pq, BF16), jnp.asarray(crow, F32),
            jnp.asarray(eye, BF16))


def _attn(z, cparts, batch, seq):
    n = z.shape[0]
    n_qt = seq // TQ
    pk, pq, crow, eye = _attn_consts()
    width = ATTN_PAIRS * LANES
    qcol = ZB_Q * D_MODEL // width
    kcol = ZB_K * D_MODEL // width
    vcol = ZB_VA * D_MODEL // width
    return pl.pallas_call(
        functools.partial(_attn_kernel, n_qt),
        grid=(batch, HEAD_PAIRS // ATTN_PAIRS),
        in_specs=[
            pl.BlockSpec((seq, width), lambda b, g: (b, qcol + g)),
            pl.BlockSpec((seq, width), lambda b, g: (b, kcol + g)),
            pl.BlockSpec((seq, width), lambda b, g: (b, vcol + g)),
            pl.BlockSpec((seq, N_PARTS * LANES), lambda b, g: (b, 0)),
            pl.BlockSpec((ATTN_PAIRS, N_PARTS * LANES, LANES), lambda b, g: (g, 0, 0)),
            pl.BlockSpec((ATTN_PAIRS, 2, N_PARTS * LANES, LANES), lambda b, g: (g, 0, 0, 0)),
            pl.BlockSpec((8, LANES), lambda b, g: (0, 0)),
            pl.BlockSpec((2 * LANES, 2 * LANES), lambda b, g: (0, 0)),
        ],
        out_specs=pl.BlockSpec((seq, width), lambda b, g: (b, g)),
        out_shape=jax.ShapeDtypeStruct((n, FOX_WIDTH), BF16),
        scratch_shapes=[
            pltpu.VMEM((ATTN_PAIRS, seq, 2 * LANES), BF16),
            pltpu.VMEM((ATTN_PAIRS, n_qt, 2, PV_ROWS, TQ), BF16),
            pltpu.VMEM((ATTN_PAIRS, n_qt, 2 * LANES, 2 * TQ), BF16),
            pltpu.VMEM((ATTN_PAIRS, TQ, 2 * TQ), F32),
            pltpu.VMEM((ATTN_PAIRS, TQ, 2 * TQ), BF16),
            pltpu.VMEM((ATTN_PAIRS, 2, PV_ROWS, TQ), F32),
        ],
        compiler_params=pltpu.CompilerParams(
            dimension_semantics=("arbitrary", "arbitrary"),
            vmem_limit_bytes=VMEM_LIMIT),
        name="attn",
    )(z, z, z, cparts, pk, pq, crow, eye)


def _mix_kernel(u_ref, v_ref, ga_ref, gb_ref, o_ref, x_ref, lng_ref, lnb_ref, ws_ref,
                bias_ref, wa_ref, wb_ref, wo_ref, g2_ref, x1_ref, h2_ref, vn_scr, a_scr):
    v = v_ref[...].astype(F32)
    mu = jnp.mean(v, axis=-1, keepdims=True)
    vc = v - mu
    var = jnp.mean(vc * vc, axis=-1, keepdims=True)
    vn_scr[...] = (vc * lax.rsqrt(var + EPS) * lng_ref[...] + lnb_ref[...]).astype(BF16)

    t_idx = lax.broadcasted_iota(jnp.int32, (GMLP_BLOCK, GMLP_BLOCK), 0)
    s_idx = lax.broadcasted_iota(jnp.int32, (GMLP_BLOCK, GMLP_BLOCK), 1)
    causal = (s_idx // CHUNK) <= (t_idx // CHUNK)
    for g in range(GMLP_GROUPS):
        cols = slice(g * GMLP_BLOCK, (g + 1) * GMLP_BLOCK)
        wg = jnp.where(causal, ws_ref[g], jnp.zeros_like(ws_ref[g]))
        for r in range(TM_MIX // GMLP_BLOCK):
            rows = slice(r * GMLP_BLOCK, (r + 1) * GMLP_BLOCK)
            mixed = jnp.dot(wg, vn_scr[rows, cols], preferred_element_type=F32) + bias_ref[:, cols]
            a_scr[rows, cols] = (u_ref[rows, cols].astype(F32) * mixed).astype(BF16)

    ya = jnp.dot(a_scr[...], wa_ref[...], preferred_element_type=F32)
    yb = jnp.dot(o_ref[...], wb_ref[...], preferred_element_type=F32)
    merged = ga_ref[...].astype(F32) * ya + gb_ref[...].astype(F32) * yb
    x1 = x_ref[...] + jnp.dot(merged.astype(BF16), wo_ref[...], preferred_element_type=F32)
    x1_ref[...] = x1
    h2_ref[...] = _rms(x1, g2_ref[...]).astype(BF16)


def _mix(z, o, x2, ln_g, ln_b, w_s, bias_full, w_a, w_b, w_o, g2):
    n = x2.shape[0]
    row = lambda c: pl.BlockSpec((TM_MIX, D_MODEL), lambda i, c=c: (i, c))
    const = lambda shape: pl.BlockSpec(shape, lambda i: (0,) * len(shape))
    return pl.pallas_call(
        _mix_kernel,
        grid=(n // TM_MIX,),
        in_specs=[
            row(ZB_U), row(ZB_V), row(ZB_GA), row(ZB_GB), row(0), row(0),
            const((1, GMLP_WIDTH)), const((1, GMLP_WIDTH)),
            const((GMLP_GROUPS, GMLP_BLOCK, GMLP_BLOCK)),
            const((GMLP_BLOCK, GMLP_WIDTH)),
            const((GMLP_WIDTH, D_MODEL)), const((FOX_WIDTH, D_MODEL)),
            const((D_MODEL, D_MODEL)), const((1, D_MODEL)),
        ],
        out_specs=[row(0), row(0)],
        out_shape=[
            jax.ShapeDtypeStruct((n, D_MODEL), F32),
            jax.ShapeDtypeStruct((n, D_MODEL), BF16),
        ],
        scratch_shapes=[
            pltpu.VMEM((TM_MIX, GMLP_WIDTH), BF16),
            pltpu.VMEM((TM_MIX, GMLP_WIDTH), BF16),
        ],
        compiler_params=pltpu.CompilerParams(
            dimension_semantics=("arbitrary",),
            vmem_limit_bytes=VMEM_LIMIT),
        name="mix",
    )(z, z, z, z, o, x2, ln_g, ln_b, w_s, bias_full, w_a, w_b, w_o, g2)


def _ffn_kernel(tiles_per_batch, n_fc, h2_ref, halo_ref, x1_ref, p_ref, wug_ref, wul_ref,
                cwg_ref, cwl_ref, cbg_ref, cbl_ref, wd_ref, g3_ref, wpg_ref, wple_ref,
                gf_ref, out_ref, ug_scr, ul_scr, act_scr, acc_scr):
    i = pl.program_id(0)
    j = pl.program_id(1)

    halo = halo_ref[...]
    halo = jnp.where(i % tiles_per_batch == 0, jnp.zeros_like(halo), halo)
    hext = jnp.concatenate([halo, h2_ref[...]], axis=0)
    ug_scr[...] = jnp.dot(hext, wug_ref[...], preferred_element_type=F32)
    ul_scr[...] = jnp.dot(hext, wul_ref[...], preferred_element_type=F32)

    def conv(scr, cw_ref, cb_ref, cols):
        out = cb_ref[:, cols] + cw_ref[0:1, cols] * scr[pl.ds(HALO - 2, TM_FFN), cols]
        for t in range(1, CONV_WIDTH):
            out = out + cw_ref[t:t + 1, cols] * scr[pl.ds(HALO - 2 + t, TM_FFN), cols]
        return out

    for c in range(FC // LANES):
        cols = slice(c * LANES, (c + 1) * LANES)
        gate = conv(ug_scr, cwg_ref, cbg_ref, cols)
        lin = conv(ul_scr, cwl_ref, cbl_ref, cols)
        act_scr[:, cols] = (jax.nn.gelu(gate) * lin).astype(BF16)

    down = jnp.dot(act_scr[...], wd_ref[...], preferred_element_type=F32)

    @pl.when(j == 0)
    def _():
        acc_scr[...] = down

    @pl.when(j > 0)
    def _():
        acc_scr[...] += down

    @pl.when(j == n_fc - 1)
    def _():
        x2 = x1_ref[...] + acc_scr[...]
        h3 = _rms(x2, g3_ref[...]).astype(BF16)
        gate = jax.nn.sigmoid(jnp.dot(h3, wpg_ref[...], preferred_element_type=F32))
        ple = jnp.dot(p_ref[...].astype(BF16), wple_ref[...], preferred_element_type=F32)
        x3 = x2 + ple * gate
        out_ref[...] = _rms(x3, gf_ref[...])


def _ffn(h2, x1, p2, w_up, conv_w, conv_b, w_down, g3, w_pg, w_ple, gf, seq):
    n = x1.shape[0]
    n_fc = D_FF // FC
    halo_blocks = TM_FFN // HALO
    return pl.pallas_call(
        functools.partial(_ffn_kernel, seq // TM_FFN, n_fc),
        grid=(n // TM_FFN, n_fc),
        in_specs=[
            pl.BlockSpec((TM_FFN, D_MODEL), lambda i, j: (i, 0)),
            pl.BlockSpec((HALO, D_MODEL), lambda i, j: (jnp.maximum(i * halo_blocks - 1, 0), 0)),
            pl.BlockSpec((TM_FFN, D_MODEL), lambda i, j: (i, 0)),
            pl.BlockSpec((TM_FFN, PLE_DIM), lambda i, j: (i, 0)),
            pl.BlockSpec((D_MODEL, FC), lambda i, j: (0, j)),
            pl.BlockSpec((D_MODEL, FC), lambda i, j: (0, n_fc + j)),
            pl.BlockSpec((CONV_WIDTH, FC), lambda i, j: (0, j)),
            pl.BlockSpec((CONV_WIDTH, FC), lambda i, j: (0, n_fc + j)),
            pl.BlockSpec((1, FC), lambda i, j: (0, j)),
            pl.BlockSpec((1, FC), lambda i, j: (0, n_fc + j)),
            pl.BlockSpec((FC, D_MODEL), lambda i, j: (j, 0)),
            pl.BlockSpec((1, D_MODEL), lambda i, j: (0, 0)),
            pl.BlockSpec((D_MODEL, D_MODEL), lambda i, j: (0, 0)),
            pl.BlockSpec((PLE_DIM, D_MODEL), lambda i, j: (0, 0)),
            pl.BlockSpec((1, D_MODEL), lambda i, j: (0, 0)),
        ],
        out_specs=pl.BlockSpec((TM_FFN, D_MODEL), lambda i, j: (i, 0)),
        out_shape=jax.ShapeDtypeStruct((n, D_MODEL), F32),
        scratch_shapes=[
            pltpu.VMEM((HALO + TM_FFN, FC), F32),
            pltpu.VMEM((HALO + TM_FFN, FC), F32),
            pltpu.VMEM((TM_FFN, FC), BF16),
            pltpu.VMEM((TM_FFN, D_MODEL), F32),
        ],
        compiler_params=pltpu.CompilerParams(
            dimension_semantics=("arbitrary", "arbitrary"),
            vmem_limit_bytes=VMEM_LIMIT),
        name="ffn",
    )(h2, h2, x1, p2, w_up, w_up, conv_w, conv_w, conv_b, conv_b, w_down, g3, w_pg, w_ple, gf)


def _layer(x2, p2, batch, seq, norm_mix_g, w_in, b_f, gmlp_ln_g, gmlp_ln_b, gmlp_w_s, gmlp_b_s,
           w_branch_a, w_branch_b, w_out, norm_ffn_g, w_up, conv_w, conv_b, w_down,
           norm_ple_g, w_ple, w_ple_gate, out_g):
    o_f = 2 * GMLP_WIDTH + 3 * FOX_WIDTH
    w_main = jnp.concatenate([w_in[:, :o_f], w_in[:, o_f + FOX_HEADS:]], axis=1).astype(BF16)
    w_f = jnp.pad(w_in[:, o_f:o_f + FOX_HEADS], ((0, 0), (0, LANES - FOX_HEADS))).astype(BF16)
    b_fp = jnp.pad(b_f, (0, LANES - FOX_HEADS)).reshape(1, LANES)
    tri = jnp.asarray(np.tril(np.ones((CUM_BLK, CUM_BLK), np.float32)), BF16)

    z, cparts = _inproj(x2, norm_mix_g.reshape(1, -1), w_main, w_f, b_fp, tri, seq)
    o = _attn(z, cparts, batch, seq)

    bias_full = jnp.repeat(gmlp_b_s.T, GMLP_BLOCK, axis=1)
    x1, h2 = _mix(z, o, x2, gmlp_ln_g.reshape(1, -1), gmlp_ln_b.reshape(1, -1),
                  gmlp_w_s.astype(BF16), bias_full, w_branch_a.astype(BF16),
                  w_branch_b.astype(BF16), w_out.astype(BF16), norm_ffn_g.reshape(1, -1))

    return _ffn(h2, x1, p2, w_up.astype(BF16), conv_w, conv_b.reshape(1, -1),
                w_down.astype(BF16), norm_ple_g.reshape(1, -1), w_ple_gate.astype(BF16),
                w_ple.astype(BF16), out_g.reshape(1, -1), seq)


def kernel(x, p, norm_mix_g, w_in, b_f, gmlp_ln_g, gmlp_ln_b, gmlp_w_s, gmlp_b_s, w_branch_a,
           w_branch_b, w_out, norm_ffn_g, w_up, conv_w, conv_b, w_down, norm_ple_g, w_ple,
           w_ple_gate, norm_final_g):
    batch, seq, d = x.shape
    depth = p.shape[0]
    assert d == D_MODEL and depth == 1, "the fused final norm assumes a single layer"
    assert seq % TM_IN == 0 and seq % TM_FFN == 0 and seq % TQ == 0
    x2 = x.reshape(batch * seq, d)
    out = _layer(x2, p[0].reshape(batch * seq, PLE_DIM), batch, seq,
                 norm_mix_g[0], w_in[0], b_f[0], gmlp_ln_g[0], gmlp_ln_b[0], gmlp_w_s[0],
                 gmlp_b_s[0], w_branch_a[0], w_branch_b[0], w_out[0], norm_ffn_g[0], w_up[0],
                 conv_w[0], conv_b[0], w_down[0], norm_ple_g[0], w_ple[0], w_ple_gate[0],
                 norm_final_g)
    return out.reshape(batch, seq, d)
```

```python
import functools

import numpy as np
import jax
import jax.numpy as jnp
from jax import lax
from jax.experimental import pallas as pl
from jax.experimental.pallas import tpu as pltpu

F32 = jnp.float32
BF16 = jnp.bfloat16

D_MODEL = 1024
CHUNK = 64
PLE_DIM = 256
EPS = 1e-6
GMLP_GROUPS = 8
GMLP_BLOCK = 128
GMLP_WIDTH = 1024
FOX_HEADS = 16
FOX_HEAD_DIM = 64
FOX_WIDTH = 1024
D_FF = 2816
CONV_WIDTH = 3

LANES = 128
HEAD_PAIRS = FOX_HEADS // 2
N_PARTS = 3
LOG2E = 1.4426950408889634
Q_SCALE = FOX_HEAD_DIM ** -0.5 * LOG2E
PV_ROWS = FOX_HEAD_DIM + 16

ZB_U, ZB_V, ZB_Q, ZB_K, ZB_VA, ZB_GA, ZB_GB = range(7)
Z_COLS = 7 * D_MODEL

TM_IN = 1024
RC_IN = 256
CUM_BLK = 256
TQ = 256
ATTN_PAIRS = 2
TM_MIX = 256
TM_FFN = 512
FC = D_FF // 2
HALO = 16

VMEM_LIMIT = 56 * 1024 * 1024


def _rms(x, g):
    return x * lax.rsqrt(jnp.mean(x * x, axis=-1, keepdims=True) + EPS) * g


def _split3(x):
    hi = x.astype(BF16)
    r1 = x - hi.astype(F32)
    mid = r1.astype(BF16)
    lo = (r1 - mid.astype(F32)).astype(BF16)
    return hi, mid, lo


def _pack_parts(x):
    hi, mid, lo = (part.astype(F32) for part in _split3(x))
    lane = lax.broadcasted_iota(jnp.int32, x.shape, 1)
    packed = jnp.where(lane < FOX_HEADS, hi,
                       jnp.where(lane < 2 * FOX_HEADS, pltpu.roll(mid, FOX_HEADS, 1),
                                 pltpu.roll(lo, 2 * FOX_HEADS, 1)))
    return packed.astype(BF16)


def _log_sigmoid(x):
    return jnp.minimum(x, 0.0) - jnp.log1p(jnp.exp(-jnp.abs(x)))


def _inproj_kernel(tiles_per_batch, x_ref, g_ref, w_ref, wf_ref, bf_ref, tri_ref,
                   z_ref, cp_ref, h_scr, carry_scr):
    i = pl.program_id(0)
    j = pl.program_id(1)

    @pl.when(j == 0)
    def _():
        hb = _rms(x_ref[...], g_ref[...]).astype(BF16)
        h_scr[...] = hb
        zf = jnp.dot(hb, wf_ref[...], preferred_element_type=F32)
        logf = _log_sigmoid(zf + bf_ref[...])

        @pl.when(i % tiles_per_batch == 0)
        def _():
            carry_scr[...] = jnp.zeros_like(carry_scr)

        carry = carry_scr[0:1, :]
        for r in range(TM_IN // CUM_BLK):
            rows = slice(r * CUM_BLK, (r + 1) * CUM_BLK)
            parts = jnp.concatenate(_split3(logf[rows]), axis=1)
            cs = jnp.dot(tri_ref[...], parts, preferred_element_type=F32)
            cum = (cs[:, :LANES] + cs[:, LANES:2 * LANES]) + cs[:, 2 * LANES:] + carry
            carry = cum[CUM_BLK - 1:CUM_BLK, :]
            cp_ref[rows, :] = _pack_parts(cum * LOG2E)
        carry_scr[...] = jnp.broadcast_to(carry, carry_scr.shape)

    def project(fn):
        for r in range(TM_IN // RC_IN):
            rows = slice(r * RC_IN, (r + 1) * RC_IN)
            acc = jnp.dot(h_scr[rows, :], w_ref[...], preferred_element_type=F32)
            z_ref[rows, :] = fn(acc).astype(BF16)

    @pl.when(j <= ZB_V)
    def _():
        project(jax.nn.gelu)

    @pl.when(j == ZB_Q)
    def _():
        project(lambda a: a * Q_SCALE)

    @pl.when(jnp.logical_or(j == ZB_K, j == ZB_VA))
    def _():
        project(lambda a: a)

    @pl.when(j >= ZB_GA)
    def _():
        project(jax.nn.sigmoid)


def _inproj(x2, g, w_main, w_f, b_f, tri, seq):
    n = x2.shape[0]
    grid = (n // TM_IN, Z_COLS // D_MODEL)
    return pl.pallas_call(
        functools.partial(_inproj_kernel, seq // TM_IN),
        grid=grid,
        in_specs=[
            pl.BlockSpec((TM_IN, D_MODEL), lambda i, j: (i, 0)),
            pl.BlockSpec((1, D_MODEL), lambda i, j: (0, 0)),
            pl.BlockSpec((D_MODEL, D_MODEL), lambda i, j: (0, j)),
            pl.BlockSpec((D_MODEL, LANES), lambda i, j: (0, 0)),
            pl.BlockSpec((1, LANES), lambda i, j: (0, 0)),
            pl.BlockSpec((CUM_BLK, CUM_BLK), lambda i, j: (0, 0)),
        ],
        out_specs=[
            pl.BlockSpec((TM_IN, D_MODEL), lambda i, j: (i, j)),
            pl.BlockSpec((TM_IN, LANES), lambda i, j: (i, 0)),
        ],
        out_shape=[
            jax.ShapeDtypeStruct((n, Z_COLS), BF16),
            jax.ShapeDtypeStruct((n, LANES), BF16),
        ],
        scratch_shapes=[
            pltpu.VMEM((TM_IN, D_MODEL), BF16),
            pltpu.VMEM((8, LANES), F32),
        ],
        compiler_params=pltpu.CompilerParams(
            dimension_semantics=("arbitrary", "arbitrary"),
            vmem_limit_bytes=VMEM_LIMIT),
        name="inproj",
    )(x2, g, w_main, w_f, b_f, tri)


def _attn_kernel(n_qt, q_ref, k_ref, v_ref, cp_ref, place_ref, crow_ref, eye_ref,
                 o_ref, kaug_scr, vt_scr, qr_scr, s_scr):
    nt = (((1,), (1,)), ((), ()))
    chains = range(ATTN_PAIRS)
    lane = lax.broadcasted_iota(jnp.int32, (TQ, LANES), 1)

    for c in chains:
        lanes = slice(c * LANES, (c + 1) * LANES)
        aug = (jnp.dot(cp_ref[...], place_ref[c], preferred_element_type=F32)
               + crow_ref[0:1, :]).astype(BF16)
        kaug_scr[c, :, :LANES] = k_ref[:, lanes]
        kaug_scr[c, :, LANES:] = aug[:, :LANES]
        ones = jnp.ones((PV_ROWS - FOX_HEAD_DIM, TQ), BF16)
        for t in range(n_qt):
            rows = slice(t * TQ, (t + 1) * TQ)
            vt = lax.dot_general(eye_ref[...], v_ref[rows, lanes], nt,
                                 preferred_element_type=F32).astype(BF16)
            for hh in range(2):
                vt_scr[c, t, hh, :FOX_HEAD_DIM, :] = vt[hh * FOX_HEAD_DIM:(hh + 1) * FOX_HEAD_DIM]
                vt_scr[c, t, hh, FOX_HEAD_DIM:, :] = ones
            q = q_ref[rows, lanes]
            zero = jnp.zeros_like(q)
            for hh in range(2):
                own = (lane >= FOX_HEAD_DIM) if hh else (lane < FOX_HEAD_DIM)
                qrows = slice(hh * TQ, (hh + 1) * TQ)
                qr_scr[c, t, qrows, :LANES] = jnp.where(own, q, zero)
                qr_scr[c, t, qrows, LANES:] = aug[rows, (1 + hh) * LANES:(2 + hh) * LANES]

    kpos = lax.broadcasted_iota(jnp.int32, (TQ, 2 * TQ), 0)
    col = lax.broadcasted_iota(jnp.int32, (TQ, 2 * TQ), 1)
    causal = kpos <= jnp.where(col >= TQ, col - TQ, col)
    def pass1(qi):
        nk = (qi + 1) * TQ
        smax = []
        for c in chains:
            s = lax.dot_general(kaug_scr[c, :nk, :], qr_scr[c, qi], nt,
                                preferred_element_type=F32)
            diag = jnp.where(causal, s[nk - TQ:], -1e30)
            m = jnp.max(diag, axis=0, keepdims=True)
            if qi:
                s_scr[qi % 2, c, :nk - TQ, :] = s[:nk - TQ]
                m = jnp.maximum(m, jnp.max(s[:nk - TQ], axis=0, keepdims=True))
            s_scr[qi % 2, c, nk - TQ:nk, :] = diag
            smax.append(m)
        return smax

    def pass2(qi, smax):
        for c in chains:
            acc = [None, None]
            for t in range(qi + 1):
                p = jnp.exp2(s_scr[qi % 2, c, t * TQ:(t + 1) * TQ, :] - smax[c]).astype(BF16)
                for hh in range(2):
                    d = jnp.dot(vt_scr[c, t, hh], p[:, hh * TQ:(hh + 1) * TQ],
                                preferred_element_type=F32)
                    acc[hh] = d if acc[hh] is None else acc[hh] + d
            heads = [a[:FOX_HEAD_DIM] * (1.0 / a[FOX_HEAD_DIM:FOX_HEAD_DIM + 1]) for a in acc]
            ot = jnp.concatenate(heads, axis=0)
            o_ref[qi * TQ:(qi + 1) * TQ, c * LANES:(c + 1) * LANES] = ot.T.astype(BF16)

    smax = pass1(0)
    for qi in range(n_qt):
        nxt = pass1(qi + 1) if qi + 1 < n_qt else None
        pass2(qi, smax)
        smax = nxt


def _attn_consts():
    place = np.zeros((HEAD_PAIRS, LANES, 3 * LANES), np.float32)
    crow = np.zeros((8, 3 * LANES), np.float32)
    for p in range(HEAD_PAIRS):
        for a in range(N_PARTS):
            place[p, a * FOX_HEADS + 2 * p, 0 + a] = -1.0
            place[p, a * FOX_HEADS + 2 * p + 1, 6 + a] = -1.0
            place[p, a * FOX_HEADS + 2 * p, LANES + 3 + a] = 1.0
            place[p, a * FOX_HEADS + 2 * p + 1, 2 * LANES + 9 + a] = 1.0
    crow[0, 3:6] = 1.0
    crow[0, 9:12] = 1.0
    crow[0, LANES + 0:LANES + 3] = 1.0
    crow[0, 2 * LANES + 6:2 * LANES + 9] = 1.0
    eye = np.eye(LANES, dtype=np.float32)
    return jnp.asarray(place, BF16), jnp.asarray(crow, F32), jnp.asarray(eye, BF16)


def _attn(z, cparts, batch, seq):
    n = z.shape[0]
    n_qt = seq // TQ
    place, crow, eye = _attn_consts()
    width = ATTN_PAIRS * LANES
    qcol = ZB_Q * D_MODEL // width
    kcol = ZB_K * D_MODEL // width
    vcol = ZB_VA * D_MODEL // width
    return pl.pallas_call(
        functools.partial(_attn_kernel, n_qt),
        grid=(batch, HEAD_PAIRS // ATTN_PAIRS),
        in_specs=[
            pl.BlockSpec((seq, width), lambda b, g: (b, qcol + g)),
            pl.BlockSpec((seq, width), lambda b, g: (b, kcol + g)),
            pl.BlockSpec((seq, width), lambda b, g: (b, vcol + g)),
            pl.BlockSpec((seq, LANES), lambda b, g: (b, 0)),
            pl.BlockSpec((ATTN_PAIRS, LANES, 3 * LANES), lambda b, g: (g, 0, 0)),
            pl.BlockSpec((8, 3 * LANES), lambda b, g: (0, 0)),
            pl.BlockSpec((LANES, LANES), lambda b, g: (0, 0)),
        ],
        out_specs=pl.BlockSpec((seq, width), lambda b, g: (b, g)),
        out_shape=jax.ShapeDtypeStruct((n, FOX_WIDTH), BF16),
        scratch_shapes=[
            pltpu.VMEM((ATTN_PAIRS, seq, 2 * LANES), BF16),
            pltpu.VMEM((ATTN_PAIRS, n_qt, 2, PV_ROWS, TQ), BF16),
            pltpu.VMEM((ATTN_PAIRS, n_qt, 2 * TQ, 2 * LANES), BF16),
            pltpu.VMEM((2, ATTN_PAIRS, seq, 2 * TQ), F32),
        ],
        compiler_params=pltpu.CompilerParams(
            dimension_semantics=("arbitrary", "arbitrary"),
            vmem_limit_bytes=VMEM_LIMIT),
        name="attn",
    )(z, z, z, cparts, place, crow, eye)


def _mix_kernel(u_ref, v_ref, ga_ref, gb_ref, o_ref, x_ref, lng_ref, lnb_ref, ws_ref,
                bias_ref, wa_ref, wb_ref, wo_ref, g2_ref, x1_ref, h2_ref, vn_scr, a_scr):
    v = v_ref[...].astype(F32)
    mu = jnp.mean(v, axis=-1, keepdims=True)
    vc = v - mu
    var = jnp.mean(vc * vc, axis=-1, keepdims=True)
    vn_scr[...] = (vc * lax.rsqrt(var + EPS) * lng_ref[...] + lnb_ref[...]).astype(BF16)

    t_idx = lax.broadcasted_iota(jnp.int32, (GMLP_BLOCK, GMLP_BLOCK), 0)
    s_idx = lax.broadcasted_iota(jnp.int32, (GMLP_BLOCK, GMLP_BLOCK), 1)
    causal = (s_idx // CHUNK) <= (t_idx // CHUNK)
    for g in range(GMLP_GROUPS):
        cols = slice(g * GMLP_BLOCK, (g + 1) * GMLP_BLOCK)
        wg = jnp.where(causal, ws_ref[g], jnp.zeros_like(ws_ref[g]))
        for r in range(TM_MIX // GMLP_BLOCK):
            rows = slice(r * GMLP_BLOCK, (r + 1) * GMLP_BLOCK)
            mixed = jnp.dot(wg, vn_scr[rows, cols], preferred_element_type=F32) + bias_ref[:, cols]
            a_scr[rows, cols] = (u_ref[rows, cols].astype(F32) * mixed).astype(BF16)

    ya = jnp.dot(a_scr[...], wa_ref[...], preferred_element_type=F32)
    yb = jnp.dot(o_ref[...], wb_ref[...], preferred_element_type=F32)
    merged = ga_ref[...].astype(F32) * ya + gb_ref[...].astype(F32) * yb
    x1 = x_ref[...] + jnp.dot(merged.astype(BF16), wo_ref[...], preferred_element_type=F32)
    x1_ref[...] = x1
    h2_ref[...] = _rms(x1, g2_ref[...]).astype(BF16)


def _mix(z, o, x2, ln_g, ln_b, w_s, bias_full, w_a, w_b, w_o, g2):
    n = x2.shape[0]
    row = lambda c: pl.BlockSpec((TM_MIX, D_MODEL), lambda i, c=c: (i, c))
    const = lambda shape: pl.BlockSpec(shape, lambda i: (0,) * len(shape))
    return pl.pallas_call(
        _mix_kernel,
        grid=(n // TM_MIX,),
        in_specs=[
            row(ZB_U), row(ZB_V), row(ZB_GA), row(ZB_GB), row(0), row(0),
            const((1, GMLP_WIDTH)), const((1, GMLP_WIDTH)),
            const((GMLP_GROUPS, GMLP_BLOCK, GMLP_BLOCK)),
            const((GMLP_BLOCK, GMLP_WIDTH)),
            const((GMLP_WIDTH, D_MODEL)), const((FOX_WIDTH, D_MODEL)),
            const((D_MODEL, D_MODEL)), const((1, D_MODEL)),
        ],
        out_specs=[row(0), row(0)],
        out_shape=[
            jax.ShapeDtypeStruct((n, D_MODEL), F32),
            jax.ShapeDtypeStruct((n, D_MODEL), BF16),
        ],
        scratch_shapes=[
            pltpu.VMEM((TM_MIX, GMLP_WIDTH), BF16),
            pltpu.VMEM((TM_MIX, GMLP_WIDTH), BF16),
        ],
        compiler_params=pltpu.CompilerParams(
            dimension_semantics=("arbitrary",),
            vmem_limit_bytes=VMEM_LIMIT),
        name="mix",
    )(z, z, z, z, o, x2, ln_g, ln_b, w_s, bias_full, w_a, w_b, w_o, g2)


def _ffn_kernel(tiles_per_batch, n_fc, h2_ref, halo_ref, x1_ref, p_ref, wug_ref, wul_ref,
                cwg_ref, cwl_ref, cbg_ref, cbl_ref, wd_ref, g3_ref, wpg_ref, wple_ref,
                gf_ref, out_ref, ug_scr, ul_scr, act_scr, acc_scr):
    i = pl.program_id(0)
    j = pl.program_id(1)

    halo = halo_ref[...]
    halo = jnp.where(i % tiles_per_batch == 0, jnp.zeros_like(halo), halo)
    hext = jnp.concatenate([halo, h2_ref[...]], axis=0)
    ug_scr[...] = jnp.dot(hext, wug_ref[...], preferred_element_type=F32)
    ul_scr[...] = jnp.dot(hext, wul_ref[...], preferred_element_type=F32)

    def conv(scr, cw_ref, cb_ref, cols):
        out = cb_ref[:, cols] + cw_ref[0:1, cols] * scr[pl.ds(HALO - 2, TM_FFN), cols]
        for t in range(1, CONV_WIDTH):
            out = out + cw_ref[t:t + 1, cols] * scr[pl.ds(HALO - 2 + t, TM_FFN), cols]
        return out

    for c in range(FC // LANES):
        cols = slice(c * LANES, (c + 1) * LANES)
        gate = conv(ug_scr, cwg_ref, cbg_ref, cols)
        lin = conv(ul_scr, cwl_ref, cbl_ref, cols)
        act_scr[:, cols] = (jax.nn.gelu(gate) * lin).astype(BF16)

    down = jnp.dot(act_scr[...], wd_ref[...], preferred_element_type=F32)

    @pl.when(j == 0)
    def _():
        acc_scr[...] = down

    @pl.when(j > 0)
    def _():
        acc_scr[...] += down

    @pl.when(j == n_fc - 1)
    def _():
        x2 = x1_ref[...] + acc_scr[...]
        h3 = _rms(x2, g3_ref[...]).astype(BF16)
        gate = jax.nn.sigmoid(jnp.dot(h3, wpg_ref[...], preferred_element_type=F32))
        ple = jnp.dot(p_ref[...].astype(BF16), wple_ref[...], preferred_element_type=F32)
        x3 = x2 + ple * gate
        out_ref[...] = _rms(x3, gf_ref[...])


def _ffn(h2, x1, p2, w_up, conv_w, conv_b, w_down, g3, w_pg, w_ple, gf, seq):
    n = x1.shape[0]
    n_fc = D_FF // FC
    halo_blocks = TM_FFN // HALO
    return pl.pallas_call(
        functools.partial(_ffn_kernel, seq // TM_FFN, n_fc),
        grid=(n // TM_FFN, n_fc),
        in_specs=[
            pl.BlockSpec((TM_FFN, D_MODEL), lambda i, j: (i, 0)),
            pl.BlockSpec((HALO, D_MODEL), lambda i, j: (jnp.maximum(i * halo_blocks - 1, 0), 0)),
            pl.BlockSpec((TM_FFN, D_MODEL), lambda i, j: (i, 0)),
            pl.BlockSpec((TM_FFN, PLE_DIM), lambda i, j: (i, 0)),
            pl.BlockSpec((D_MODEL, FC), lambda i, j: (0, j)),
            pl.BlockSpec((D_MODEL, FC), lambda i, j: (0, n_fc + j)),
            pl.BlockSpec((CONV_WIDTH, FC), lambda i, j: (0, j)),
            pl.BlockSpec((CONV_WIDTH, FC), lambda i, j: (0, n_fc + j)),
            pl.BlockSpec((1, FC), lambda i, j: (0, j)),
            pl.BlockSpec((1, FC), lambda i, j: (0, n_fc + j)),
            pl.BlockSpec((FC, D_MODEL), lambda i, j: (j, 0)),
            pl.BlockSpec((1, D_MODEL), lambda i, j: (0, 0)),
            pl.BlockSpec((D_MODEL, D_MODEL), lambda i, j: (0, 0)),
            pl.BlockSpec((PLE_DIM, D_MODEL), lambda i, j: (0, 0)),
            pl.BlockSpec((1, D_MODEL), lambda i, j: (0, 0)),
        ],
        out_specs=pl.BlockSpec((TM_FFN, D_MODEL), lambda i, j: (i, 0)),
        out_shape=jax.ShapeDtypeStruct((n, D_MODEL), F32),
        scratch_shapes=[
            pltpu.VMEM((HALO + TM_FFN, FC), F32),
            pltpu.VMEM((HALO + TM_FFN, FC), F32),
            pltpu.VMEM((TM_FFN, FC), BF16),
            pltpu.VMEM((TM_FFN, D_MODEL), F32),
        ],
        compiler_params=pltpu.CompilerParams(
            dimension_semantics=("arbitrary", "arbitrary"),
            vmem_limit_bytes=VMEM_LIMIT),
        name="ffn",
    )(h2, h2, x1, p2, w_up, w_up, conv_w, conv_w, conv_b, conv_b, w_down, g3, w_pg, w_ple, gf)


def _layer(x2, p2, batch, seq, norm_mix_g, w_in, b_f, gmlp_ln_g, gmlp_ln_b, gmlp_w_s, gmlp_b_s,
           w_branch_a, w_branch_b, w_out, norm_ffn_g, w_up, conv_w, conv_b, w_down,
           norm_ple_g, w_ple, w_ple_gate, out_g):
    o_f = 2 * GMLP_WIDTH + 3 * FOX_WIDTH
    w_main = jnp.concatenate([w_in[:, :o_f], w_in[:, o_f + FOX_HEADS:]], axis=1).astype(BF16)
    w_f = jnp.pad(w_in[:, o_f:o_f + FOX_HEADS], ((0, 0), (0, LANES - FOX_HEADS))).astype(BF16)
    b_fp = jnp.pad(b_f, (0, LANES - FOX_HEADS)).reshape(1, LANES)
    tri = jnp.asarray(np.tril(np.ones((CUM_BLK, CUM_BLK), np.float32)), BF16)

    z, cparts = _inproj(x2, norm_mix_g.reshape(1, -1), w_main, w_f, b_fp, tri, seq)
    o = _attn(z, cparts, batch, seq)

    bias_full = jnp.repeat(gmlp_b_s.T, GMLP_BLOCK, axis=1)
    x1, h2 = _mix(z, o, x2, gmlp_ln_g.reshape(1, -1), gmlp_ln_b.reshape(1, -1),
                  gmlp_w_s.astype(BF16), bias_full, w_branch_a.astype(BF16),
                  w_branch_b.astype(BF16), w_out.astype(BF16), norm_ffn_g.reshape(1, -1))

    return _ffn(h2, x1, p2, w_up.astype(BF16), conv_w, conv_b.reshape(1, -1),
                w_down.astype(BF16), norm_ple_g.reshape(1, -1), w_ple_gate.astype(BF16),
                w_ple.astype(BF16), out_g.reshape(1, -1), seq)


def kernel(x, p, norm_mix_g, w_in, b_f, gmlp_ln_g, gmlp_ln_b, gmlp_w_s, gmlp_b_s, w_branch_a,
           w_branch_b, w_out, norm_ffn_g, w_up, conv_w, conv_b, w_down, norm_ple_g, w_ple,
           w_ple_gate, norm_final_g):
    batch, seq, d = x.shape
    depth = p.shape[0]
    assert d == D_MODEL and depth == 1, "the fused final norm assumes a single layer"
    assert seq % TM_IN == 0 and seq % TM_FFN == 0 and seq % TQ == 0
    x2 = x.reshape(batch * seq, d)
    out = _layer(x2, p[0].reshape(batch * seq, PLE_DIM), batch, seq,
                 norm_mix_g[0], w_in[0], b_f[0], gmlp_ln_g[0], gmlp_ln_b[0], gmlp_w_s[0],
                 gmlp_b_s[0], w_branch_a[0], w_branch_b[0], w_out[0], norm_ffn_g[0], w_up[0],
                 conv_w[0], conv_b[0], w_down[0], norm_ple_g[0], w_ple[0], w_ple_gate[0],
                 norm_final_g)
    return out.reshape(batch, seq, d)
```

```python
import functools

import numpy as np
import jax
import jax.numpy as jnp
from jax import lax
from jax.experimental import pallas as pl
from jax.experimental.pallas import tpu as pltpu

F32 = jnp.float32
BF16 = jnp.bfloat16

D_MODEL = 1024
CHUNK = 64
PLE_DIM = 256
EPS = 1e-6
GMLP_GROUPS = 8
GMLP_BLOCK = 128
GMLP_WIDTH = 1024
FOX_HEADS = 16
FOX_HEAD_DIM = 64
FOX_WIDTH = 1024
D_FF = 2816
CONV_WIDTH = 3

LANES = 128
HEAD_PAIRS = FOX_HEADS // 2
N_PARTS = 3
LOG2E = 1.4426950408889634
Q_SCALE = FOX_HEAD_DIM ** -0.5 * LOG2E
PV_ROWS = FOX_HEAD_DIM + 16

ZB_U, ZB_V, ZB_Q, ZB_K, ZB_VA, ZB_GA, ZB_GB = range(7)
Z_COLS = 7 * D_MODEL

TM_IN = 1024
RC_IN = 256
CUM_BLK = 256
TQ = 256
ATTN_PAIRS = 2
TM_MIX = 256
TM_FFN = 512
FFN_CHUNK = 256
UP_AHEAD = 1
CONV_ROWS = 128
HALO = 16

VMEM_LIMIT = 56 * 1024 * 1024


def _rms(x, g):
    return x * lax.rsqrt(jnp.mean(x * x, axis=-1, keepdims=True) + EPS) * g


def _split3(x):
    hi = x.astype(BF16)
    r1 = x - hi.astype(F32)
    mid = r1.astype(BF16)
    lo = (r1 - mid.astype(F32)).astype(BF16)
    return hi, mid, lo


def _pack_parts(x):
    hi, mid, lo = (part.astype(F32) for part in _split3(x))
    lane = lax.broadcasted_iota(jnp.int32, x.shape, 1)
    packed = jnp.where(lane < FOX_HEADS, hi,
                       jnp.where(lane < 2 * FOX_HEADS, pltpu.roll(mid, FOX_HEADS, 1),
                                 pltpu.roll(lo, 2 * FOX_HEADS, 1)))
    return packed.astype(BF16)


def _log_sigmoid(x):
    return jnp.minimum(x, 0.0) - jnp.log1p(jnp.exp(-jnp.abs(x)))


def _inproj_kernel(tiles_per_batch, x_ref, g_ref, w_ref, wf_ref, bf_ref, tri_ref,
                   z_ref, cp_ref, h_scr, carry_scr):
    i = pl.program_id(0)
    j = pl.program_id(1)

    @pl.when(j == 0)
    def _():
        hb = _rms(x_ref[...], g_ref[...]).astype(BF16)
        h_scr[...] = hb
        zf = jnp.dot(hb, wf_ref[...], preferred_element_type=F32)
        logf = _log_sigmoid(zf + bf_ref[...])

        @pl.when(i % tiles_per_batch == 0)
        def _():
            carry_scr[...] = jnp.zeros_like(carry_scr)

        carry = carry_scr[0:1, :]
        for r in range(TM_IN // CUM_BLK):
            rows = slice(r * CUM_BLK, (r + 1) * CUM_BLK)
            parts = jnp.concatenate(_split3(logf[rows]), axis=1)
            cs = jnp.dot(tri_ref[...], parts, preferred_element_type=F32)
            cum = (cs[:, :LANES] + cs[:, LANES:2 * LANES]) + cs[:, 2 * LANES:] + carry
            carry = cum[CUM_BLK - 1:CUM_BLK, :]
            cp_ref[rows, :] = _pack_parts(cum * LOG2E)
        carry_scr[...] = jnp.broadcast_to(carry, carry_scr.shape)

    def project(fn):
        for r in range(TM_IN // RC_IN):
            rows = slice(r * RC_IN, (r + 1) * RC_IN)
            acc = jnp.dot(h_scr[rows, :], w_ref[...], preferred_element_type=F32)
            z_ref[rows, :] = fn(acc).astype(BF16)

    @pl.when(j <= ZB_V)
    def _():
        project(jax.nn.gelu)

    @pl.when(j == ZB_Q)
    def _():
        project(lambda a: a * Q_SCALE)

    @pl.when(jnp.logical_or(j == ZB_K, j == ZB_VA))
    def _():
        project(lambda a: a)

    @pl.when(j >= ZB_GA)
    def _():
        project(jax.nn.sigmoid)


def _inproj(x2, g, w_main, w_f, b_f, tri, seq):
    n = x2.shape[0]
    grid = (n // TM_IN, Z_COLS // D_MODEL)
    return pl.pallas_call(
        functools.partial(_inproj_kernel, seq // TM_IN),
        grid=grid,
        in_specs=[
            pl.BlockSpec((TM_IN, D_MODEL), lambda i, j: (i, 0)),
            pl.BlockSpec((1, D_MODEL), lambda i, j: (0, 0)),
            pl.BlockSpec((D_MODEL, D_MODEL), lambda i, j: (0, j)),
            pl.BlockSpec((D_MODEL, LANES), lambda i, j: (0, 0)),
            pl.BlockSpec((1, LANES), lambda i, j: (0, 0)),
            pl.BlockSpec((CUM_BLK, CUM_BLK), lambda i, j: (0, 0)),
        ],
        out_specs=[
            pl.BlockSpec((TM_IN, D_MODEL), lambda i, j: (i, j)),
            pl.BlockSpec((TM_IN, LANES), lambda i, j: (i, 0)),
        ],
        out_shape=[
            jax.ShapeDtypeStruct((n, Z_COLS), BF16),
            jax.ShapeDtypeStruct((n, LANES), BF16),
        ],
        scratch_shapes=[
            pltpu.VMEM((TM_IN, D_MODEL), BF16),
            pltpu.VMEM((8, LANES), F32),
        ],
        compiler_params=pltpu.CompilerParams(
            dimension_semantics=("arbitrary", "arbitrary"),
            vmem_limit_bytes=VMEM_LIMIT),
        name="inproj",
    )(x2, g, w_main, w_f, b_f, tri)


def _attn_kernel(n_qt, q_ref, k_ref, v_ref, cp_ref, place_ref, crow_ref, eye_ref,
                 o_ref, kaug_scr, vt_scr, qr_scr, s_scr):
    nt = (((1,), (1,)), ((), ()))
    chains = range(ATTN_PAIRS)
    lane = lax.broadcasted_iota(jnp.int32, (TQ, LANES), 1)

    for c in chains:
        lanes = slice(c * LANES, (c + 1) * LANES)
        aug = (jnp.dot(cp_ref[...], place_ref[c], preferred_element_type=F32)
               + crow_ref[0:1, :]).astype(BF16)
        kaug_scr[c, :, :LANES] = k_ref[:, lanes]
        kaug_scr[c, :, LANES:] = aug[:, :LANES]
        ones = jnp.ones((PV_ROWS - FOX_HEAD_DIM, TQ), BF16)
        for t in range(n_qt):
            rows = slice(t * TQ, (t + 1) * TQ)
            vt = lax.dot_general(eye_ref[...], v_ref[rows, lanes], nt,
                                 preferred_element_type=F32).astype(BF16)
            for hh in range(2):
                vt_scr[c, t, hh, :FOX_HEAD_DIM, :] = vt[hh * FOX_HEAD_DIM:(hh + 1) * FOX_HEAD_DIM]
                vt_scr[c, t, hh, FOX_HEAD_DIM:, :] = ones
            q = q_ref[rows, lanes]
            zero = jnp.zeros_like(q)
            for hh in range(2):
                own = (lane >= FOX_HEAD_DIM) if hh else (lane < FOX_HEAD_DIM)
                qrows = slice(hh * TQ, (hh + 1) * TQ)
                qr_scr[c, t, qrows, :LANES] = jnp.where(own, q, zero)
                qr_scr[c, t, qrows, LANES:] = aug[rows, (1 + hh) * LANES:(2 + hh) * LANES]

    kpos = lax.broadcasted_iota(jnp.int32, (TQ, 2 * TQ), 0)
    col = lax.broadcasted_iota(jnp.int32, (TQ, 2 * TQ), 1)
    causal = kpos <= jnp.where(col >= TQ, col - TQ, col)
    def pass1(qi):
        nk = (qi + 1) * TQ
        smax = []
        for c in chains:
            s = lax.dot_general(kaug_scr[c, :nk, :], qr_scr[c, qi], nt,
                                preferred_element_type=F32)
            diag = jnp.where(causal, s[nk - TQ:], -1e30)
            m = jnp.max(diag, axis=0, keepdims=True)
            if qi:
                s_scr[qi % 2, c, :nk - TQ, :] = s[:nk - TQ]
                m = jnp.maximum(m, jnp.max(s[:nk - TQ], axis=0, keepdims=True))
            s_scr[qi % 2, c, nk - TQ:nk, :] = diag
            smax.append(m)
        return smax

    def pass2(qi, smax):
        for c in chains:
            acc = [None, None]
            for t in range(qi + 1):
                p = jnp.exp2(s_scr[qi % 2, c, t * TQ:(t + 1) * TQ, :] - smax[c]).astype(BF16)
                for hh in range(2):
                    d = jnp.dot(vt_scr[c, t, hh], p[:, hh * TQ:(hh + 1) * TQ],
                                preferred_element_type=F32)
                    acc[hh] = d if acc[hh] is None else acc[hh] + d
            heads = [a[:FOX_HEAD_DIM] * (1.0 / a[FOX_HEAD_DIM:FOX_HEAD_DIM + 1]) for a in acc]
            ot = jnp.concatenate(heads, axis=0)
            o_ref[qi * TQ:(qi + 1) * TQ, c * LANES:(c + 1) * LANES] = ot.T.astype(BF16)

    smax = pass1(0)
    for qi in range(n_qt):
        nxt = pass1(qi + 1) if qi + 1 < n_qt else None
        pass2(qi, smax)
        smax = nxt


def _attn_consts():
    place = np.zeros((HEAD_PAIRS, LANES, 3 * LANES), np.float32)
    crow = np.zeros((8, 3 * LANES), np.float32)
    for p in range(HEAD_PAIRS):
        for a in range(N_PARTS):
            place[p, a * FOX_HEADS + 2 * p, 0 + a] = -1.0
            place[p, a * FOX_HEADS + 2 * p + 1, 6 + a] = -1.0
            place[p, a * FOX_HEADS + 2 * p, LANES + 3 + a] = 1.0
            place[p, a * FOX_HEADS + 2 * p + 1, 2 * LANES + 9 + a] = 1.0
    crow[0, 3:6] = 1.0
    crow[0, 9:12] = 1.0
    crow[0, LANES + 0:LANES + 3] = 1.0
    crow[0, 2 * LANES + 6:2 * LANES + 9] = 1.0
    eye = np.eye(LANES, dtype=np.float32)
    return jnp.asarray(place, BF16), jnp.asarray(crow, F32), jnp.asarray(eye, BF16)


def _attn(z, cparts, batch, seq):
    n = z.shape[0]
    n_qt = seq // TQ
    place, crow, eye = _attn_consts()
    width = ATTN_PAIRS * LANES
    qcol = ZB_Q * D_MODEL // width
    kcol = ZB_K * D_MODEL // width
    vcol = ZB_VA * D_MODEL // width
    return pl.pallas_call(
        functools.partial(_attn_kernel, n_qt),
        grid=(batch, HEAD_PAIRS // ATTN_PAIRS),
        in_specs=[
            pl.BlockSpec((seq, width), lambda b, g: (b, qcol + g)),
            pl.BlockSpec((seq, width), lambda b, g: (b, kcol + g)),
            pl.BlockSpec((seq, width), lambda b, g: (b, vcol + g)),
            pl.BlockSpec((seq, LANES), lambda b, g: (b, 0)),
            pl.BlockSpec((ATTN_PAIRS, LANES, 3 * LANES), lambda b, g: (g, 0, 0)),
            pl.BlockSpec((8, 3 * LANES), lambda b, g: (0, 0)),
            pl.BlockSpec((LANES, LANES), lambda b, g: (0, 0)),
        ],
        out_specs=pl.BlockSpec((seq, width), lambda b, g: (b, g)),
        out_shape=jax.ShapeDtypeStruct((n, FOX_WIDTH), BF16),
        scratch_shapes=[
            pltpu.VMEM((ATTN_PAIRS, seq, 2 * LANES), BF16),
            pltpu.VMEM((ATTN_PAIRS, n_qt, 2, PV_ROWS, TQ), BF16),
            pltpu.VMEM((ATTN_PAIRS, n_qt, 2 * TQ, 2 * LANES), BF16),
            pltpu.VMEM((2, ATTN_PAIRS, seq, 2 * TQ), F32),
        ],
        compiler_params=pltpu.CompilerParams(
            dimension_semantics=("arbitrary", "arbitrary"),
            vmem_limit_bytes=VMEM_LIMIT),
        name="attn",
    )(z, z, z, cparts, place, crow, eye)


def _mix_kernel(u_ref, v_ref, ga_ref, gb_ref, o_ref, x_ref, lng_ref, lnb_ref, ws_ref,
                bias_ref, wa_ref, wb_ref, wo_ref, g2_ref, x1_ref, h2_ref, vn_scr, a_scr):
    v = v_ref[...].astype(F32)
    mu = jnp.mean(v, axis=-1, keepdims=True)
    vc = v - mu
    var = jnp.mean(vc * vc, axis=-1, keepdims=True)
    vn_scr[...] = (vc * lax.rsqrt(var + EPS) * lng_ref[...] + lnb_ref[...]).astype(BF16)

    t_idx = lax.broadcasted_iota(jnp.int32, (GMLP_BLOCK, GMLP_BLOCK), 0)
    s_idx = lax.broadcasted_iota(jnp.int32, (GMLP_BLOCK, GMLP_BLOCK), 1)
    causal = (s_idx // CHUNK) <= (t_idx // CHUNK)
    for g in range(GMLP_GROUPS):
        cols = slice(g * GMLP_BLOCK, (g + 1) * GMLP_BLOCK)
        wg = jnp.where(causal, ws_ref[g], jnp.zeros_like(ws_ref[g]))
        for r in range(TM_MIX // GMLP_BLOCK):
            rows = slice(r * GMLP_BLOCK, (r + 1) * GMLP_BLOCK)
            mixed = jnp.dot(wg, vn_scr[rows, cols], preferred_element_type=F32) + bias_ref[:, cols]
            a_scr[rows, cols] = (u_ref[rows, cols].astype(F32) * mixed).astype(BF16)

    ya = jnp.dot(a_scr[...], wa_ref[...], preferred_element_type=F32)
    yb = jnp.dot(o_ref[...], wb_ref[...], preferred_element_type=F32)
    merged = ga_ref[...].astype(F32) * ya + gb_ref[...].astype(F32) * yb
    x1 = x_ref[...] + jnp.dot(merged.astype(BF16), wo_ref[...], preferred_element_type=F32)
    x1_ref[...] = x1
    h2_ref[...] = _rms(x1, g2_ref[...]).astype(BF16)


def _mix(z, o, x2, ln_g, ln_b, w_s, bias_full, w_a, w_b, w_o, g2):
    n = x2.shape[0]
    row = lambda c: pl.BlockSpec((TM_MIX, D_MODEL), lambda i, c=c: (i, c))
    const = lambda shape: pl.BlockSpec(shape, lambda i: (0,) * len(shape))
    return pl.pallas_call(
        _mix_kernel,
        grid=(n // TM_MIX,),
        in_specs=[
            row(ZB_U), row(ZB_V), row(ZB_GA), row(ZB_GB), row(0), row(0),
            const((1, GMLP_WIDTH)), const((1, GMLP_WIDTH)),
            const((GMLP_GROUPS, GMLP_BLOCK, GMLP_BLOCK)),
            const((GMLP_BLOCK, GMLP_WIDTH)),
            const((GMLP_WIDTH, D_MODEL)), const((FOX_WIDTH, D_MODEL)),
            const((D_MODEL, D_MODEL)), const((1, D_MODEL)),
        ],
        out_specs=[row(0), row(0)],
        out_shape=[
            jax.ShapeDtypeStruct((n, D_MODEL), F32),
            jax.ShapeDtypeStruct((n, D_MODEL), BF16),
        ],
        scratch_shapes=[
            pltpu.VMEM((TM_MIX, GMLP_WIDTH), BF16),
            pltpu.VMEM((TM_MIX, GMLP_WIDTH), BF16),
        ],
        compiler_params=pltpu.CompilerParams(
            dimension_semantics=("arbitrary",),
            vmem_limit_bytes=VMEM_LIMIT),
        name="mix",
    )(z, z, z, z, o, x2, ln_g, ln_b, w_s, bias_full, w_a, w_b, w_o, g2)


def _ffn_kernel(tiles_per_batch, h2_ref, halo_ref, x1_ref, p_ref, wup_ref, cw_ref, cb_ref,
                wd_ref, g3_ref, wpg_ref, wple_ref, gf_ref, out_ref,
                hext_scr, u_scr, act_scr, acc_scr):
    i = pl.program_id(0)
    halo = halo_ref[...]
    hext_scr[:HALO, :] = jnp.where(i % tiles_per_batch == 0, jnp.zeros_like(halo), halo)
    hext_scr[HALO:, :] = h2_ref[...]

    def up(c):
        cols = slice(c * 2 * FFN_CHUNK, (c + 1) * 2 * FFN_CHUNK)
        u_scr[c % (UP_AHEAD + 1)] = jnp.dot(hext_scr[...], wup_ref[:, cols],
                                            preferred_element_type=F32)

    def conv(c, r, lanes):
        cols = slice(c * 2 * FFN_CHUNK + lanes.start, c * 2 * FFN_CHUNK + lanes.stop)
        u = u_scr.at[c % (UP_AHEAD + 1)]
        first = HALO - (CONV_WIDTH - 1) + r * CONV_ROWS
        out = cb_ref[:, cols] + cw_ref[0:1, cols] * u[pl.ds(first, CONV_ROWS), lanes]
        for t in range(1, CONV_WIDTH):
            out = out + cw_ref[t:t + 1, cols] * u[pl.ds(first + t, CONV_ROWS), lanes]
        return out

    n_chunks = D_FF // FFN_CHUNK
    for c in range(UP_AHEAD):
        up(c)
    for c in range(n_chunks):
        if c + UP_AHEAD < n_chunks:
            up(c + UP_AHEAD)
        for k in range(FFN_CHUNK // LANES):
            for r in range(TM_FFN // CONV_ROWS):
                gate = conv(c, r, slice(k * LANES, (k + 1) * LANES))
                lin = conv(c, r, slice(FFN_CHUNK + k * LANES, FFN_CHUNK + (k + 1) * LANES))
                act_scr[c % 2, r * CONV_ROWS:(r + 1) * CONV_ROWS, k * LANES:(k + 1) * LANES] = (
                    jax.nn.gelu(gate) * lin).astype(BF16)
        down = jnp.dot(act_scr[c % 2], wd_ref[c * FFN_CHUNK:(c + 1) * FFN_CHUNK, :],
                       preferred_element_type=F32)
        if c == 0:
            acc_scr[...] = x1_ref[...] + down
        else:
            acc_scr[...] += down

    x2 = acc_scr[...]
    h3 = _rms(x2, g3_ref[...]).astype(BF16)
    gate = jax.nn.sigmoid(jnp.dot(h3, wpg_ref[...], preferred_element_type=F32))
    ple = jnp.dot(p_ref[...].astype(BF16), wple_ref[...], preferred_element_type=F32)
    x3 = x2 + ple * gate
    out_ref[...] = _rms(x3, gf_ref[...])


def _ffn(h2, x1, p2, w_up, conv_w, conv_b, w_down, g3, w_pg, w_ple, gf, seq):
    n = x1.shape[0]
    halo_blocks = TM_FFN // HALO
    row = lambda width: pl.BlockSpec((TM_FFN, width), lambda i: (i, 0))
    const = lambda shape: pl.BlockSpec(shape, lambda i: (0,) * len(shape),
                                       pipeline_mode=pl.Buffered(1))
    return pl.pallas_call(
        functools.partial(_ffn_kernel, seq // TM_FFN),
        grid=(n // TM_FFN,),
        in_specs=[
            row(D_MODEL),
            pl.BlockSpec((HALO, D_MODEL), lambda i: (jnp.maximum(i * halo_blocks - 1, 0), 0)),
            row(D_MODEL), row(PLE_DIM),
            const((D_MODEL, 2 * D_FF)), const((CONV_WIDTH, 2 * D_FF)), const((1, 2 * D_FF)),
            const((D_FF, D_MODEL)), const((1, D_MODEL)), const((D_MODEL, D_MODEL)),
            const((PLE_DIM, D_MODEL)), const((1, D_MODEL)),
        ],
        out_specs=row(D_MODEL),
        out_shape=jax.ShapeDtypeStruct((n, D_MODEL), F32),
        scratch_shapes=[
            pltpu.VMEM((HALO + TM_FFN, D_MODEL), BF16),
            pltpu.VMEM((UP_AHEAD + 1, HALO + TM_FFN, 2 * FFN_CHUNK), F32),
            pltpu.VMEM((2, TM_FFN, FFN_CHUNK), BF16),
            pltpu.VMEM((TM_FFN, D_MODEL), F32),
        ],
        compiler_params=pltpu.CompilerParams(
            dimension_semantics=("arbitrary",),
            vmem_limit_bytes=VMEM_LIMIT),
        name="ffn",
    )(h2, h2, x1, p2, w_up, conv_w, conv_b, w_down, g3, w_pg, w_ple, gf)


def _layer(x2, p2, batch, seq, norm_mix_g, w_in, b_f, gmlp_ln_g, gmlp_ln_b, gmlp_w_s, gmlp_b_s,
           w_branch_a, w_branch_b, w_out, norm_ffn_g, w_up, conv_w, conv_b, w_down,
           norm_ple_g, w_ple, w_ple_gate, out_g):
    o_f = 2 * GMLP_WIDTH + 3 * FOX_WIDTH
    w_main = jnp.concatenate([w_in[:, :o_f], w_in[:, o_f + FOX_HEADS:]], axis=1).astype(BF16)
    w_f = jnp.pad(w_in[:, o_f:o_f + FOX_HEADS], ((0, 0), (0, LANES - FOX_HEADS))).astype(BF16)
    b_fp = jnp.pad(b_f, (0, LANES - FOX_HEADS)).reshape(1, LANES)
    tri = jnp.asarray(np.tril(np.ones((CUM_BLK, CUM_BLK), np.float32)), BF16)

    z, cparts = _inproj(x2, norm_mix_g.reshape(1, -1), w_main, w_f, b_fp, tri, seq)
    o = _attn(z, cparts, batch, seq)

    bias_full = jnp.repeat(gmlp_b_s.T, GMLP_BLOCK, axis=1)
    x1, h2 = _mix(z, o, x2, gmlp_ln_g.reshape(1, -1), gmlp_ln_b.reshape(1, -1),
                  gmlp_w_s.astype(BF16), bias_full, w_branch_a.astype(BF16),
                  w_branch_b.astype(BF16), w_out.astype(BF16), norm_ffn_g.reshape(1, -1))

    def interleave(a):
        lead = a.shape[:-1]
        a = a.reshape(lead + (2, D_FF // FFN_CHUNK, FFN_CHUNK))
        return jnp.swapaxes(a, -3, -2).reshape(lead + (2 * D_FF,))

    return _ffn(h2, x1, p2, interleave(w_up).astype(BF16), interleave(conv_w),
                interleave(conv_b).reshape(1, -1), w_down.astype(BF16),
                norm_ple_g.reshape(1, -1), w_ple_gate.astype(BF16), w_ple.astype(BF16),
                out_g.reshape(1, -1), seq)


def kernel(x, p, norm_mix_g, w_in, b_f, gmlp_ln_g, gmlp_ln_b, gmlp_w_s, gmlp_b_s, w_branch_a,
           w_branch_b, w_out, norm_ffn_g, w_up, conv_w, conv_b, w_down, norm_ple_g, w_ple,
           w_ple_gate, norm_final_g):
    batch, seq, d = x.shape
    depth = p.shape[0]
    assert d == D_MODEL and depth == 1, "the fused final norm assumes a single layer"
    assert seq % TM_IN == 0 and seq % TM_FFN == 0 and seq % TQ == 0
    x2 = x.reshape(batch * seq, d)
    out = _layer(x2, p[0].reshape(batch * seq, PLE_DIM), batch, seq,
                 norm_mix_g[0], w_in[0], b_f[0], gmlp_ln_g[0], gmlp_ln_b[0], gmlp_w_s[0],
                 gmlp_b_s[0], w_branch_a[0], w_branch_b[0], w_out[0], norm_ffn_g[0], w_up[0],
                 conv_w[0], conv_b[0], w_down[0], norm_ple_g[0], w_ple[0], w_ple_gate[0],
                 norm_final_g)
    return out.reshape(batch, seq, d)
```

```python
import functools

import numpy as np
import jax
import jax.numpy as jnp
from jax import lax
from jax.experimental import pallas as pl
from jax.experimental.pallas import tpu as pltpu

F32 = jnp.float32
BF16 = jnp.bfloat16

D_MODEL = 1024
CHUNK = 64
PLE_DIM = 256
EPS = 1e-6
GMLP_GROUPS = 8
GMLP_BLOCK = 128
GMLP_WIDTH = 1024
FOX_HEADS = 16
FOX_HEAD_DIM = 64
FOX_WIDTH = 1024
D_FF = 2816
CONV_WIDTH = 3

LANES = 128
HEAD_PAIRS = FOX_HEADS // 2
N_PARTS = 3
LOG2E = 1.4426950408889634
Q_SCALE = FOX_HEAD_DIM ** -0.5 * LOG2E
PV_ROWS = FOX_HEAD_DIM + 16

ZB_U, ZB_V, ZB_Q, ZB_K, ZB_VA, ZB_GA, ZB_GB = range(7)
Z_COLS = 7 * D_MODEL

TM_IN = 2048
RC_IN = 256
CUM_BLK = 256
TQ = 256
ATTN_PAIRS = 2
TM_MIX = 512
TM_FFN = 512
FC = D_FF // 2
HALO = 16

VMEM_LIMIT = 56 * 1024 * 1024


def _rms(x, g):
    return x * lax.rsqrt(jnp.mean(x * x, axis=-1, keepdims=True) + EPS) * g


def _split3(x):
    hi = x.astype(BF16)
    r1 = x - hi.astype(F32)
    mid = r1.astype(BF16)
    lo = (r1 - mid.astype(F32)).astype(BF16)
    return hi, mid, lo


def _pack_parts(x):
    hi, mid, lo = (part.astype(F32) for part in _split3(x))
    lane = lax.broadcasted_iota(jnp.int32, x.shape, 1)
    packed = jnp.where(lane < FOX_HEADS, hi,
                       jnp.where(lane < 2 * FOX_HEADS, pltpu.roll(mid, FOX_HEADS, 1),
                                 pltpu.roll(lo, 2 * FOX_HEADS, 1)))
    return packed.astype(BF16)


def _log_sigmoid(x):
    return jnp.minimum(x, 0.0) - jnp.log1p(jnp.exp(-jnp.abs(x)))


def _inproj_kernel(tiles_per_batch, x_ref, g_ref, w_ref, wg_ref, wf_ref, bf_ref, tri_ref,
                   z_ref, cp_ref, h_scr, carry_scr):
    i = pl.program_id(0)
    j = pl.program_id(1)

    @pl.when(j == 0)
    def _():
        hb = _rms(x_ref[...], g_ref[...]).astype(BF16)
        h_scr[...] = hb
        zf = jnp.dot(hb, wf_ref[...], preferred_element_type=F32)
        logf = _log_sigmoid(zf + bf_ref[...])

        @pl.when(i % tiles_per_batch == 0)
        def _():
            carry_scr[...] = jnp.zeros_like(carry_scr)

        carry = carry_scr[0:1, :]
        for r in range(TM_IN // CUM_BLK):
            rows = slice(r * CUM_BLK, (r + 1) * CUM_BLK)
            parts = jnp.concatenate(_split3(logf[rows]), axis=1)
            cs = jnp.dot(tri_ref[...], parts, preferred_element_type=F32)
            cum = (cs[:, :LANES] + cs[:, LANES:2 * LANES]) + cs[:, 2 * LANES:] + carry
            carry = cum[CUM_BLK - 1:CUM_BLK, :]
            cp_ref[rows, :] = _pack_parts(cum * LOG2E)
        carry_scr[...] = jnp.broadcast_to(carry, carry_scr.shape)

    def project(fn, weights=w_ref):
        for r in range(TM_IN // RC_IN):
            rows = slice(r * RC_IN, (r + 1) * RC_IN)
            acc = jnp.dot(h_scr[rows, :], weights[...], preferred_element_type=F32)
            z_ref[rows, :] = fn(acc).astype(BF16)

    @pl.when(j <= ZB_V)
    def _():
        project(jax.nn.gelu)

    @pl.when(j == ZB_Q)
    def _():
        project(lambda a: a * Q_SCALE)

    @pl.when(jnp.logical_or(j == ZB_K, j == ZB_VA))
    def _():
        project(lambda a: a)

    @pl.when(j >= ZB_GA)
    def _():
        project(jax.nn.sigmoid, wg_ref)


def _inproj(x2, g, w_all, w_gates, w_f, b_f, tri, seq):
    n = x2.shape[0]
    grid = (n // TM_IN, Z_COLS // D_MODEL)
    return pl.pallas_call(
        functools.partial(_inproj_kernel, seq // TM_IN),
        grid=grid,
        in_specs=[
            pl.BlockSpec((TM_IN, D_MODEL), lambda i, j: (i, 0)),
            pl.BlockSpec((1, D_MODEL), lambda i, j: (0, 0)),
            pl.BlockSpec((D_MODEL, D_MODEL), lambda i, j: (0, jnp.minimum(j, ZB_VA))),
            pl.BlockSpec((D_MODEL, D_MODEL), lambda i, j: (0, jnp.maximum(j - ZB_GA, 0))),
            pl.BlockSpec((D_MODEL, LANES), lambda i, j: (0, 0)),
            pl.BlockSpec((1, LANES), lambda i, j: (0, 0)),
            pl.BlockSpec((CUM_BLK, CUM_BLK), lambda i, j: (0, 0)),
        ],
        out_specs=[
            pl.BlockSpec((TM_IN, D_MODEL), lambda i, j: (i, j)),
            pl.BlockSpec((TM_IN, LANES), lambda i, j: (i, 0)),
        ],
        out_shape=[
            jax.ShapeDtypeStruct((n, Z_COLS), BF16),
            jax.ShapeDtypeStruct((n, LANES), BF16),
        ],
        scratch_shapes=[
            pltpu.VMEM((TM_IN, D_MODEL), BF16),
            pltpu.VMEM((8, LANES), F32),
        ],
        compiler_params=pltpu.CompilerParams(
            dimension_semantics=("arbitrary", "arbitrary"),
            vmem_limit_bytes=VMEM_LIMIT),
        name="inproj",
    )(x2, g, w_all, w_gates, w_f, b_f, tri)


def _attn_kernel(n_qt, q_ref, k_ref, v_ref, cp_ref, place_ref, crow_ref, eye_ref,
                 o_ref, kaug_scr, vt_scr, qr_scr, s_scr):
    nt = (((1,), (1,)), ((), ()))
    chains = range(ATTN_PAIRS)
    lane = lax.broadcasted_iota(jnp.int32, (TQ, LANES), 1)

    for c in chains:
        lanes = slice(c * LANES, (c + 1) * LANES)
        aug = (jnp.dot(cp_ref[...], place_ref[c], preferred_element_type=F32)
               + crow_ref[0:1, :]).astype(BF16)
        kaug_scr[c, :, :LANES] = k_ref[:, lanes]
        kaug_scr[c, :, LANES:] = aug[:, :LANES]
        ones = jnp.ones((PV_ROWS - FOX_HEAD_DIM, TQ), BF16)
        for t in range(n_qt):
            rows = slice(t * TQ, (t + 1) * TQ)
            vt = lax.dot_general(eye_ref[...], v_ref[rows, lanes], nt,
                                 preferred_element_type=F32).astype(BF16)
            for hh in range(2):
                vt_scr[c, t, hh, :FOX_HEAD_DIM, :] = vt[hh * FOX_HEAD_DIM:(hh + 1) * FOX_HEAD_DIM]
                vt_scr[c, t, hh, FOX_HEAD_DIM:, :] = ones
            q = q_ref[rows, lanes]
            zero = jnp.zeros_like(q)
            for hh in range(2):
                own = (lane >= FOX_HEAD_DIM) if hh else (lane < FOX_HEAD_DIM)
                qrows = slice(hh * TQ, (hh + 1) * TQ)
                qr_scr[c, t, qrows, :LANES] = jnp.where(own, q, zero)
                qr_scr[c, t, qrows, LANES:] = aug[rows, (1 + hh) * LANES:(2 + hh) * LANES]

    kpos = lax.broadcasted_iota(jnp.int32, (TQ, 2 * TQ), 0)
    col = lax.broadcasted_iota(jnp.int32, (TQ, 2 * TQ), 1)
    causal = kpos <= jnp.where(col >= TQ, col - TQ, col)
    def pass1(qi):
        nk = (qi + 1) * TQ
        smax = []
        for c in chains:
            s = lax.dot_general(kaug_scr[c, :nk, :], qr_scr[c, qi], nt,
                                preferred_element_type=F32)
            diag = jnp.where(causal, s[nk - TQ:], -1e30)
            m = jnp.max(diag, axis=0, keepdims=True)
            if qi:
                s_scr[qi % 2, c, :nk - TQ, :] = s[:nk - TQ]
                m = jnp.maximum(m, jnp.max(s[:nk - TQ], axis=0, keepdims=True))
            s_scr[qi % 2, c, nk - TQ:nk, :] = diag
            smax.append(m)
        return smax

    def pass2(qi, smax):
        for c in chains:
            acc = [None, None]
            for t in range(qi + 1):
                p = jnp.exp2(s_scr[qi % 2, c, t * TQ:(t + 1) * TQ, :] - smax[c]).astype(BF16)
                for hh in range(2):
                    d = jnp.dot(vt_scr[c, t, hh], p[:, hh * TQ:(hh + 1) * TQ],
                                preferred_element_type=F32)
                    acc[hh] = d if acc[hh] is None else acc[hh] + d
            heads = [a[:FOX_HEAD_DIM] * (1.0 / a[FOX_HEAD_DIM:FOX_HEAD_DIM + 1]) for a in acc]
            ot = jnp.concatenate(heads, axis=0)
            o_ref[qi * TQ:(qi + 1) * TQ, c * LANES:(c + 1) * LANES] = ot.T.astype(BF16)

    smax = pass1(0)
    for qi in range(n_qt):
        nxt = pass1(qi + 1) if qi + 1 < n_qt else None
        pass2(qi, smax)
        smax = nxt


def _attn_consts():
    place = np.zeros((HEAD_PAIRS, LANES, 3 * LANES), np.float32)
    crow = np.zeros((8, 3 * LANES), np.float32)
    for p in range(HEAD_PAIRS):
        for a in range(N_PARTS):
            place[p, a * FOX_HEADS + 2 * p, 0 + a] = -1.0
            place[p, a * FOX_HEADS + 2 * p + 1, 6 + a] = -1.0
            place[p, a * FOX_HEADS + 2 * p, LANES + 3 + a] = 1.0
            place[p, a * FOX_HEADS + 2 * p + 1, 2 * LANES + 9 + a] = 1.0
    crow[0, 3:6] = 1.0
    crow[0, 9:12] = 1.0
    crow[0, LANES + 0:LANES + 3] = 1.0
    crow[0, 2 * LANES + 6:2 * LANES + 9] = 1.0
    eye = np.eye(LANES, dtype=np.float32)
    return jnp.asarray(place, BF16), jnp.asarray(crow, F32), jnp.asarray(eye, BF16)


def _attn(z, cparts, batch, seq):
    n = z.shape[0]
    n_qt = seq // TQ
    place, crow, eye = _attn_consts()
    width = ATTN_PAIRS * LANES
    qcol = ZB_Q * D_MODEL // width
    kcol = ZB_K * D_MODEL // width
    vcol = ZB_VA * D_MODEL // width
    return pl.pallas_call(
        functools.partial(_attn_kernel, n_qt),
        grid=(batch, HEAD_PAIRS // ATTN_PAIRS),
        in_specs=[
            pl.BlockSpec((seq, width), lambda b, g: (b, qcol + g)),
            pl.BlockSpec((seq, width), lambda b, g: (b, kcol + g)),
            pl.BlockSpec((seq, width), lambda b, g: (b, vcol + g)),
            pl.BlockSpec((seq, LANES), lambda b, g: (b, 0)),
            pl.BlockSpec((ATTN_PAIRS, LANES, 3 * LANES), lambda b, g: (g, 0, 0)),
            pl.BlockSpec((8, 3 * LANES), lambda b, g: (0, 0)),
            pl.BlockSpec((LANES, LANES), lambda b, g: (0, 0)),
        ],
        out_specs=pl.BlockSpec((seq, width), lambda b, g: (b, g)),
        out_shape=jax.ShapeDtypeStruct((n, FOX_WIDTH), BF16),
        scratch_shapes=[
            pltpu.VMEM((ATTN_PAIRS, seq, 2 * LANES), BF16),
            pltpu.VMEM((ATTN_PAIRS, n_qt, 2, PV_ROWS, TQ), BF16),
            pltpu.VMEM((ATTN_PAIRS, n_qt, 2 * TQ, 2 * LANES), BF16),
            pltpu.VMEM((2, ATTN_PAIRS, seq, 2 * TQ), F32),
        ],
        compiler_params=pltpu.CompilerParams(
            dimension_semantics=("arbitrary", "arbitrary"),
            vmem_limit_bytes=VMEM_LIMIT),
        name="attn",
    )(z, z, z, cparts, place, crow, eye)


def _mix_kernel(u_ref, v_ref, ga_ref, gb_ref, o_ref, x_ref, lng_ref, lnb_ref, ws_ref,
                bias_ref, wa_ref, wb_ref, wo_ref, g2_ref, x1_ref, h2_ref, vn_scr, a_scr):
    v = v_ref[...].astype(F32)
    mu = jnp.mean(v, axis=-1, keepdims=True)
    vc = v - mu
    var = jnp.mean(vc * vc, axis=-1, keepdims=True)
    vn_scr[...] = (vc * lax.rsqrt(var + EPS) * lng_ref[...] + lnb_ref[...]).astype(BF16)

    t_idx = lax.broadcasted_iota(jnp.int32, (GMLP_BLOCK, GMLP_BLOCK), 0)
    s_idx = lax.broadcasted_iota(jnp.int32, (GMLP_BLOCK, GMLP_BLOCK), 1)
    causal = (s_idx // CHUNK) <= (t_idx // CHUNK)
    for g in range(GMLP_GROUPS):
        cols = slice(g * GMLP_BLOCK, (g + 1) * GMLP_BLOCK)
        wg = jnp.where(causal, ws_ref[g], jnp.zeros_like(ws_ref[g]))
        for r in range(TM_MIX // GMLP_BLOCK):
            rows = slice(r * GMLP_BLOCK, (r + 1) * GMLP_BLOCK)
            mixed = jnp.dot(wg, vn_scr[rows, cols], preferred_element_type=F32) + bias_ref[:, cols]
            a_scr[rows, cols] = (u_ref[rows, cols].astype(F32) * mixed).astype(BF16)

    ya = jnp.dot(a_scr[...], wa_ref[...], preferred_element_type=F32)
    yb = jnp.dot(o_ref[...], wb_ref[...], preferred_element_type=F32)
    merged = ga_ref[...].astype(F32) * ya + gb_ref[...].astype(F32) * yb
    x1 = x_ref[...] + jnp.dot(merged.astype(BF16), wo_ref[...], preferred_element_type=F32)
    x1_ref[...] = x1
    h2_ref[...] = _rms(x1, g2_ref[...]).astype(BF16)


def _mix(z, o, x2, ln_g, ln_b, w_s, bias_full, w_a, w_b, w_o, g2):
    n = x2.shape[0]
    row = lambda c: pl.BlockSpec((TM_MIX, D_MODEL), lambda i, c=c: (i, c))
    const = lambda shape: pl.BlockSpec(shape, lambda i: (0,) * len(shape))
    return pl.pallas_call(
        _mix_kernel,
        grid=(n // TM_MIX,),
        in_specs=[
            row(ZB_U), row(ZB_V), row(ZB_GA), row(ZB_GB), row(0), row(0),
            const((1, GMLP_WIDTH)), const((1, GMLP_WIDTH)),
            const((GMLP_GROUPS, GMLP_BLOCK, GMLP_BLOCK)),
            const((GMLP_BLOCK, GMLP_WIDTH)),
            const((GMLP_WIDTH, D_MODEL)), const((FOX_WIDTH, D_MODEL)),
            const((D_MODEL, D_MODEL)), const((1, D_MODEL)),
        ],
        out_specs=[row(0), row(0)],
        out_shape=[
            jax.ShapeDtypeStruct((n, D_MODEL), F32),
            jax.ShapeDtypeStruct((n, D_MODEL), BF16),
        ],
        scratch_shapes=[
            pltpu.VMEM((TM_MIX, GMLP_WIDTH), BF16),
            pltpu.VMEM((TM_MIX, GMLP_WIDTH), BF16),
        ],
        compiler_params=pltpu.CompilerParams(
            dimension_semantics=("arbitrary",),
            vmem_limit_bytes=VMEM_LIMIT),
        name="mix",
    )(z, z, z, z, o, x2, ln_g, ln_b, w_s, bias_full, w_a, w_b, w_o, g2)


def _ffn_kernel(tiles_per_batch, n_fc, h2_ref, halo_ref, x1_ref, p_ref, wug_ref, wul_ref,
                cwg_ref, cwl_ref, cbg_ref, cbl_ref, wd_ref, g3_ref, wpg_ref, wple_ref,
                gf_ref, out_ref, ug_scr, ul_scr, act_scr, acc_scr):
    i = pl.program_id(0)
    j = pl.program_id(1)

    halo = halo_ref[...]
    halo = jnp.where(i % tiles_per_batch == 0, jnp.zeros_like(halo), halo)
    hext = jnp.concatenate([halo, h2_ref[...]], axis=0)
    ug_scr[...] = jnp.dot(hext, wug_ref[...], preferred_element_type=F32)
    ul_scr[...] = jnp.dot(hext, wul_ref[...], preferred_element_type=F32)

    def conv(scr, cw_ref, cb_ref, cols):
        out = cb_ref[:, cols] + cw_ref[0:1, cols] * scr[pl.ds(HALO - 2, TM_FFN), cols]
        for t in range(1, CONV_WIDTH):
            out = out + cw_ref[t:t + 1, cols] * scr[pl.ds(HALO - 2 + t, TM_FFN), cols]
        return out

    for c in range(FC // LANES):
        cols = slice(c * LANES, (c + 1) * LANES)
        gate = conv(ug_scr, cwg_ref, cbg_ref, cols)
        lin = conv(ul_scr, cwl_ref, cbl_ref, cols)
        act_scr[:, cols] = (jax.nn.gelu(gate) * lin).astype(BF16)

    down = jnp.dot(act_scr[...], wd_ref[...], preferred_element_type=F32)

    @pl.when(j == 0)
    def _():
        acc_scr[...] = down

    @pl.when(j > 0)
    def _():
        acc_scr[...] += down

    @pl.when(j == n_fc - 1)
    def _():
        x2 = x1_ref[...] + acc_scr[...]
        h3 = _rms(x2, g3_ref[...]).astype(BF16)
        gate = jax.nn.sigmoid(jnp.dot(h3, wpg_ref[...], preferred_element_type=F32))
        ple = jnp.dot(p_ref[...].astype(BF16), wple_ref[...], preferred_element_type=F32)
        x3 = x2 + ple * gate
        out_ref[...] = _rms(x3, gf_ref[...])


def _ffn(h2, x1, p2, w_up, conv_w, conv_b, w_down, g3, w_pg, w_ple, gf, seq):
    n = x1.shape[0]
    n_fc = D_FF // FC
    halo_blocks = TM_FFN // HALO
    return pl.pallas_call(
        functools.partial(_ffn_kernel, seq // TM_FFN, n_fc),
        grid=(n // TM_FFN, n_fc),
        in_specs=[
            pl.BlockSpec((TM_FFN, D_MODEL), lambda i, j: (i, 0)),
            pl.BlockSpec((HALO, D_MODEL), lambda i, j: (jnp.maximum(i * halo_blocks - 1, 0), 0)),
            pl.BlockSpec((TM_FFN, D_MODEL), lambda i, j: (i, 0)),
            pl.BlockSpec((TM_FFN, PLE_DIM), lambda i, j: (i, 0)),
            pl.BlockSpec((D_MODEL, FC), lambda i, j: (0, j)),
            pl.BlockSpec((D_MODEL, FC), lambda i, j: (0, n_fc + j)),
            pl.BlockSpec((CONV_WIDTH, FC), lambda i, j: (0, j)),
            pl.BlockSpec((CONV_WIDTH, FC), lambda i, j: (0, n_fc + j)),
            pl.BlockSpec((1, FC), lambda i, j: (0, j)),
            pl.BlockSpec((1, FC), lambda i, j: (0, n_fc + j)),
            pl.BlockSpec((FC, D_MODEL), lambda i, j: (j, 0)),
            pl.BlockSpec((1, D_MODEL), lambda i, j: (0, 0)),
            pl.BlockSpec((D_MODEL, D_MODEL), lambda i, j: (0, 0)),
            pl.BlockSpec((PLE_DIM, D_MODEL), lambda i, j: (0, 0)),
            pl.BlockSpec((1, D_MODEL), lambda i, j: (0, 0)),
        ],
        out_specs=pl.BlockSpec((TM_FFN, D_MODEL), lambda i, j: (i, 0)),
        out_shape=jax.ShapeDtypeStruct((n, D_MODEL), F32),
        scratch_shapes=[
            pltpu.VMEM((HALO + TM_FFN, FC), F32),
            pltpu.VMEM((HALO + TM_FFN, FC), F32),
            pltpu.VMEM((TM_FFN, FC), BF16),
            pltpu.VMEM((TM_FFN, D_MODEL), F32),
        ],
        compiler_params=pltpu.CompilerParams(
            dimension_semantics=("arbitrary", "arbitrary"),
            vmem_limit_bytes=VMEM_LIMIT),
        name="ffn",
    )(h2, h2, x1, p2, w_up, w_up, conv_w, conv_w, conv_b, conv_b, w_down, g3, w_pg, w_ple, gf)


def _layer(x2, p2, batch, seq, norm_mix_g, w_in, b_f, gmlp_ln_g, gmlp_ln_b, gmlp_w_s, gmlp_b_s,
           w_branch_a, w_branch_b, w_out, norm_ffn_g, w_up, conv_w, conv_b, w_down,
           norm_ple_g, w_ple, w_ple_gate, out_g):
    o_f = 2 * GMLP_WIDTH + 3 * FOX_WIDTH
    w_all = w_in.astype(BF16)
    w_gates = w_all[:, o_f + FOX_HEADS:]
    w_f = jnp.pad(w_all[:, o_f:o_f + FOX_HEADS], ((0, 0), (0, LANES - FOX_HEADS)))
    b_fp = jnp.pad(b_f, (0, LANES - FOX_HEADS)).reshape(1, LANES)
    tri = jnp.asarray(np.tril(np.ones((CUM_BLK, CUM_BLK), np.float32)), BF16)

    z, cparts = _inproj(x2, norm_mix_g.reshape(1, -1), w_all, w_gates, w_f, b_fp, tri, seq)
    o = _attn(z, cparts, batch, seq)

    bias_full = jnp.repeat(gmlp_b_s.T, GMLP_BLOCK, axis=1)
    x1, h2 = _mix(z, o, x2, gmlp_ln_g.reshape(1, -1), gmlp_ln_b.reshape(1, -1),
                  gmlp_w_s.astype(BF16), bias_full, w_branch_a.astype(BF16),
                  w_branch_b.astype(BF16), w_out.astype(BF16), norm_ffn_g.reshape(1, -1))

    return _ffn(h2, x1, p2, w_up.astype(BF16), conv_w, conv_b.reshape(1, -1),
                w_down.astype(BF16), norm_ple_g.reshape(1, -1), w_ple_gate.astype(BF16),
                w_ple.astype(BF16), out_g.reshape(1, -1), seq)


def kernel(x, p, norm_mix_g, w_in, b_f, gmlp_ln_g, gmlp_ln_b, gmlp_w_s, gmlp_b_s, w_branch_a,
           w_branch_b, w_out, norm_ffn_g, w_up, conv_w, conv_b, w_down, norm_ple_g, w_ple,
           w_ple_gate, norm_final_g):
    batch, seq, d = x.shape
    depth = p.shape[0]
    assert d == D_MODEL and depth == 1, "the fused final norm assumes a single layer"
    assert seq % TM_IN == 0 and seq % TM_FFN == 0 and seq % TQ == 0
    x2 = x.reshape(batch * seq, d)
    out = _layer(x2, p[0].reshape(batch * seq, PLE_DIM), batch, seq,
                 norm_mix_g[0], w_in[0], b_f[0], gmlp_ln_g[0], gmlp_ln_b[0], gmlp_w_s[0],
                 gmlp_b_s[0], w_branch_a[0], w_branch_b[0], w_out[0], norm_ffn_g[0], w_up[0],
                 conv_w[0], conv_b[0], w_down[0], norm_ple_g[0], w_ple[0], w_ple_gate[0],
                 norm_final_g)
    return out.reshape(batch, seq, d)
```

```python
import functools

import numpy as np
import jax
import jax.numpy as jnp
from jax import lax
from jax.experimental import pallas as pl
from jax.experimental.pallas import tpu as pltpu

F32 = jnp.float32
BF16 = jnp.bfloat16

D_MODEL = 1024
CHUNK = 64
PLE_DIM = 256
EPS = 1e-6
GMLP_GROUPS = 8
GMLP_BLOCK = 128
GMLP_WIDTH = 1024
FOX_HEADS = 16
FOX_HEAD_DIM = 64
FOX_WIDTH = 1024
D_FF = 2816
CONV_WIDTH = 3

LANES = 128
BF16_ROWS = 16
HEAD_PAIRS = FOX_HEADS // 2
N_PARTS = 3
LOG2E = 1.4426950408889634
Q_SCALE = FOX_HEAD_DIM ** -0.5 * LOG2E
PV_ROWS = FOX_HEAD_DIM + 16

ZB_U, ZB_V, ZB_Q, ZB_K, ZB_VA, ZB_GA, ZB_GB = range(7)
Z_COLS = 7 * D_MODEL

TM_IN = 2048
RC_IN = 256
CUM_BLK = 256
TQ = 256
ATTN_PAIRS = 2
TM_MIX = 512
TM_FFN = 512
FC = D_FF // 2
HALO = 16

VMEM_LIMIT = 56 * 1024 * 1024


def _rms(x, g):
    return x * lax.rsqrt(jnp.mean(x * x, axis=-1, keepdims=True) + EPS) * g


def _split3(x):
    hi = x.astype(BF16)
    r1 = x - hi.astype(F32)
    mid = r1.astype(BF16)
    lo = (r1 - mid.astype(F32)).astype(BF16)
    return hi, mid, lo


def _pack_parts(x):
    hi, mid, lo = (part.astype(F32) for part in _split3(x))
    lane = lax.broadcasted_iota(jnp.int32, x.shape, 1)
    packed = jnp.where(lane < FOX_HEADS, hi,
                       jnp.where(lane < 2 * FOX_HEADS, pltpu.roll(mid, FOX_HEADS, 1),
                                 pltpu.roll(lo, 2 * FOX_HEADS, 1)))
    return packed.astype(BF16)


def _log_sigmoid(x):
    return jnp.minimum(x, 0.0) - jnp.log1p(jnp.exp(-jnp.abs(x)))


def _inproj_kernel(tiles_per_batch, x_ref, g_ref, w_ref, wg_ref, wf_ref, bf_ref, tri_ref,
                   z_ref, cp_ref, h_scr, carry_scr):
    i = pl.program_id(0)
    j = pl.program_id(1)

    @pl.when(j == 0)
    def _():
        hb = _rms(x_ref[...], g_ref[...]).astype(BF16)
        h_scr[...] = hb
        zf = jnp.dot(hb, wf_ref[...], preferred_element_type=F32)
        logf = _log_sigmoid(zf + bf_ref[...])

        @pl.when(i % tiles_per_batch == 0)
        def _():
            carry_scr[...] = jnp.zeros_like(carry_scr)

        carry = carry_scr[0:1, :]
        for r in range(TM_IN // CUM_BLK):
            rows = slice(r * CUM_BLK, (r + 1) * CUM_BLK)
            parts = jnp.concatenate(_split3(logf[rows]), axis=1)
            cs = jnp.dot(tri_ref[...], parts, preferred_element_type=F32)
            cum = (cs[:, :LANES] + cs[:, LANES:2 * LANES]) + cs[:, 2 * LANES:] + carry
            carry = cum[CUM_BLK - 1:CUM_BLK, :]
            cp_ref[rows, :] = _pack_parts(cum * LOG2E)
        carry_scr[...] = jnp.broadcast_to(carry, carry_scr.shape)

    def project(fn, weights=w_ref):
        for r in range(TM_IN // RC_IN):
            rows = slice(r * RC_IN, (r + 1) * RC_IN)
            acc = jnp.dot(h_scr[rows, :], weights[...], preferred_element_type=F32)
            z_ref[rows, :] = fn(acc).astype(BF16)

    @pl.when(j <= ZB_V)
    def _():
        project(jax.nn.gelu)

    @pl.when(j == ZB_Q)
    def _():
        project(lambda a: a * Q_SCALE)

    @pl.when(jnp.logical_or(j == ZB_K, j == ZB_VA))
    def _():
        project(lambda a: a)

    @pl.when(j >= ZB_GA)
    def _():
        project(jax.nn.sigmoid, wg_ref)


def _inproj(x2, g, w_all, w_gates, w_f, b_f, tri, seq):
    n = x2.shape[0]
    grid = (n // TM_IN, Z_COLS // D_MODEL)
    return pl.pallas_call(
        functools.partial(_inproj_kernel, seq // TM_IN),
        grid=grid,
        in_specs=[
            pl.BlockSpec((TM_IN, D_MODEL), lambda i, j: (i, 0)),
            pl.BlockSpec((1, D_MODEL), lambda i, j: (0, 0)),
            pl.BlockSpec((D_MODEL, D_MODEL), lambda i, j: (0, jnp.minimum(j, ZB_VA))),
            pl.BlockSpec((D_MODEL, D_MODEL), lambda i, j: (0, jnp.maximum(j - ZB_GA, 0))),
            pl.BlockSpec((D_MODEL, LANES), lambda i, j: (0, 0)),
            pl.BlockSpec((1, LANES), lambda i, j: (0, 0)),
            pl.BlockSpec((CUM_BLK, CUM_BLK), lambda i, j: (0, 0)),
        ],
        out_specs=[
            pl.BlockSpec((TM_IN, D_MODEL), lambda i, j: (i, j)),
            pl.BlockSpec((TM_IN, LANES), lambda i, j: (i, 0)),
        ],
        out_shape=[
            jax.ShapeDtypeStruct((n, Z_COLS), BF16),
            jax.ShapeDtypeStruct((n, LANES), BF16),
        ],
        scratch_shapes=[
            pltpu.VMEM((TM_IN, D_MODEL), BF16),
            pltpu.VMEM((8, LANES), F32),
        ],
        compiler_params=pltpu.CompilerParams(
            dimension_semantics=("arbitrary", "arbitrary"),
            vmem_limit_bytes=VMEM_LIMIT),
        name="inproj",
    )(x2, g, w_all, w_gates, w_f, b_f, tri)


def _attn_kernel(n_qt, n_cast, q_ref, k_ref, v_ref, cp_ref, place_ref, crow_ref, eye_ref, *rest):
    cast_in, (o_ref, *cast_out) = rest[:n_cast], rest[n_cast:2 * n_cast + 1]
    kaug_scr, vt_scr, qr_scr, s_scr = rest[2 * n_cast + 1:]
    for src, dst in zip(cast_in, cast_out):
        dst[...] = src[...].astype(BF16)

    nt = (((1,), (1,)), ((), ()))
    chains = range(ATTN_PAIRS)
    lane = lax.broadcasted_iota(jnp.int32, (TQ, LANES), 1)

    for c in chains:
        lanes = slice(c * LANES, (c + 1) * LANES)
        aug = (jnp.dot(cp_ref[...], place_ref[c], preferred_element_type=F32)
               + crow_ref[0:1, :]).astype(BF16)
        kaug_scr[c, :, :LANES] = k_ref[:, lanes]
        kaug_scr[c, :, LANES:] = aug[:, :LANES]
        ones = jnp.ones((PV_ROWS - FOX_HEAD_DIM, TQ), BF16)
        for t in range(n_qt):
            rows = slice(t * TQ, (t + 1) * TQ)
            vt = lax.dot_general(eye_ref[...], v_ref[rows, lanes], nt,
                                 preferred_element_type=F32).astype(BF16)
            for hh in range(2):
                vt_scr[c, t, hh, :FOX_HEAD_DIM, :] = vt[hh * FOX_HEAD_DIM:(hh + 1) * FOX_HEAD_DIM]
                vt_scr[c, t, hh, FOX_HEAD_DIM:, :] = ones
            q = q_ref[rows, lanes]
            zero = jnp.zeros_like(q)
            for hh in range(2):
                own = (lane >= FOX_HEAD_DIM) if hh else (lane < FOX_HEAD_DIM)
                qrows = slice(hh * TQ, (hh + 1) * TQ)
                qr_scr[c, t, qrows, :LANES] = jnp.where(own, q, zero)
                qr_scr[c, t, qrows, LANES:] = aug[rows, (1 + hh) * LANES:(2 + hh) * LANES]

    kpos = lax.broadcasted_iota(jnp.int32, (TQ, 2 * TQ), 0)
    col = lax.broadcasted_iota(jnp.int32, (TQ, 2 * TQ), 1)
    causal = kpos <= jnp.where(col >= TQ, col - TQ, col)
    def pass1(qi):
        nk = (qi + 1) * TQ
        smax = []
        for c in chains:
            s = lax.dot_general(kaug_scr[c, :nk, :], qr_scr[c, qi], nt,
                                preferred_element_type=F32)
            diag = jnp.where(causal, s[nk - TQ:], -1e30)
            m = jnp.max(diag, axis=0, keepdims=True)
            if qi:
                s_scr[qi % 2, c, :nk - TQ, :] = s[:nk - TQ]
                m = jnp.maximum(m, jnp.max(s[:nk - TQ], axis=0, keepdims=True))
            s_scr[qi % 2, c, nk - TQ:nk, :] = diag
            smax.append(m)
        return smax

    def pass2(qi, smax):
        for c in chains:
            acc = [None, None]
            for t in range(qi + 1):
                p = jnp.exp2(s_scr[qi % 2, c, t * TQ:(t + 1) * TQ, :] - smax[c]).astype(BF16)
                for hh in range(2):
                    d = jnp.dot(vt_scr[c, t, hh], p[:, hh * TQ:(hh + 1) * TQ],
                                preferred_element_type=F32)
                    acc[hh] = d if acc[hh] is None else acc[hh] + d
            heads = [a[:FOX_HEAD_DIM] * (1.0 / a[FOX_HEAD_DIM:FOX_HEAD_DIM + 1]) for a in acc]
            ot = jnp.concatenate(heads, axis=0)
            o_ref[qi * TQ:(qi + 1) * TQ, c * LANES:(c + 1) * LANES] = ot.T.astype(BF16)

    smax = pass1(0)
    for qi in range(n_qt):
        nxt = pass1(qi + 1) if qi + 1 < n_qt else None
        pass2(qi, smax)
        smax = nxt


def _attn_consts():
    place = np.zeros((HEAD_PAIRS, LANES, 3 * LANES), np.float32)
    crow = np.zeros((8, 3 * LANES), np.float32)
    for p in range(HEAD_PAIRS):
        for a in range(N_PARTS):
            place[p, a * FOX_HEADS + 2 * p, 0 + a] = -1.0
            place[p, a * FOX_HEADS + 2 * p + 1, 6 + a] = -1.0
            place[p, a * FOX_HEADS + 2 * p, LANES + 3 + a] = 1.0
            place[p, a * FOX_HEADS + 2 * p + 1, 2 * LANES + 9 + a] = 1.0
    crow[0, 3:6] = 1.0
    crow[0, 9:12] = 1.0
    crow[0, LANES + 0:LANES + 3] = 1.0
    crow[0, 2 * LANES + 6:2 * LANES + 9] = 1.0
    eye = np.eye(LANES, dtype=np.float32)
    return jnp.asarray(place, BF16), jnp.asarray(crow, F32), jnp.asarray(eye, BF16)


def _attn(z, cparts, batch, seq, weights):
    n = z.shape[0]
    n_qt = seq // TQ
    place, crow, eye = _attn_consts()
    n_groups = HEAD_PAIRS // ATTN_PAIRS
    cast_in_specs, cast_out_specs, cast_out_shapes = [], [], []
    for w in weights:
        rows, cols = w.shape
        n_slabs = max(s for s in range(1, batch * n_groups + 1)
                      if rows % s == 0 and (rows // s) % BF16_ROWS == 0)
        slab = rows // n_slabs
        index = lambda b, g, last=n_slabs - 1: (jnp.minimum(b * n_groups + g, last), 0)
        cast_in_specs.append(pl.BlockSpec((slab, cols), index))
        cast_out_specs.append(pl.BlockSpec((slab, cols), index))
        cast_out_shapes.append(jax.ShapeDtypeStruct(w.shape, BF16))
    width = ATTN_PAIRS * LANES
    qcol = ZB_Q * D_MODEL // width
    kcol = ZB_K * D_MODEL // width
    vcol = ZB_VA * D_MODEL // width
    outs = pl.pallas_call(
        functools.partial(_attn_kernel, n_qt, len(weights)),
        grid=(batch, n_groups),
        in_specs=[
            pl.BlockSpec((seq, width), lambda b, g: (b, qcol + g)),
            pl.BlockSpec((seq, width), lambda b, g: (b, kcol + g)),
            pl.BlockSpec((seq, width), lambda b, g: (b, vcol + g)),
            pl.BlockSpec((seq, LANES), lambda b, g: (b, 0)),
            pl.BlockSpec((ATTN_PAIRS, LANES, 3 * LANES), lambda b, g: (g, 0, 0)),
            pl.BlockSpec((8, 3 * LANES), lambda b, g: (0, 0)),
            pl.BlockSpec((LANES, LANES), lambda b, g: (0, 0)),
        ] + cast_in_specs,
        out_specs=[pl.BlockSpec((seq, width), lambda b, g: (b, g))] + cast_out_specs,
        out_shape=[jax.ShapeDtypeStruct((n, FOX_WIDTH), BF16)] + cast_out_shapes,
        scratch_shapes=[
            pltpu.VMEM((ATTN_PAIRS, seq, 2 * LANES), BF16),
            pltpu.VMEM((ATTN_PAIRS, n_qt, 2, PV_ROWS, TQ), BF16),
            pltpu.VMEM((ATTN_PAIRS, n_qt, 2 * TQ, 2 * LANES), BF16),
            pltpu.VMEM((2, ATTN_PAIRS, seq, 2 * TQ), F32),
        ],
        compiler_params=pltpu.CompilerParams(
            dimension_semantics=("arbitrary", "arbitrary"),
            vmem_limit_bytes=VMEM_LIMIT),
        name="attn",
    )(z, z, z, cparts, place, crow, eye, *weights)
    return outs[0], outs[1:]


def _mix_kernel(u_ref, v_ref, ga_ref, gb_ref, o_ref, x_ref, lng_ref, lnb_ref, ws_ref,
                bias_ref, wa_ref, wb_ref, wo_ref, g2_ref, x1_ref, h2_ref, vn_scr, a_scr):
    v = v_ref[...].astype(F32)
    mu = jnp.mean(v, axis=-1, keepdims=True)
    vc = v - mu
    var = jnp.mean(vc * vc, axis=-1, keepdims=True)
    vn_scr[...] = (vc * lax.rsqrt(var + EPS) * lng_ref[...] + lnb_ref[...]).astype(BF16)

    t_idx = lax.broadcasted_iota(jnp.int32, (GMLP_BLOCK, GMLP_BLOCK), 0)
    s_idx = lax.broadcasted_iota(jnp.int32, (GMLP_BLOCK, GMLP_BLOCK), 1)
    causal = (s_idx // CHUNK) <= (t_idx // CHUNK)
    for g in range(GMLP_GROUPS):
        cols = slice(g * GMLP_BLOCK, (g + 1) * GMLP_BLOCK)
        wg = jnp.where(causal, ws_ref[g], jnp.zeros_like(ws_ref[g]))
        for r in range(TM_MIX // GMLP_BLOCK):
            rows = slice(r * GMLP_BLOCK, (r + 1) * GMLP_BLOCK)
            mixed = jnp.dot(wg, vn_scr[rows, cols], preferred_element_type=F32) + bias_ref[:, cols]
            a_scr[rows, cols] = (u_ref[rows, cols].astype(F32) * mixed).astype(BF16)

    ya = jnp.dot(a_scr[...], wa_ref[...], preferred_element_type=F32)
    yb = jnp.dot(o_ref[...], wb_ref[...], preferred_element_type=F32)
    merged = ga_ref[...].astype(F32) * ya + gb_ref[...].astype(F32) * yb
    x1 = x_ref[...] + jnp.dot(merged.astype(BF16), wo_ref[...], preferred_element_type=F32)
    x1_ref[...] = x1
    h2_ref[...] = _rms(x1, g2_ref[...]).astype(BF16)


def _mix(z, o, x2, ln_g, ln_b, w_s, bias_full, w_a, w_b, w_o, g2):
    n = x2.shape[0]
    row = lambda c: pl.BlockSpec((TM_MIX, D_MODEL), lambda i, c=c: (i, c))
    const = lambda shape: pl.BlockSpec(shape, lambda i: (0,) * len(shape))
    return pl.pallas_call(
        _mix_kernel,
        grid=(n // TM_MIX,),
        in_specs=[
            row(ZB_U), row(ZB_V), row(ZB_GA), row(ZB_GB), row(0), row(0),
            const((1, GMLP_WIDTH)), const((1, GMLP_WIDTH)),
            const((GMLP_GROUPS, GMLP_BLOCK, GMLP_BLOCK)),
            const((GMLP_BLOCK, GMLP_WIDTH)),
            const((GMLP_WIDTH, D_MODEL)), const((FOX_WIDTH, D_MODEL)),
            const((D_MODEL, D_MODEL)), const((1, D_MODEL)),
        ],
        out_specs=[row(0), row(0)],
        out_shape=[
            jax.ShapeDtypeStruct((n, D_MODEL), F32),
            jax.ShapeDtypeStruct((n, D_MODEL), BF16),
        ],
        scratch_shapes=[
            pltpu.VMEM((TM_MIX, GMLP_WIDTH), BF16),
            pltpu.VMEM((TM_MIX, GMLP_WIDTH), BF16),
        ],
        compiler_params=pltpu.CompilerParams(
            dimension_semantics=("arbitrary",),
            vmem_limit_bytes=VMEM_LIMIT),
        name="mix",
    )(z, z, z, z, o, x2, ln_g, ln_b, w_s, bias_full, w_a, w_b, w_o, g2)


def _ffn_kernel(tiles_per_batch, n_fc, h2_ref, halo_ref, x1_ref, p_ref, wug_ref, wul_ref,
                cwg_ref, cwl_ref, cbg_ref, cbl_ref, wd_ref, g3_ref, wpg_ref, wple_ref,
                gf_ref, out_ref, ug_scr, ul_scr, act_scr, acc_scr):
    i = pl.program_id(0)
    j = pl.program_id(1)

    halo = halo_ref[...]
    halo = jnp.where(i % tiles_per_batch == 0, jnp.zeros_like(halo), halo)
    hext = jnp.concatenate([halo, h2_ref[...]], axis=0)
    ug_scr[...] = jnp.dot(hext, wug_ref[...], preferred_element_type=F32)
    ul_scr[...] = jnp.dot(hext, wul_ref[...], preferred_element_type=F32)

    def conv(scr, cw_ref, cb_ref, cols):
        out = cb_ref[:, cols] + cw_ref[0:1, cols] * scr[pl.ds(HALO - 2, TM_FFN), cols]
        for t in range(1, CONV_WIDTH):
            out = out + cw_ref[t:t + 1, cols] * scr[pl.ds(HALO - 2 + t, TM_FFN), cols]
        return out

    for c in range(FC // LANES):
        cols = slice(c * LANES, (c + 1) * LANES)
        gate = conv(ug_scr, cwg_ref, cbg_ref, cols)
        lin = conv(ul_scr, cwl_ref, cbl_ref, cols)
        act_scr[:, cols] = (jax.nn.gelu(gate) * lin).astype(BF16)

    down = jnp.dot(act_scr[...], wd_ref[...], preferred_element_type=F32)

    @pl.when(j == 0)
    def _():
        acc_scr[...] = down

    @pl.when(j > 0)
    def _():
        acc_scr[...] += down

    @pl.when(j == n_fc - 1)
    def _():
        x2 = x1_ref[...] + acc_scr[...]
        h3 = _rms(x2, g3_ref[...]).astype(BF16)
        gate = jax.nn.sigmoid(jnp.dot(h3, wpg_ref[...], preferred_element_type=F32))
        ple = jnp.dot(p_ref[...].astype(BF16), wple_ref[...], preferred_element_type=F32)
        x3 = x2 + ple * gate
        out_ref[...] = _rms(x3, gf_ref[...])


def _ffn(h2, x1, p2, w_up, conv_w, conv_b, w_down, g3, w_pg, w_ple, gf, seq):
    n = x1.shape[0]
    n_fc = D_FF // FC
    halo_blocks = TM_FFN // HALO
    return pl.pallas_call(
        functools.partial(_ffn_kernel, seq // TM_FFN, n_fc),
        grid=(n // TM_FFN, n_fc),
        in_specs=[
            pl.BlockSpec((TM_FFN, D_MODEL), lambda i, j: (i, 0)),
            pl.BlockSpec((HALO, D_MODEL), lambda i, j: (jnp.maximum(i * halo_blocks - 1, 0), 0)),
            pl.BlockSpec((TM_FFN, D_MODEL), lambda i, j: (i, 0)),
            pl.BlockSpec((TM_FFN, PLE_DIM), lambda i, j: (i, 0)),
            pl.BlockSpec((D_MODEL, FC), lambda i, j: (0, j)),
            pl.BlockSpec((D_MODEL, FC), lambda i, j: (0, n_fc + j)),
            pl.BlockSpec((CONV_WIDTH, FC), lambda i, j: (0, j)),
            pl.BlockSpec((CONV_WIDTH, FC), lambda i, j: (0, n_fc + j)),
            pl.BlockSpec((1, FC), lambda i, j: (0, j)),
            pl.BlockSpec((1, FC), lambda i, j: (0, n_fc + j)),
            pl.BlockSpec((FC, D_MODEL), lambda i, j: (j, 0)),
            pl.BlockSpec((1, D_MODEL), lambda i, j: (0, 0)),
            pl.BlockSpec((D_MODEL, D_MODEL), lambda i, j: (0, 0)),
            pl.BlockSpec((PLE_DIM, D_MODEL), lambda i, j: (0, 0)),
            pl.BlockSpec((1, D_MODEL), lambda i, j: (0, 0)),
        ],
        out_specs=pl.BlockSpec((TM_FFN, D_MODEL), lambda i, j: (i, 0)),
        out_shape=jax.ShapeDtypeStruct((n, D_MODEL), F32),
        scratch_shapes=[
            pltpu.VMEM((HALO + TM_FFN, FC), F32),
            pltpu.VMEM((HALO + TM_FFN, FC), F32),
            pltpu.VMEM((TM_FFN, FC), BF16),
            pltpu.VMEM((TM_FFN, D_MODEL), F32),
        ],
        compiler_params=pltpu.CompilerParams(
            dimension_semantics=("arbitrary", "arbitrary"),
            vmem_limit_bytes=VMEM_LIMIT),
        name="ffn",
    )(h2, h2, x1, p2, w_up, w_up, conv_w, conv_w, conv_b, conv_b, w_down, g3, w_pg, w_ple, gf)


def _layer(x2, p2, batch, seq, norm_mix_g, w_in, b_f, gmlp_ln_g, gmlp_ln_b, gmlp_w_s, gmlp_b_s,
           w_branch_a, w_branch_b, w_out, norm_ffn_g, w_up, conv_w, conv_b, w_down,
           norm_ple_g, w_ple, w_ple_gate, out_g):
    o_f = 2 * GMLP_WIDTH + 3 * FOX_WIDTH
    w_all = w_in.astype(BF16)
    w_gates = w_all[:, o_f + FOX_HEADS:]
    w_f = jnp.pad(w_all[:, o_f:o_f + FOX_HEADS], ((0, 0), (0, LANES - FOX_HEADS)))
    b_fp = jnp.pad(b_f, (0, LANES - FOX_HEADS)).reshape(1, LANES)
    tri = jnp.asarray(np.tril(np.ones((CUM_BLK, CUM_BLK), np.float32)), BF16)

    z, cparts = _inproj(x2, norm_mix_g.reshape(1, -1), w_all, w_gates, w_f, b_fp, tri, seq)
    o, (w_a, w_b, w_o, w_u, w_d, w_pg, w_pl) = _attn(
        z, cparts, batch, seq,
        (w_branch_a, w_branch_b, w_out, w_up, w_down, w_ple_gate, w_ple))

    bias_full = jnp.repeat(gmlp_b_s.T, GMLP_BLOCK, axis=1)
    x1, h2 = _mix(z, o, x2, gmlp_ln_g.reshape(1, -1), gmlp_ln_b.reshape(1, -1),
                  gmlp_w_s.astype(BF16), bias_full, w_a, w_b, w_o, norm_ffn_g.reshape(1, -1))

    return _ffn(h2, x1, p2, w_u, conv_w, conv_b.reshape(1, -1), w_d,
                norm_ple_g.reshape(1, -1), w_pg, w_pl, out_g.reshape(1, -1), seq)


def kernel(x, p, norm_mix_g, w_in, b_f, gmlp_ln_g, gmlp_ln_b, gmlp_w_s, gmlp_b_s, w_branch_a,
           w_branch_b, w_out, norm_ffn_g, w_up, conv_w, conv_b, w_down, norm_ple_g, w_ple,
           w_ple_gate, norm_final_g):
    batch, seq, d = x.shape
    depth = p.shape[0]
    assert d == D_MODEL and depth == 1, "the fused final norm assumes a single layer"
    assert seq % TM_IN == 0 and seq % TM_FFN == 0 and seq % TQ == 0
    x2 = x.reshape(batch * seq, d)
    out = _layer(x2, p[0].reshape(batch * seq, PLE_DIM), batch, seq,
                 norm_mix_g[0], w_in[0], b_f[0], gmlp_ln_g[0], gmlp_ln_b[0], gmlp_w_s[0],
                 gmlp_b_s[0], w_branch_a[0], w_branch_b[0], w_out[0], norm_ffn_g[0], w_up[0],
                 conv_w[0], conv_b[0], w_down[0], norm_ple_g[0], w_ple[0], w_ple_gate[0],
                 norm_final_g)
    return out.reshape(batch, seq, d)
```

```python
import functools

import numpy as np
import jax
import jax.numpy as jnp
from jax import lax
from jax.experimental import pallas as pl
from jax.experimental.pallas import tpu as pltpu

F32 = jnp.float32
BF16 = jnp.bfloat16

D_MODEL = 1024
CHUNK = 64
PLE_DIM = 256
EPS = 1e-6
GMLP_GROUPS = 8
GMLP_BLOCK = 128
GMLP_WIDTH = 1024
FOX_HEADS = 16
FOX_HEAD_DIM = 64
FOX_WIDTH = 1024
D_FF = 2816
CONV_WIDTH = 3

LANES = 128
BF16_ROWS = 16
HEAD_PAIRS = FOX_HEADS // 2
N_PARTS = 3
LOG2E = 1.4426950408889634
Q_SCALE = FOX_HEAD_DIM ** -0.5 * LOG2E
PV_ROWS = FOX_HEAD_DIM + 16
BIAS_LANES = 4 * N_PARTS
Q_BIAS_OFFSET = 16

ZB_U, ZB_V, ZB_Q, ZB_K, ZB_VA, ZB_GA, ZB_GB = range(7)
Z_COLS = 7 * D_MODEL

TM_IN = 2048
RC_IN = 256
CUM_BLK = 256
TQ = 256
ATTN_PAIRS = 2
TM_MIX = 512
TM_FFN = 512
FC = D_FF // 2
HALO = 16

VMEM_LIMIT = 56 * 1024 * 1024


def _rms(x, g):
    return x * lax.rsqrt(jnp.mean(x * x, axis=-1, keepdims=True) + EPS) * g


def _split3(x):
    hi = x.astype(BF16)
    r1 = x - hi.astype(F32)
    mid = r1.astype(BF16)
    lo = (r1 - mid.astype(F32)).astype(BF16)
    return hi, mid, lo


def _pack_parts(x):
    hi, mid, lo = (part.astype(F32) for part in _split3(x))
    lane = lax.broadcasted_iota(jnp.int32, x.shape, 1)
    packed = jnp.where(lane < FOX_HEADS, hi,
                       jnp.where(lane < 2 * FOX_HEADS, pltpu.roll(mid, FOX_HEADS, 1),
                                 pltpu.roll(lo, 2 * FOX_HEADS, 1)))
    return packed.astype(BF16)


def _gelu(x):
    a = -2.0 * np.sqrt(2.0 / np.pi) * LOG2E
    e = jnp.exp2(x * (a + (a * 0.044715) * (x * x)))
    return x * (1.0 / (1.0 + e))


def _log_sigmoid(x):
    return jnp.minimum(x, 0.0) - jnp.log1p(jnp.exp(-jnp.abs(x)))


def _inproj_kernel(tiles_per_batch, x_ref, g_ref, w_ref, wg_ref, wf_ref, bf_ref, tri_ref,
                   z_ref, cp_ref, h_scr, carry_scr):
    i = pl.program_id(0)
    j = pl.program_id(1)

    @pl.when(j == 0)
    def _():
        hb = _rms(x_ref[...], g_ref[...]).astype(BF16)
        h_scr[...] = hb
        zf = jnp.dot(hb, wf_ref[...], preferred_element_type=F32)
        logf = _log_sigmoid(zf + bf_ref[...])

        @pl.when(i % tiles_per_batch == 0)
        def _():
            carry_scr[...] = jnp.zeros_like(carry_scr)

        carry = carry_scr[0:1, :]
        for r in range(TM_IN // CUM_BLK):
            rows = slice(r * CUM_BLK, (r + 1) * CUM_BLK)
            parts = jnp.concatenate(_split3(logf[rows]), axis=1)
            cs = jnp.dot(tri_ref[...], parts, preferred_element_type=F32)
            cum = (cs[:, :LANES] + cs[:, LANES:2 * LANES]) + cs[:, 2 * LANES:] + carry
            carry = cum[CUM_BLK - 1:CUM_BLK, :]
            cp_ref[rows, :] = _pack_parts(cum * LOG2E)
        carry_scr[...] = jnp.broadcast_to(carry, carry_scr.shape)

    def project(fn, weights=w_ref):
        for r in range(TM_IN // RC_IN):
            rows = slice(r * RC_IN, (r + 1) * RC_IN)
            acc = jnp.dot(h_scr[rows, :], weights[...], preferred_element_type=F32)
            z_ref[rows, :] = fn(acc).astype(BF16)

    @pl.when(j <= ZB_V)
    def _():
        project(_gelu)

    @pl.when(j == ZB_Q)
    def _():
        project(lambda a: a * Q_SCALE)

    @pl.when(jnp.logical_or(j == ZB_K, j == ZB_VA))
    def _():
        project(lambda a: a)

    @pl.when(j >= ZB_GA)
    def _():
        project(jax.nn.sigmoid, wg_ref)


def _inproj(x2, g, w_all, w_gates, w_f, b_f, tri, seq):
    n = x2.shape[0]
    grid = (n // TM_IN, Z_COLS // D_MODEL)
    return pl.pallas_call(
        functools.partial(_inproj_kernel, seq // TM_IN),
        grid=grid,
        in_specs=[
            pl.BlockSpec((TM_IN, D_MODEL), lambda i, j: (i, 0)),
            pl.BlockSpec((1, D_MODEL), lambda i, j: (0, 0)),
            pl.BlockSpec((D_MODEL, D_MODEL), lambda i, j: (0, jnp.minimum(j, ZB_VA))),
            pl.BlockSpec((D_MODEL, D_MODEL), lambda i, j: (0, jnp.maximum(j - ZB_GA, 0))),
            pl.BlockSpec((D_MODEL, LANES), lambda i, j: (0, 0)),
            pl.BlockSpec((1, LANES), lambda i, j: (0, 0)),
            pl.BlockSpec((CUM_BLK, CUM_BLK), lambda i, j: (0, 0)),
        ],
        out_specs=[
            pl.BlockSpec((TM_IN, D_MODEL), lambda i, j: (i, j)),
            pl.BlockSpec((TM_IN, LANES), lambda i, j: (i, 0)),
        ],
        out_shape=[
            jax.ShapeDtypeStruct((n, Z_COLS), BF16),
            jax.ShapeDtypeStruct((n, LANES), BF16),
        ],
        scratch_shapes=[
            pltpu.VMEM((TM_IN, D_MODEL), BF16),
            pltpu.VMEM((8, LANES), F32),
        ],
        compiler_params=pltpu.CompilerParams(
            dimension_semantics=("arbitrary", "arbitrary"),
            vmem_limit_bytes=VMEM_LIMIT),
        name="inproj",
    )(x2, g, w_all, w_gates, w_f, b_f, tri)


def _attn_kernel(n_qt, n_cast, q_ref, k_ref, v_ref, cp_ref, place_ref, crow_ref, eye_ref, *rest):
    cast_in, (o_ref, *cast_out) = rest[:n_cast], rest[n_cast:2 * n_cast + 1]
    kaug_scr, vt_scr, qr_scr, s_scr = rest[2 * n_cast + 1:]
    for src, dst in zip(cast_in, cast_out):
        dst[...] = src[...].astype(BF16)

    nt = (((1,), (1,)), ((), ()))
    chains = range(ATTN_PAIRS)
    lane = lax.broadcasted_iota(jnp.int32, (TQ, LANES), 1)

    for c in chains:
        lanes = slice(c * LANES, (c + 1) * LANES)
        aug = jnp.dot(cp_ref[...], place_ref[c], preferred_element_type=F32) + crow_ref[0:1, :]
        kaug_scr[c, :, :LANES] = k_ref[:, lanes]
        kaug_scr[c, :, LANES:] = aug.astype(BF16)
        bias_lane = lax.broadcasted_iota(jnp.int32, aug.shape, 1) < BIAS_LANES
        augq = [jnp.where(bias_lane, pltpu.roll(aug, LANES - Q_BIAS_OFFSET * (1 + hh), 1), 0.0)
                .astype(BF16) for hh in range(2)]
        ones = jnp.ones((PV_ROWS - FOX_HEAD_DIM, TQ), BF16)
        for t in range(n_qt):
            rows = slice(t * TQ, (t + 1) * TQ)
            vt = lax.dot_general(eye_ref[...], v_ref[rows, lanes], nt,
                                 preferred_element_type=F32).astype(BF16)
            for hh in range(2):
                vt_scr[c, t, hh, :FOX_HEAD_DIM, :] = vt[hh * FOX_HEAD_DIM:(hh + 1) * FOX_HEAD_DIM]
                vt_scr[c, t, hh, FOX_HEAD_DIM:, :] = ones
            q = q_ref[rows, lanes]
            zero = jnp.zeros_like(q)
            for hh in range(2):
                own = (lane >= FOX_HEAD_DIM) if hh else (lane < FOX_HEAD_DIM)
                qrows = slice(hh * TQ, (hh + 1) * TQ)
                qr_scr[c, t, qrows, :LANES] = jnp.where(own, q, zero)
                qr_scr[c, t, qrows, LANES:] = augq[hh][rows]

    kpos = lax.broadcasted_iota(jnp.int32, (TQ, 2 * TQ), 0)
    col = lax.broadcasted_iota(jnp.int32, (TQ, 2 * TQ), 1)
    causal = kpos <= jnp.where(col >= TQ, col - TQ, col)
    def pass1(qi):
        nk = (qi + 1) * TQ
        smax = []
        for c in chains:
            s = lax.dot_general(kaug_scr[c, :nk, :], qr_scr[c, qi], nt,
                                preferred_element_type=F32)
            diag = jnp.where(causal, s[nk - TQ:], -1e30)
            m = jnp.max(diag, axis=0, keepdims=True)
            if qi:
                s_scr[qi % 2, c, :nk - TQ, :] = s[:nk - TQ]
                m = jnp.maximum(m, jnp.max(s[:nk - TQ], axis=0, keepdims=True))
            s_scr[qi % 2, c, nk - TQ:nk, :] = diag
            smax.append(m)
        return smax

    def pass2(qi, smax):
        for c in chains:
            acc = [None, None]
            for t in range(qi + 1):
                p = jnp.exp2(s_scr[qi % 2, c, t * TQ:(t + 1) * TQ, :] - smax[c]).astype(BF16)
                for hh in range(2):
                    d = jnp.dot(vt_scr[c, t, hh], p[:, hh * TQ:(hh + 1) * TQ],
                                preferred_element_type=F32)
                    acc[hh] = d if acc[hh] is None else acc[hh] + d
            heads = [a[:FOX_HEAD_DIM] * (1.0 / a[FOX_HEAD_DIM:FOX_HEAD_DIM + 1]) for a in acc]
            ot = jnp.concatenate(heads, axis=0)
            o_ref[qi * TQ:(qi + 1) * TQ, c * LANES:(c + 1) * LANES] = ot.T.astype(BF16)

    smax = pass1(0)
    for qi in range(n_qt):
        nxt = pass1(qi + 1) if qi + 1 < n_qt else None
        pass2(qi, smax)
        smax = nxt


def _attn_consts():
    place = np.zeros((HEAD_PAIRS, LANES, LANES), np.float32)
    crow = np.zeros((8, LANES), np.float32)
    q0, q1 = Q_BIAS_OFFSET, 2 * Q_BIAS_OFFSET
    for p in range(HEAD_PAIRS):
        for a in range(N_PARTS):
            place[p, a * FOX_HEADS + 2 * p, 0 + a] = -1.0
            place[p, a * FOX_HEADS + 2 * p + 1, 6 + a] = -1.0
            place[p, a * FOX_HEADS + 2 * p, q0 + 3 + a] = 1.0
            place[p, a * FOX_HEADS + 2 * p + 1, q1 + 9 + a] = 1.0
    crow[0, 3:6] = 1.0
    crow[0, 9:12] = 1.0
    crow[0, q0 + 0:q0 + 3] = 1.0
    crow[0, q1 + 6:q1 + 9] = 1.0
    eye = np.eye(LANES, dtype=np.float32)
    return jnp.asarray(place, BF16), jnp.asarray(crow, F32), jnp.asarray(eye, BF16)


def _attn(z, cparts, batch, seq, weights):
    n = z.shape[0]
    n_qt = seq // TQ
    place, crow, eye = _attn_consts()
    n_groups = HEAD_PAIRS // ATTN_PAIRS
    cast_in_specs, cast_out_specs, cast_out_shapes = [], [], []
    for w in weights:
        rows, cols = w.shape
        n_slabs = max(s for s in range(1, batch * n_groups + 1)
                      if rows % s == 0 and (rows // s) % BF16_ROWS == 0)
        slab = rows // n_slabs
        index = lambda b, g, last=n_slabs - 1: (jnp.minimum(b * n_groups + g, last), 0)
        cast_in_specs.append(pl.BlockSpec((slab, cols), index))
        cast_out_specs.append(pl.BlockSpec((slab, cols), index))
        cast_out_shapes.append(jax.ShapeDtypeStruct(w.shape, BF16))
    width = ATTN_PAIRS * LANES
    qcol = ZB_Q * D_MODEL // width
    kcol = ZB_K * D_MODEL // width
    vcol = ZB_VA * D_MODEL // width
    outs = pl.pallas_call(
        functools.partial(_attn_kernel, n_qt, len(weights)),
        grid=(batch, n_groups),
        in_specs=[
            pl.BlockSpec((seq, width), lambda b, g: (b, qcol + g)),
            pl.BlockSpec((seq, width), lambda b, g: (b, kcol + g)),
            pl.BlockSpec((seq, width), lambda b, g: (b, vcol + g)),
            pl.BlockSpec((seq, LANES), lambda b, g: (b, 0)),
            pl.BlockSpec((ATTN_PAIRS, LANES, LANES), lambda b, g: (g, 0, 0)),
            pl.BlockSpec((8, LANES), lambda b, g: (0, 0)),
            pl.BlockSpec((LANES, LANES), lambda b, g: (0, 0)),
        ] + cast_in_specs,
        out_specs=[pl.BlockSpec((seq, width), lambda b, g: (b, g))] + cast_out_specs,
        out_shape=[jax.ShapeDtypeStruct((n, FOX_WIDTH), BF16)] + cast_out_shapes,
        scratch_shapes=[
            pltpu.VMEM((ATTN_PAIRS, seq, 2 * LANES), BF16),
            pltpu.VMEM((ATTN_PAIRS, n_qt, 2, PV_ROWS, TQ), BF16),
            pltpu.VMEM((ATTN_PAIRS, n_qt, 2 * TQ, 2 * LANES), BF16),
            pltpu.VMEM((2, ATTN_PAIRS, seq, 2 * TQ), F32),
        ],
        compiler_params=pltpu.CompilerParams(
            dimension_semantics=("arbitrary", "arbitrary"),
            vmem_limit_bytes=VMEM_LIMIT),
        name="attn",
    )(z, z, z, cparts, place, crow, eye, *weights)
    return outs[0], outs[1:]


def _mix_kernel(u_ref, v_ref, ga_ref, gb_ref, o_ref, x_ref, lng_ref, lnb_ref, ws_ref,
                bias_ref, wa_ref, wb_ref, wo_ref, g2_ref, x1_ref, h2_ref, vn_scr, a_scr):
    v = v_ref[...].astype(F32)
    mu = jnp.mean(v, axis=-1, keepdims=True)
    vc = v - mu
    var = jnp.mean(vc * vc, axis=-1, keepdims=True)
    vn_scr[...] = (vc * lax.rsqrt(var + EPS) * lng_ref[...] + lnb_ref[...]).astype(BF16)

    t_idx = lax.broadcasted_iota(jnp.int32, (GMLP_BLOCK, GMLP_BLOCK), 0)
    s_idx = lax.broadcasted_iota(jnp.int32, (GMLP_BLOCK, GMLP_BLOCK), 1)
    causal = (s_idx // CHUNK) <= (t_idx // CHUNK)
    for g in range(GMLP_GROUPS):
        cols = slice(g * GMLP_BLOCK, (g + 1) * GMLP_BLOCK)
        wg = jnp.where(causal, ws_ref[g], jnp.zeros_like(ws_ref[g]))
        for r in range(TM_MIX // GMLP_BLOCK):
            rows = slice(r * GMLP_BLOCK, (r + 1) * GMLP_BLOCK)
            mixed = jnp.dot(wg, vn_scr[rows, cols], preferred_element_type=F32) + bias_ref[:, cols]
            a_scr[rows, cols] = (u_ref[rows, cols].astype(F32) * mixed).astype(BF16)

    ya = jnp.dot(a_scr[...], wa_ref[...], preferred_element_type=F32)
    yb = jnp.dot(o_ref[...], wb_ref[...], preferred_element_type=F32)
    merged = ga_ref[...].astype(F32) * ya + gb_ref[...].astype(F32) * yb
    x1 = x_ref[...] + jnp.dot(merged.astype(BF16), wo_ref[...], preferred_element_type=F32)
    x1_ref[...] = x1
    h2_ref[...] = _rms(x1, g2_ref[...]).astype(BF16)


def _mix(z, o, x2, ln_g, ln_b, w_s, bias_full, w_a, w_b, w_o, g2):
    n = x2.shape[0]
    row = lambda c: pl.BlockSpec((TM_MIX, D_MODEL), lambda i, c=c: (i, c))
    const = lambda shape: pl.BlockSpec(shape, lambda i: (0,) * len(shape))
    return pl.pallas_call(
        _mix_kernel,
        grid=(n // TM_MIX,),
        in_specs=[
            row(ZB_U), row(ZB_V), row(ZB_GA), row(ZB_GB), row(0), row(0),
            const((1, GMLP_WIDTH)), const((1, GMLP_WIDTH)),
            const((GMLP_GROUPS, GMLP_BLOCK, GMLP_BLOCK)),
            const((GMLP_BLOCK, GMLP_WIDTH)),
            const((GMLP_WIDTH, D_MODEL)), const((FOX_WIDTH, D_MODEL)),
            const((D_MODEL, D_MODEL)), const((1, D_MODEL)),
        ],
        out_specs=[row(0), row(0)],
        out_shape=[
            jax.ShapeDtypeStruct((n, D_MODEL), F32),
            jax.ShapeDtypeStruct((n, D_MODEL), BF16),
        ],
        scratch_shapes=[
            pltpu.VMEM((TM_MIX, GMLP_WIDTH), BF16),
            pltpu.VMEM((TM_MIX, GMLP_WIDTH), BF16),
        ],
        compiler_params=pltpu.CompilerParams(
            dimension_semantics=("arbitrary",),
            vmem_limit_bytes=VMEM_LIMIT),
        name="mix",
    )(z, z, z, z, o, x2, ln_g, ln_b, w_s, bias_full, w_a, w_b, w_o, g2)


def _ffn_kernel(tiles_per_batch, n_fc, h2_ref, halo_ref, x1_ref, p_ref, wug_ref, wul_ref,
                cwg_ref, cwl_ref, cbg_ref, cbl_ref, wd_ref, g3_ref, wpg_ref, wple_ref,
                gf_ref, out_ref, ug_scr, ul_scr, act_scr, acc_scr):
    i = pl.program_id(0)
    j = pl.program_id(1)

    halo = halo_ref[...]
    halo = jnp.where(i % tiles_per_batch == 0, jnp.zeros_like(halo), halo)
    hext = jnp.concatenate([halo, h2_ref[...]], axis=0)
    ug_scr[...] = jnp.dot(hext, wug_ref[...], preferred_element_type=F32)
    ul_scr[...] = jnp.dot(hext, wul_ref[...], preferred_element_type=F32)

    def conv(scr, cw_ref, cb_ref, cols):
        out = cb_ref[:, cols] + cw_ref[0:1, cols] * scr[pl.ds(HALO - 2, TM_FFN), cols]
        for t in range(1, CONV_WIDTH):
            out = out + cw_ref[t:t + 1, cols] * scr[pl.ds(HALO - 2 + t, TM_FFN), cols]
        return out

    for c in range(FC // LANES):
        cols = slice(c * LANES, (c + 1) * LANES)
        gate = conv(ug_scr, cwg_ref, cbg_ref, cols)
        lin = conv(ul_scr, cwl_ref, cbl_ref, cols)
        act_scr[:, cols] = (_gelu(gate) * lin).astype(BF16)

    down = jnp.dot(act_scr[...], wd_ref[...], preferred_element_type=F32)

    @pl.when(j == 0)
    def _():
        acc_scr[...] = down

    @pl.when(j > 0)
    def _():
        acc_scr[...] += down

    @pl.when(j == n_fc - 1)
    def _():
        x2 = x1_ref[...] + acc_scr[...]
        h3 = _rms(x2, g3_ref[...]).astype(BF16)
        gate = jax.nn.sigmoid(jnp.dot(h3, wpg_ref[...], preferred_element_type=F32))
        ple = jnp.dot(p_ref[...].astype(BF16), wple_ref[...], preferred_element_type=F32)
        x3 = x2 + ple * gate
        out_ref[...] = _rms(x3, gf_ref[...])


def _ffn(h2, x1, p2, w_up, conv_w, conv_b, w_down, g3, w_pg, w_ple, gf, seq):
    n = x1.shape[0]
    n_fc = D_FF // FC
    halo_blocks = TM_FFN // HALO
    return pl.pallas_call(
        functools.partial(_ffn_kernel, seq // TM_FFN, n_fc),
        grid=(n // TM_FFN, n_fc),
        in_specs=[
            pl.BlockSpec((TM_FFN, D_MODEL), lambda i, j: (i, 0)),
            pl.BlockSpec((HALO, D_MODEL), lambda i, j: (jnp.maximum(i * halo_blocks - 1, 0), 0)),
            pl.BlockSpec((TM_FFN, D_MODEL), lambda i, j: (i, 0)),
            pl.BlockSpec((TM_FFN, PLE_DIM), lambda i, j: (i, 0)),
            pl.BlockSpec((D_MODEL, FC), lambda i, j: (0, j)),
            pl.BlockSpec((D_MODEL, FC), lambda i, j: (0, n_fc + j)),
            pl.BlockSpec((CONV_WIDTH, FC), lambda i, j: (0, j)),
            pl.BlockSpec((CONV_WIDTH, FC), lambda i, j: (0, n_fc + j)),
            pl.BlockSpec((1, FC), lambda i, j: (0, j)),
            pl.BlockSpec((1, FC), lambda i, j: (0, n_fc + j)),
            pl.BlockSpec((FC, D_MODEL), lambda i, j: (j, 0)),
            pl.BlockSpec((1, D_MODEL), lambda i, j: (0, 0)),
            pl.BlockSpec((D_MODEL, D_MODEL), lambda i, j: (0, 0)),
            pl.BlockSpec((PLE_DIM, D_MODEL), lambda i, j: (0, 0)),
            pl.BlockSpec((1, D_MODEL), lambda i, j: (0, 0)),
        ],
        out_specs=pl.BlockSpec((TM_FFN, D_MODEL), lambda i, j: (i, 0)),
        out_shape=jax.ShapeDtypeStruct((n, D_MODEL), F32),
        scratch_shapes=[
            pltpu.VMEM((HALO + TM_FFN, FC), F32),
            pltpu.VMEM((HALO + TM_FFN, FC), F32),
            pltpu.VMEM((TM_FFN, FC), BF16),
            pltpu.VMEM((TM_FFN, D_MODEL), F32),
        ],
        compiler_params=pltpu.CompilerParams(
            dimension_semantics=("arbitrary", "arbitrary"),
            vmem_limit_bytes=VMEM_LIMIT),
        name="ffn",
    )(h2, h2, x1, p2, w_up, w_up, conv_w, conv_w, conv_b, conv_b, w_down, g3, w_pg, w_ple, gf)


def _layer(x2, p2, batch, seq, norm_mix_g, w_in, b_f, gmlp_ln_g, gmlp_ln_b, gmlp_w_s, gmlp_b_s,
           w_branch_a, w_branch_b, w_out, norm_ffn_g, w_up, conv_w, conv_b, w_down,
           norm_ple_g, w_ple, w_ple_gate, out_g):
    o_f = 2 * GMLP_WIDTH + 3 * FOX_WIDTH
    w_all = w_in.astype(BF16)
    w_gates = w_all[:, o_f + FOX_HEADS:]
    w_f = jnp.pad(w_all[:, o_f:o_f + FOX_HEADS], ((0, 0), (0, LANES - FOX_HEADS)))
    b_fp = jnp.pad(b_f, (0, LANES - FOX_HEADS)).reshape(1, LANES)
    tri = jnp.asarray(np.tril(np.ones((CUM_BLK, CUM_BLK), np.float32)), BF16)

    z, cparts = _inproj(x2, norm_mix_g.reshape(1, -1), w_all, w_gates, w_f, b_fp, tri, seq)
    o, (w_a, w_b, w_o, w_u, w_d, w_pg, w_pl) = _attn(
        z, cparts, batch, seq,
        (w_branch_a, w_branch_b, w_out, w_up, w_down, w_ple_gate, w_ple))

    bias_full = jnp.repeat(gmlp_b_s.T, GMLP_BLOCK, axis=1)
    x1, h2 = _mix(z, o, x2, gmlp_ln_g.reshape(1, -1), gmlp_ln_b.reshape(1, -1),
                  gmlp_w_s.astype(BF16), bias_full, w_a, w_b, w_o, norm_ffn_g.reshape(1, -1))

    return _ffn(h2, x1, p2, w_u, conv_w, conv_b.reshape(1, -1), w_d,
                norm_ple_g.reshape(1, -1), w_pg, w_pl, out_g.reshape(1, -1), seq)


def kernel(x, p, norm_mix_g, w_in, b_f, gmlp_ln_g, gmlp_ln_b, gmlp_w_s, gmlp_b_s, w_branch_a,
           w_branch_b, w_out, norm_ffn_g, w_up, conv_w, conv_b, w_down, norm_ple_g, w_ple,
           w_ple_gate, norm_final_g):
    batch, seq, d = x.shape
    depth = p.shape[0]
    assert d == D_MODEL and depth == 1, "the fused final norm assumes a single layer"
    assert seq % TM_IN == 0 and seq % TM_FFN == 0 and seq % TQ == 0
    x2 = x.reshape(batch * seq, d)
    out = _layer(x2, p[0].reshape(batch * seq, PLE_DIM), batch, seq,
                 norm_mix_g[0], w_in[0], b_f[0], gmlp_ln_g[0], gmlp_ln_b[0], gmlp_w_s[0],
                 gmlp_b_s[0], w_branch_a[0], w_branch_b[0], w_out[0], norm_ffn_g[0], w_up[0],
                 conv_w[0], conv_b[0], w_down[0], norm_ple_g[0], w_ple[0], w_ple_gate[0],
                 norm_final_g)
    return out.reshape(batch, seq, d)
```

```python
import functools

import numpy as np
import jax
import jax.numpy as jnp
from jax import lax
from jax.experimental import pallas as pl
from jax.experimental.pallas import tpu as pltpu

F32 = jnp.float32
BF16 = jnp.bfloat16

D_MODEL = 1024
CHUNK = 64
PLE_DIM = 256
EPS = 1e-6
GMLP_GROUPS = 8
GMLP_BLOCK = 128
GMLP_WIDTH = 1024
FOX_HEADS = 16
FOX_HEAD_DIM = 64
FOX_WIDTH = 1024
D_FF = 2816
CONV_WIDTH = 3

LANES = 128
BF16_ROWS = 16
HEAD_PAIRS = FOX_HEADS // 2
N_PARTS = 3
LOG2E = 1.4426950408889634
Q_SCALE = FOX_HEAD_DIM ** -0.5 * LOG2E
PV_ROWS = FOX_HEAD_DIM + 16
BIAS_LANES = 4 * N_PARTS
Q_BIAS_OFFSET = 16

ZB_U, ZB_V, ZB_Q, ZB_K, ZB_VA, ZB_GA, ZB_GB = range(7)
Z_COLS = 7 * D_MODEL

TM_IN = 2048
RC_IN = 256
CUM_BLK = 256
TQ = 256
ATTN_PAIRS = 2
TM_MIX = 512
TM_FFN = 512
FC = D_FF // 2
HALO = 16

VMEM_LIMIT = 56 * 1024 * 1024


def _rms(x, g):
    return x * lax.rsqrt(jnp.mean(x * x, axis=-1, keepdims=True) + EPS) * g


def _split3(x):
    hi = x.astype(BF16)
    r1 = x - hi.astype(F32)
    mid = r1.astype(BF16)
    lo = (r1 - mid.astype(F32)).astype(BF16)
    return hi, mid, lo


def _pack_parts(x):
    hi, mid, lo = (part.astype(F32) for part in _split3(x))
    lane = lax.broadcasted_iota(jnp.int32, x.shape, 1)
    packed = jnp.where(lane < FOX_HEADS, hi,
                       jnp.where(lane < 2 * FOX_HEADS, pltpu.roll(mid, FOX_HEADS, 1),
                                 pltpu.roll(lo, 2 * FOX_HEADS, 1)))
    return packed.astype(BF16)


def _gelu(x):
    a = -2.0 * np.sqrt(2.0 / np.pi) * LOG2E
    e = jnp.exp2(x * (a + (a * 0.044715) * (x * x)))
    return x * (1.0 / (1.0 + e))


def _log_sigmoid(x):
    return jnp.minimum(x, 0.0) - jnp.log1p(jnp.exp(-jnp.abs(x)))


def _inproj_kernel(tiles_per_batch, x_ref, g_ref, w_ref, wg_ref, wf_ref, bf_ref, tri_ref,
                   z_ref, cp_ref, h_scr, carry_scr):
    i = pl.program_id(0)
    j = pl.program_id(1)

    @pl.when(j == 0)
    def _():
        hb = _rms(x_ref[...], g_ref[...]).astype(BF16)
        h_scr[...] = hb
        zf = jnp.dot(hb, wf_ref[...], preferred_element_type=F32)
        logf = _log_sigmoid(zf + bf_ref[...])

        @pl.when(i % tiles_per_batch == 0)
        def _():
            carry_scr[...] = jnp.zeros_like(carry_scr)

        carry = carry_scr[0:1, :]
        for r in range(TM_IN // CUM_BLK):
            rows = slice(r * CUM_BLK, (r + 1) * CUM_BLK)
            parts = jnp.concatenate(_split3(logf[rows]), axis=1)
            cs = jnp.dot(tri_ref[...], parts, preferred_element_type=F32)
            cum = (cs[:, :LANES] + cs[:, LANES:2 * LANES]) + cs[:, 2 * LANES:] + carry
            carry = cum[CUM_BLK - 1:CUM_BLK, :]
            cp_ref[rows, :] = _pack_parts(cum * LOG2E)
        carry_scr[...] = jnp.broadcast_to(carry, carry_scr.shape)

    def project(fn, weights=w_ref):
        for r in range(TM_IN // RC_IN):
            rows = slice(r * RC_IN, (r + 1) * RC_IN)
            acc = jnp.dot(h_scr[rows, :], weights[...], preferred_element_type=F32)
            z_ref[rows, :] = fn(acc).astype(BF16)

    @pl.when(j <= ZB_V)
    def _():
        project(_gelu)

    @pl.when(j == ZB_Q)
    def _():
        project(lambda a: a * Q_SCALE)

    @pl.when(jnp.logical_or(j == ZB_K, j == ZB_VA))
    def _():
        project(lambda a: a)

    @pl.when(j >= ZB_GA)
    def _():
        project(jax.nn.sigmoid, wg_ref)


def _inproj(x2, g, w_all, w_gates, w_f, b_f, tri, seq):
    n = x2.shape[0]
    grid = (n // TM_IN, Z_COLS // D_MODEL)
    return pl.pallas_call(
        functools.partial(_inproj_kernel, seq // TM_IN),
        grid=grid,
        in_specs=[
            pl.BlockSpec((TM_IN, D_MODEL), lambda i, j: (i, 0)),
            pl.BlockSpec((1, D_MODEL), lambda i, j: (0, 0)),
            pl.BlockSpec((D_MODEL, D_MODEL), lambda i, j: (0, jnp.minimum(j, ZB_VA))),
            pl.BlockSpec((D_MODEL, D_MODEL), lambda i, j: (0, jnp.maximum(j - ZB_GA, 0))),
            pl.BlockSpec((D_MODEL, LANES), lambda i, j: (0, 0)),
            pl.BlockSpec((1, LANES), lambda i, j: (0, 0)),
            pl.BlockSpec((CUM_BLK, CUM_BLK), lambda i, j: (0, 0)),
        ],
        out_specs=[
            pl.BlockSpec((TM_IN, D_MODEL), lambda i, j: (i, j)),
            pl.BlockSpec((TM_IN, LANES), lambda i, j: (i, 0)),
        ],
        out_shape=[
            jax.ShapeDtypeStruct((n, Z_COLS), BF16),
            jax.ShapeDtypeStruct((n, LANES), BF16),
        ],
        scratch_shapes=[
            pltpu.VMEM((TM_IN, D_MODEL), BF16),
            pltpu.VMEM((8, LANES), F32),
        ],
        compiler_params=pltpu.CompilerParams(
            dimension_semantics=("arbitrary", "arbitrary"),
            vmem_limit_bytes=VMEM_LIMIT),
        name="inproj",
    )(x2, g, w_all, w_gates, w_f, b_f, tri)


def _attn_kernel(n_qt, n_cast, q_ref, k_ref, v_ref, cp_ref, place_ref, crow_ref, eye_ref, *rest):
    cast_in, (o_ref, *cast_out) = rest[:n_cast], rest[n_cast:2 * n_cast + 1]
    kaug_scr, vt_scr, qr_scr, s_scr = rest[2 * n_cast + 1:]
    for src, dst in zip(cast_in, cast_out):
        dst[...] = src[...].astype(BF16)

    nt = (((1,), (1,)), ((), ()))
    chains = range(ATTN_PAIRS)
    lane = lax.broadcasted_iota(jnp.int32, (TQ, LANES), 1)

    for c in chains:
        lanes = slice(c * LANES, (c + 1) * LANES)
        aug = jnp.dot(cp_ref[...], place_ref[c], preferred_element_type=F32) + crow_ref[0:1, :]
        kaug_scr[c, :, :LANES] = k_ref[:, lanes]
        kaug_scr[c, :, LANES:] = aug.astype(BF16)
        bias_lane = lax.broadcasted_iota(jnp.int32, aug.shape, 1) < BIAS_LANES
        augq = [jnp.where(bias_lane, pltpu.roll(aug, LANES - Q_BIAS_OFFSET * (1 + hh), 1), 0.0)
                .astype(BF16) for hh in range(2)]
        ones = jnp.ones((PV_ROWS - FOX_HEAD_DIM, TQ), BF16)
        for t in range(n_qt):
            rows = slice(t * TQ, (t + 1) * TQ)
            vt = lax.dot_general(eye_ref[...], v_ref[rows, lanes], nt,
                                 preferred_element_type=F32).astype(BF16)
            for hh in range(2):
                vt_scr[c, t, hh, :FOX_HEAD_DIM, :] = vt[hh * FOX_HEAD_DIM:(hh + 1) * FOX_HEAD_DIM]
                vt_scr[c, t, hh, FOX_HEAD_DIM:, :] = ones
            q = q_ref[rows, lanes]
            zero = jnp.zeros_like(q)
            for hh in range(2):
                own = (lane >= FOX_HEAD_DIM) if hh else (lane < FOX_HEAD_DIM)
                qrows = slice(hh * TQ, (hh + 1) * TQ)
                qr_scr[c, t, qrows, :LANES] = jnp.where(own, q, zero)
                qr_scr[c, t, qrows, LANES:] = augq[hh][rows]

    kpos = lax.broadcasted_iota(jnp.int32, (TQ, 2 * TQ), 0)
    col = lax.broadcasted_iota(jnp.int32, (TQ, 2 * TQ), 1)
    causal = kpos <= jnp.where(col >= TQ, col - TQ, col)
    def pass1(qi):
        nk = (qi + 1) * TQ
        smax = []
        for c in chains:
            s = lax.dot_general(kaug_scr[c, :nk, :], qr_scr[c, qi], nt,
                                preferred_element_type=F32)
            diag = jnp.where(causal, s[nk - TQ:], -1e30)
            m = jnp.max(diag, axis=0, keepdims=True)
            if qi:
                s_scr[qi % 2, c, :nk - TQ, :] = s[:nk - TQ]
                m = jnp.maximum(m, jnp.max(s[:nk - TQ], axis=0, keepdims=True))
            s_scr[qi % 2, c, nk - TQ:nk, :] = diag
            smax.append(m)
        return smax

    def pass2(qi, smax):
        for c in chains:
            acc = [None, None]
            for t in range(qi + 1):
                p = jnp.exp2(s_scr[qi % 2, c, t * TQ:(t + 1) * TQ, :] - smax[c]).astype(BF16)
                for hh in range(2):
                    d = jnp.dot(vt_scr[c, t, hh], p[:, hh * TQ:(hh + 1) * TQ],
                                preferred_element_type=F32)
                    acc[hh] = d if acc[hh] is None else acc[hh] + d
            heads = [a[:FOX_HEAD_DIM] * (1.0 / a[FOX_HEAD_DIM:FOX_HEAD_DIM + 1]) for a in acc]
            ot = jnp.concatenate(heads, axis=0)
            o_ref[qi * TQ:(qi + 1) * TQ, c * LANES:(c + 1) * LANES] = ot.T.astype(BF16)

    smax = pass1(0)
    for qi in range(n_qt):
        nxt = pass1(qi + 1) if qi + 1 < n_qt else None
        pass2(qi, smax)
        smax = nxt


def _attn_consts():
    place = np.zeros((HEAD_PAIRS, LANES, LANES), np.float32)
    crow = np.zeros((8, LANES), np.float32)
    q0, q1 = Q_BIAS_OFFSET, 2 * Q_BIAS_OFFSET
    for p in range(HEAD_PAIRS):
        for a in range(N_PARTS):
            place[p, a * FOX_HEADS + 2 * p, 0 + a] = -1.0
            place[p, a * FOX_HEADS + 2 * p + 1, 6 + a] = -1.0
            place[p, a * FOX_HEADS + 2 * p, q0 + 3 + a] = 1.0
            place[p, a * FOX_HEADS + 2 * p + 1, q1 + 9 + a] = 1.0
    crow[0, 3:6] = 1.0
    crow[0, 9:12] = 1.0
    crow[0, q0 + 0:q0 + 3] = 1.0
    crow[0, q1 + 6:q1 + 9] = 1.0
    eye = np.eye(LANES, dtype=np.float32)
    return jnp.asarray(place, BF16), jnp.asarray(crow, F32), jnp.asarray(eye, BF16)


def _attn(z, cparts, batch, seq, weights):
    n = z.shape[0]
    n_qt = seq // TQ
    place, crow, eye = _attn_consts()
    n_groups = HEAD_PAIRS // ATTN_PAIRS
    cast_in_specs, cast_out_specs, cast_out_shapes = [], [], []
    for w in weights:
        rows, cols = w.shape
        n_slabs = max(s for s in range(1, batch * n_groups + 1)
                      if rows % s == 0 and (rows // s) % BF16_ROWS == 0)
        slab = rows // n_slabs
        index = lambda b, g, last=n_slabs - 1: (jnp.minimum(b * n_groups + g, last), 0)
        cast_in_specs.append(pl.BlockSpec((slab, cols), index))
        cast_out_specs.append(pl.BlockSpec((slab, cols), index))
        cast_out_shapes.append(jax.ShapeDtypeStruct(w.shape, BF16))
    width = ATTN_PAIRS * LANES
    qcol = ZB_Q * D_MODEL // width
    kcol = ZB_K * D_MODEL // width
    vcol = ZB_VA * D_MODEL // width
    outs = pl.pallas_call(
        functools.partial(_attn_kernel, n_qt, len(weights)),
        grid=(batch, n_groups),
        in_specs=[
            pl.BlockSpec((seq, width), lambda b, g: (b, qcol + g)),
            pl.BlockSpec((seq, width), lambda b, g: (b, kcol + g)),
            pl.BlockSpec((seq, width), lambda b, g: (b, vcol + g)),
            pl.BlockSpec((seq, LANES), lambda b, g: (b, 0)),
            pl.BlockSpec((ATTN_PAIRS, LANES, LANES), lambda b, g: (g, 0, 0)),
            pl.BlockSpec((8, LANES), lambda b, g: (0, 0)),
            pl.BlockSpec((LANES, LANES), lambda b, g: (0, 0)),
        ] + cast_in_specs,
        out_specs=[pl.BlockSpec((seq, width), lambda b, g: (b, g))] + cast_out_specs,
        out_shape=[jax.ShapeDtypeStruct((n, FOX_WIDTH), BF16)] + cast_out_shapes,
        scratch_shapes=[
            pltpu.VMEM((ATTN_PAIRS, seq, 2 * LANES), BF16),
            pltpu.VMEM((ATTN_PAIRS, n_qt, 2, PV_ROWS, TQ), BF16),
            pltpu.VMEM((ATTN_PAIRS, n_qt, 2 * TQ, 2 * LANES), BF16),
            pltpu.VMEM((2, ATTN_PAIRS, seq, 2 * TQ), F32),
        ],
        compiler_params=pltpu.CompilerParams(
            dimension_semantics=("arbitrary", "arbitrary"),
            vmem_limit_bytes=VMEM_LIMIT),
        name="attn",
    )(z, z, z, cparts, place, crow, eye, *weights)
    return outs[0], outs[1:]


def _mix_kernel(u_ref, v_ref, ga_ref, gb_ref, o_ref, x_ref, lng_ref, lnb_ref, ws_ref,
                bias_ref, wa_ref, wb_ref, wo_ref, g2_ref, x1_ref, h2_ref, vn_scr, a_scr):
    v = v_ref[...].astype(F32)
    mu = jnp.mean(v, axis=-1, keepdims=True)
    vc = v - mu
    var = jnp.mean(vc * vc, axis=-1, keepdims=True)
    vn_scr[...] = (vc * lax.rsqrt(var + EPS) * lng_ref[...] + lnb_ref[...]).astype(BF16)

    t_idx = lax.broadcasted_iota(jnp.int32, (GMLP_BLOCK, GMLP_BLOCK), 0)
    s_idx = lax.broadcasted_iota(jnp.int32, (GMLP_BLOCK, GMLP_BLOCK), 1)
    causal = (s_idx // CHUNK) <= (t_idx // CHUNK)
    for g in range(GMLP_GROUPS):
        cols = slice(g * GMLP_BLOCK, (g + 1) * GMLP_BLOCK)
        wg = jnp.where(causal, ws_ref[g], jnp.zeros_like(ws_ref[g]))
        for r in range(TM_MIX // GMLP_BLOCK):
            rows = slice(r * GMLP_BLOCK, (r + 1) * GMLP_BLOCK)
            mixed = jnp.dot(wg, vn_scr[rows, cols], preferred_element_type=F32) + bias_ref[:, cols]
            a_scr[rows, cols] = (u_ref[rows, cols].astype(F32) * mixed).astype(BF16)

    ya = jnp.dot(a_scr[...], wa_ref[...], preferred_element_type=F32)
    yb = jnp.dot(o_ref[...], wb_ref[...], preferred_element_type=F32)
    merged = ga_ref[...].astype(F32) * ya + gb_ref[...].astype(F32) * yb
    x1 = x_ref[...] + jnp.dot(merged.astype(BF16), wo_ref[...], preferred_element_type=F32)
    x1_ref[...] = x1
    h2_ref[...] = _rms(x1, g2_ref[...]).astype(BF16)


def _mix(z, o, x2, ln_g, ln_b, w_s, bias_full, w_a, w_b, w_o, g2):
    n = x2.shape[0]
    row = lambda c: pl.BlockSpec((TM_MIX, D_MODEL), lambda i, c=c: (i, c))
    const = lambda shape: pl.BlockSpec(shape, lambda i: (0,) * len(shape))
    return pl.pallas_call(
        _mix_kernel,
        grid=(n // TM_MIX,),
        in_specs=[
            row(ZB_U), row(ZB_V), row(ZB_GA), row(ZB_GB), row(0), row(0),
            const((1, GMLP_WIDTH)), const((1, GMLP_WIDTH)),
            const((GMLP_GROUPS, GMLP_BLOCK, GMLP_BLOCK)),
            const((GMLP_BLOCK, GMLP_WIDTH)),
            const((GMLP_WIDTH, D_MODEL)), const((FOX_WIDTH, D_MODEL)),
            const((D_MODEL, D_MODEL)), const((1, D_MODEL)),
        ],
        out_specs=[row(0), row(0)],
        out_shape=[
            jax.ShapeDtypeStruct((n, D_MODEL), F32),
            jax.ShapeDtypeStruct((n, D_MODEL), BF16),
        ],
        scratch_shapes=[
            pltpu.VMEM((TM_MIX, GMLP_WIDTH), BF16),
            pltpu.VMEM((TM_MIX, GMLP_WIDTH), BF16),
        ],
        compiler_params=pltpu.CompilerParams(
            dimension_semantics=("arbitrary",),
            vmem_limit_bytes=VMEM_LIMIT),
        name="mix",
    )(z, z, z, z, o, x2, ln_g, ln_b, w_s, bias_full, w_a, w_b, w_o, g2)


def _ffn_kernel(tiles_per_batch, n_fc, h2_ref, halo_ref, x1_ref, p_ref, wug_ref, wul_ref,
                cwg_ref, cwl_ref, cbg_ref, cbl_ref, wd_ref, g3_ref, wpg_ref, wple_ref,
                gf_ref, out_ref, ug_scr, ul_scr, act_scr, acc_scr):
    i = pl.program_id(0)
    j = pl.program_id(1)

    halo = halo_ref[...]
    halo = jnp.where(i % tiles_per_batch == 0, jnp.zeros_like(halo), halo)
    hext = jnp.concatenate([halo, h2_ref[...]], axis=0)
    ug_scr[...] = jnp.dot(hext, wug_ref[...], preferred_element_type=F32)
    ul_scr[...] = jnp.dot(hext, wul_ref[...], preferred_element_type=F32)

    def conv(scr, cw_ref, cb_ref, cols):
        out = cb_ref[:, cols] + cw_ref[0:1, cols] * scr[pl.ds(HALO - 2, TM_FFN), cols]
        for t in range(1, CONV_WIDTH):
            out = out + cw_ref[t:t + 1, cols] * scr[pl.ds(HALO - 2 + t, TM_FFN), cols]
        return out

    for c in range(FC // LANES):
        cols = slice(c * LANES, (c + 1) * LANES)
        gate = conv(ug_scr, cwg_ref, cbg_ref, cols)
        lin = conv(ul_scr, cwl_ref, cbl_ref, cols)
        act_scr[:, cols] = (jax.nn.gelu(gate) * lin).astype(BF16)

    down = jnp.dot(act_scr[...], wd_ref[...], preferred_element_type=F32)

    @pl.when(j == 0)
    def _():
        acc_scr[...] = down

    @pl.when(j > 0)
    def _():
        acc_scr[...] += down

    @pl.when(j == n_fc - 1)
    def _():
        x2 = x1_ref[...] + acc_scr[...]
        h3 = _rms(x2, g3_ref[...]).astype(BF16)
        gate = jax.nn.sigmoid(jnp.dot(h3, wpg_ref[...], preferred_element_type=F32))
        ple = jnp.dot(p_ref[...].astype(BF16), wple_ref[...], preferred_element_type=F32)
        x3 = x2 + ple * gate
        out_ref[...] = _rms(x3, gf_ref[...])


def _ffn(h2, x1, p2, w_up, conv_w, conv_b, w_down, g3, w_pg, w_ple, gf, seq):
    n = x1.shape[0]
    n_fc = D_FF // FC
    halo_blocks = TM_FFN // HALO
    return pl.pallas_call(
        functools.partial(_ffn_kernel, seq // TM_FFN, n_fc),
        grid=(n // TM_FFN, n_fc),
        in_specs=[
            pl.BlockSpec((TM_FFN, D_MODEL), lambda i, j: (i, 0)),
            pl.BlockSpec((HALO, D_MODEL), lambda i, j: (jnp.maximum(i * halo_blocks - 1, 0), 0)),
            pl.BlockSpec((TM_FFN, D_MODEL), lambda i, j: (i, 0)),
            pl.BlockSpec((TM_FFN, PLE_DIM), lambda i, j: (i, 0)),
            pl.BlockSpec((D_MODEL, FC), lambda i, j: (0, j)),
            pl.BlockSpec((D_MODEL, FC), lambda i, j: (0, n_fc + j)),
            pl.BlockSpec((CONV_WIDTH, FC), lambda i, j: (0, j)),
            pl.BlockSpec((CONV_WIDTH, FC), lambda i, j: (0, n_fc + j)),
            pl.BlockSpec((1, FC), lambda i, j: (0, j)),
            pl.BlockSpec((1, FC), lambda i, j: (0, n_fc + j)),
            pl.BlockSpec((FC, D_MODEL), lambda i, j: (j, 0)),
            pl.BlockSpec((1, D_MODEL), lambda i, j: (0, 0)),
            pl.BlockSpec((D_MODEL, D_MODEL), lambda i, j: (0, 0)),
            pl.BlockSpec((PLE_DIM, D_MODEL), lambda i, j: (0, 0)),
            pl.BlockSpec((1, D_MODEL), lambda i, j: (0, 0)),
        ],
        out_specs=pl.BlockSpec((TM_FFN, D_MODEL), lambda i, j: (i, 0)),
        out_shape=jax.ShapeDtypeStruct((n, D_MODEL), F32),
        scratch_shapes=[
            pltpu.VMEM((HALO + TM_FFN, FC), F32),
            pltpu.VMEM((HALO + TM_FFN, FC), F32),
            pltpu.VMEM((TM_FFN, FC), BF16),
            pltpu.VMEM((TM_FFN, D_MODEL), F32),
        ],
        compiler_params=pltpu.CompilerParams(
            dimension_semantics=("arbitrary", "arbitrary"),
            vmem_limit_bytes=VMEM_LIMIT),
        name="ffn",
    )(h2, h2, x1, p2, w_up, w_up, conv_w, conv_w, conv_b, conv_b, w_down, g3, w_pg, w_ple, gf)


def _layer(x2, p2, batch, seq, norm_mix_g, w_in, b_f, gmlp_ln_g, gmlp_ln_b, gmlp_w_s, gmlp_b_s,
           w_branch_a, w_branch_b, w_out, norm_ffn_g, w_up, conv_w, conv_b, w_down,
           norm_ple_g, w_ple, w_ple_gate, out_g):
    o_f = 2 * GMLP_WIDTH + 3 * FOX_WIDTH
    w_all = w_in.astype(BF16)
    w_gates = w_all[:, o_f + FOX_HEADS:]
    w_f = jnp.pad(w_all[:, o_f:o_f + FOX_HEADS], ((0, 0), (0, LANES - FOX_HEADS)))
    b_fp = jnp.pad(b_f, (0, LANES - FOX_HEADS)).reshape(1, LANES)
    tri = jnp.asarray(np.tril(np.ones((CUM_BLK, CUM_BLK), np.float32)), BF16)

    z, cparts = _inproj(x2, norm_mix_g.reshape(1, -1), w_all, w_gates, w_f, b_fp, tri, seq)
    o, (w_a, w_b, w_o, w_u, w_d, w_pg, w_pl) = _attn(
        z, cparts, batch, seq,
        (w_branch_a, w_branch_b, w_out, w_up, w_down, w_ple_gate, w_ple))

    bias_full = jnp.repeat(gmlp_b_s.T, GMLP_BLOCK, axis=1)
    x1, h2 = _mix(z, o, x2, gmlp_ln_g.reshape(1, -1), gmlp_ln_b.reshape(1, -1),
                  gmlp_w_s.astype(BF16), bias_full, w_a, w_b, w_o, norm_ffn_g.reshape(1, -1))

    return _ffn(h2, x1, p2, w_u, conv_w, conv_b.reshape(1, -1), w_d,
                norm_ple_g.reshape(1, -1), w_pg, w_pl, out_g.reshape(1, -1), seq)


def kernel(x, p, norm_mix_g, w_in, b_f, gmlp_ln_g, gmlp_ln_b, gmlp_w_s, gmlp_b_s, w_branch_a,
           w_branch_b, w_out, norm_ffn_g, w_up, conv_w, conv_b, w_down, norm_ple_g, w_ple,
           w_ple_gate, norm_final_g):
    batch, seq, d = x.shape
    depth = p.shape[0]
    assert d == D_MODEL and depth == 1, "the fused final norm assumes a single layer"
    assert seq % TM_IN == 0 and seq % TM_FFN == 0 and seq % TQ == 0
    x2 = x.reshape(batch * seq, d)
    out = _layer(x2, p[0].reshape(batch * seq, PLE_DIM), batch, seq,
                 norm_mix_g[0], w_in[0], b_f[0], gmlp_ln_g[0], gmlp_ln_b[0], gmlp_w_s[0],
                 gmlp_b_s[0], w_branch_a[0], w_branch_b[0], w_out[0], norm_ffn_g[0], w_up[0],
                 conv_w[0], conv_b[0], w_down[0], norm_ple_g[0], w_ple[0], w_ple_gate[0],
                 norm_final_g)
    return out.reshape(batch, seq, d)
```

```python
import functools

import numpy as np
import jax
import jax.numpy as jnp
from jax import lax
from jax.experimental import pallas as pl
from jax.experimental.pallas import tpu as pltpu

F32 = jnp.float32
BF16 = jnp.bfloat16

D_MODEL = 1024
CHUNK = 64
PLE_DIM = 256
EPS = 1e-6
GMLP_GROUPS = 8
GMLP_BLOCK = 128
GMLP_WIDTH = 1024
FOX_HEADS = 16
FOX_HEAD_DIM = 64
FOX_WIDTH = 1024
D_FF = 2816
CONV_WIDTH = 3

LANES = 128
BF16_ROWS = 16
HEAD_PAIRS = FOX_HEADS // 2
N_PARTS = 3
LOG2E = 1.4426950408889634
Q_SCALE = FOX_HEAD_DIM ** -0.5 * LOG2E
PV_ROWS = FOX_HEAD_DIM + 16
BIAS_LANES = 4 * N_PARTS
Q_BIAS_OFFSET = 16

ZB_U, ZB_V, ZB_Q, ZB_K, ZB_VA, ZB_GA, ZB_GB = range(7)
Z_COLS = 7 * D_MODEL

TM_IN = 2048
RC_IN = 256
CUM_BLK = 256
TQ = 256
ATTN_PAIRS = 2
TM_MIX = 512
TM_FFN = 512
FFN_CHUNKS = (1536, 1280)
HALO = 16

VMEM_LIMIT = 56 * 1024 * 1024


def _rms(x, g):
    return x * lax.rsqrt(jnp.mean(x * x, axis=-1, keepdims=True) + EPS) * g


def _split3(x):
    hi = x.astype(BF16)
    r1 = x - hi.astype(F32)
    mid = r1.astype(BF16)
    lo = (r1 - mid.astype(F32)).astype(BF16)
    return hi, mid, lo


def _pack_parts(x):
    hi, mid, lo = (part.astype(F32) for part in _split3(x))
    lane = lax.broadcasted_iota(jnp.int32, x.shape, 1)
    packed = jnp.where(lane < FOX_HEADS, hi,
                       jnp.where(lane < 2 * FOX_HEADS, pltpu.roll(mid, FOX_HEADS, 1),
                                 pltpu.roll(lo, 2 * FOX_HEADS, 1)))
    return packed.astype(BF16)


def _gelu(x):
    a = -2.0 * np.sqrt(2.0 / np.pi) * LOG2E
    e = jnp.exp2(x * (a + (a * 0.044715) * (x * x)))
    return x * (1.0 / (1.0 + e))


def _log_sigmoid(x):
    return jnp.minimum(x, 0.0) - jnp.log1p(jnp.exp(-jnp.abs(x)))


def _inproj_kernel(tiles_per_batch, x_ref, g_ref, w_ref, wg_ref, wf_ref, bf_ref, tri_ref,
                   z_ref, cp_ref, h_scr, carry_scr):
    i = pl.program_id(0)
    j = pl.program_id(1)

    @pl.when(j == 0)
    def _():
        hb = _rms(x_ref[...], g_ref[...]).astype(BF16)
        h_scr[...] = hb
        zf = jnp.dot(hb, wf_ref[...], preferred_element_type=F32)
        logf = _log_sigmoid(zf + bf_ref[...])

        @pl.when(i % tiles_per_batch == 0)
        def _():
            carry_scr[...] = jnp.zeros_like(carry_scr)

        carry = carry_scr[0:1, :]
        for r in range(TM_IN // CUM_BLK):
            rows = slice(r * CUM_BLK, (r + 1) * CUM_BLK)
            parts = jnp.concatenate(_split3(logf[rows]), axis=1)
            cs = jnp.dot(tri_ref[...], parts, preferred_element_type=F32)
            cum = (cs[:, :LANES] + cs[:, LANES:2 * LANES]) + cs[:, 2 * LANES:] + carry
            carry = cum[CUM_BLK - 1:CUM_BLK, :]
            cp_ref[rows, :] = _pack_parts(cum * LOG2E)
        carry_scr[...] = jnp.broadcast_to(carry, carry_scr.shape)

    def project(fn, weights=w_ref):
        for r in range(TM_IN // RC_IN):
            rows = slice(r * RC_IN, (r + 1) * RC_IN)
            acc = jnp.dot(h_scr[rows, :], weights[...], preferred_element_type=F32)
            z_ref[rows, :] = fn(acc).astype(BF16)

    @pl.when(j <= ZB_V)
    def _():
        project(_gelu)

    @pl.when(j == ZB_Q)
    def _():
        project(lambda a: a * Q_SCALE)

    @pl.when(jnp.logical_or(j == ZB_K, j == ZB_VA))
    def _():
        project(lambda a: a)

    @pl.when(j >= ZB_GA)
    def _():
        project(jax.nn.sigmoid, wg_ref)


def _inproj(x2, g, w_all, w_gates, w_f, b_f, tri, seq):
    n = x2.shape[0]
    grid = (n // TM_IN, Z_COLS // D_MODEL)
    return pl.pallas_call(
        functools.partial(_inproj_kernel, seq // TM_IN),
        grid=grid,
        in_specs=[
            pl.BlockSpec((TM_IN, D_MODEL), lambda i, j: (i, 0)),
            pl.BlockSpec((1, D_MODEL), lambda i, j: (0, 0)),
            pl.BlockSpec((D_MODEL, D_MODEL), lambda i, j: (0, jnp.minimum(j, ZB_VA))),
            pl.BlockSpec((D_MODEL, D_MODEL), lambda i, j: (0, jnp.maximum(j - ZB_GA, 0))),
            pl.BlockSpec((D_MODEL, LANES), lambda i, j: (0, 0)),
            pl.BlockSpec((1, LANES), lambda i, j: (0, 0)),
            pl.BlockSpec((CUM_BLK, CUM_BLK), lambda i, j: (0, 0)),
        ],
        out_specs=[
            pl.BlockSpec((TM_IN, D_MODEL), lambda i, j: (i, j)),
            pl.BlockSpec((TM_IN, LANES), lambda i, j: (i, 0)),
        ],
        out_shape=[
            jax.ShapeDtypeStruct((n, Z_COLS), BF16),
            jax.ShapeDtypeStruct((n, LANES), BF16),
        ],
        scratch_shapes=[
            pltpu.VMEM((TM_IN, D_MODEL), BF16),
            pltpu.VMEM((8, LANES), F32),
        ],
        compiler_params=pltpu.CompilerParams(
            dimension_semantics=("arbitrary", "arbitrary"),
            vmem_limit_bytes=VMEM_LIMIT),
        name="inproj",
    )(x2, g, w_all, w_gates, w_f, b_f, tri)


def _attn_kernel(n_qt, n_cast, q_ref, k_ref, v_ref, cp_ref, place_ref, crow_ref, eye_ref, *rest):
    cast_in, (o_ref, *cast_out) = rest[:n_cast], rest[n_cast:2 * n_cast + 1]
    kaug_scr, vt_scr, qr_scr, s_scr = rest[2 * n_cast + 1:]
    for src, dst in zip(cast_in, cast_out):
        dst[...] = src[...].astype(BF16)

    nt = (((1,), (1,)), ((), ()))
    chains = range(ATTN_PAIRS)
    lane = lax.broadcasted_iota(jnp.int32, (TQ, LANES), 1)

    for c in chains:
        lanes = slice(c * LANES, (c + 1) * LANES)
        aug = jnp.dot(cp_ref[...], place_ref[c], preferred_element_type=F32) + crow_ref[0:1, :]
        kaug_scr[c, :, :LANES] = k_ref[:, lanes]
        kaug_scr[c, :, LANES:] = aug.astype(BF16)
        bias_lane = lax.broadcasted_iota(jnp.int32, aug.shape, 1) < BIAS_LANES
        augq = [jnp.where(bias_lane, pltpu.roll(aug, LANES - Q_BIAS_OFFSET * (1 + hh), 1), 0.0)
                .astype(BF16) for hh in range(2)]
        ones = jnp.ones((PV_ROWS - FOX_HEAD_DIM, TQ), BF16)
        for t in range(n_qt):
            rows = slice(t * TQ, (t + 1) * TQ)
            vt = lax.dot_general(eye_ref[...], v_ref[rows, lanes], nt,
                                 preferred_element_type=F32).astype(BF16)
            for hh in range(2):
                vt_scr[c, t, hh, :FOX_HEAD_DIM, :] = vt[hh * FOX_HEAD_DIM:(hh + 1) * FOX_HEAD_DIM]
                vt_scr[c, t, hh, FOX_HEAD_DIM:, :] = ones
            q = q_ref[rows, lanes]
            zero = jnp.zeros_like(q)
            for hh in range(2):
                own = (lane >= FOX_HEAD_DIM) if hh else (lane < FOX_HEAD_DIM)
                qrows = slice(hh * TQ, (hh + 1) * TQ)
                qr_scr[c, t, qrows, :LANES] = jnp.where(own, q, zero)
                qr_scr[c, t, qrows, LANES:] = augq[hh][rows]

    kpos = lax.broadcasted_iota(jnp.int32, (TQ, 2 * TQ), 0)
    col = lax.broadcasted_iota(jnp.int32, (TQ, 2 * TQ), 1)
    causal = kpos <= jnp.where(col >= TQ, col - TQ, col)
    def pass1(qi):
        nk = (qi + 1) * TQ
        smax = []
        for c in chains:
            s = lax.dot_general(kaug_scr[c, :nk, :], qr_scr[c, qi], nt,
                                preferred_element_type=F32)
            diag = jnp.where(causal, s[nk - TQ:], -1e30)
            m = jnp.max(diag, axis=0, keepdims=True)
            if qi:
                s_scr[qi % 2, c, :nk - TQ, :] = s[:nk - TQ]
                m = jnp.maximum(m, jnp.max(s[:nk - TQ], axis=0, keepdims=True))
            s_scr[qi % 2, c, nk - TQ:nk, :] = diag
            smax.append(m)
        return smax

    def pass2(qi, smax):
        for c in chains:
            acc = [None, None]
            for t in range(qi + 1):
                p = jnp.exp2(s_scr[qi % 2, c, t * TQ:(t + 1) * TQ, :] - smax[c]).astype(BF16)
                for hh in range(2):
                    d = jnp.dot(vt_scr[c, t, hh], p[:, hh * TQ:(hh + 1) * TQ],
                                preferred_element_type=F32)
                    acc[hh] = d if acc[hh] is None else acc[hh] + d
            heads = [a[:FOX_HEAD_DIM] * (1.0 / a[FOX_HEAD_DIM:FOX_HEAD_DIM + 1]) for a in acc]
            ot = jnp.concatenate(heads, axis=0)
            o_ref[qi * TQ:(qi + 1) * TQ, c * LANES:(c + 1) * LANES] = ot.T.astype(BF16)

    smax = pass1(0)
    for qi in range(n_qt):
        nxt = pass1(qi + 1) if qi + 1 < n_qt else None
        pass2(qi, smax)
        smax = nxt


def _attn_consts():
    place = np.zeros((HEAD_PAIRS, LANES, LANES), np.float32)
    crow = np.zeros((8, LANES), np.float32)
    q0, q1 = Q_BIAS_OFFSET, 2 * Q_BIAS_OFFSET
    for p in range(HEAD_PAIRS):
        for a in range(N_PARTS):
            place[p, a * FOX_HEADS + 2 * p, 0 + a] = -1.0
            place[p, a * FOX_HEADS + 2 * p + 1, 6 + a] = -1.0
            place[p, a * FOX_HEADS + 2 * p, q0 + 3 + a] = 1.0
            place[p, a * FOX_HEADS + 2 * p + 1, q1 + 9 + a] = 1.0
    crow[0, 3:6] = 1.0
    crow[0, 9:12] = 1.0
    crow[0, q0 + 0:q0 + 3] = 1.0
    crow[0, q1 + 6:q1 + 9] = 1.0
    eye = np.eye(LANES, dtype=np.float32)
    return jnp.asarray(place, BF16), jnp.asarray(crow, F32), jnp.asarray(eye, BF16)


def _attn(z, cparts, batch, seq, weights):
    n = z.shape[0]
    n_qt = seq // TQ
    place, crow, eye = _attn_consts()
    n_groups = HEAD_PAIRS // ATTN_PAIRS
    cast_in_specs, cast_out_specs, cast_out_shapes = [], [], []
    for w in weights:
        rows, cols = w.shape
        n_slabs = max(s for s in range(1, batch * n_groups + 1)
                      if rows % s == 0 and (rows // s) % BF16_ROWS == 0)
        slab = rows // n_slabs
        index = lambda b, g, last=n_slabs - 1: (jnp.minimum(b * n_groups + g, last), 0)
        cast_in_specs.append(pl.BlockSpec((slab, cols), index))
        cast_out_specs.append(pl.BlockSpec((slab, cols), index))
        cast_out_shapes.append(jax.ShapeDtypeStruct(w.shape, BF16))
    width = ATTN_PAIRS * LANES
    qcol = ZB_Q * D_MODEL // width
    kcol = ZB_K * D_MODEL // width
    vcol = ZB_VA * D_MODEL // width
    outs = pl.pallas_call(
        functools.partial(_attn_kernel, n_qt, len(weights)),
        grid=(batch, n_groups),
        in_specs=[
            pl.BlockSpec((seq, width), lambda b, g: (b, qcol + g)),
            pl.BlockSpec((seq, width), lambda b, g: (b, kcol + g)),
            pl.BlockSpec((seq, width), lambda b, g: (b, vcol + g)),
            pl.BlockSpec((seq, LANES), lambda b, g: (b, 0)),
            pl.BlockSpec((ATTN_PAIRS, LANES, LANES), lambda b, g: (g, 0, 0)),
            pl.BlockSpec((8, LANES), lambda b, g: (0, 0)),
            pl.BlockSpec((LANES, LANES), lambda b, g: (0, 0)),
        ] + cast_in_specs,
        out_specs=[pl.BlockSpec((seq, width), lambda b, g: (b, g))] + cast_out_specs,
        out_shape=[jax.ShapeDtypeStruct((n, FOX_WIDTH), BF16)] + cast_out_shapes,
        scratch_shapes=[
            pltpu.VMEM((ATTN_PAIRS, seq, 2 * LANES), BF16),
            pltpu.VMEM((ATTN_PAIRS, n_qt, 2, PV_ROWS, TQ), BF16),
            pltpu.VMEM((ATTN_PAIRS, n_qt, 2 * TQ, 2 * LANES), BF16),
            pltpu.VMEM((2, ATTN_PAIRS, seq, 2 * TQ), F32),
        ],
        compiler_params=pltpu.CompilerParams(
            dimension_semantics=("arbitrary", "arbitrary"),
            vmem_limit_bytes=VMEM_LIMIT),
        name="attn",
    )(z, z, z, cparts, place, crow, eye, *weights)
    return outs[0], outs[1:]


def _mix_kernel(u_ref, v_ref, ga_ref, gb_ref, o_ref, x_ref, lng_ref, lnb_ref, ws_ref,
                bias_ref, wa_ref, wb_ref, wo_ref, g2_ref, x1_ref, h2_ref, vn_scr, a_scr):
    v = v_ref[...].astype(F32)
    mu = jnp.mean(v, axis=-1, keepdims=True)
    vc = v - mu
    var = jnp.mean(vc * vc, axis=-1, keepdims=True)
    vn_scr[...] = (vc * lax.rsqrt(var + EPS) * lng_ref[...] + lnb_ref[...]).astype(BF16)

    t_idx = lax.broadcasted_iota(jnp.int32, (GMLP_BLOCK, GMLP_BLOCK), 0)
    s_idx = lax.broadcasted_iota(jnp.int32, (GMLP_BLOCK, GMLP_BLOCK), 1)
    causal = (s_idx // CHUNK) <= (t_idx // CHUNK)
    for g in range(GMLP_GROUPS):
        cols = slice(g * GMLP_BLOCK, (g + 1) * GMLP_BLOCK)
        wg = jnp.where(causal, ws_ref[g], jnp.zeros_like(ws_ref[g]))
        for r in range(TM_MIX // GMLP_BLOCK):
            rows = slice(r * GMLP_BLOCK, (r + 1) * GMLP_BLOCK)
            mixed = jnp.dot(wg, vn_scr[rows, cols], preferred_element_type=F32) + bias_ref[:, cols]
            a_scr[rows, cols] = (u_ref[rows, cols].astype(F32) * mixed).astype(BF16)

    ya = jnp.dot(a_scr[...], wa_ref[...], preferred_element_type=F32)
    yb = jnp.dot(o_ref[...], wb_ref[...], preferred_element_type=F32)
    merged = ga_ref[...].astype(F32) * ya + gb_ref[...].astype(F32) * yb
    x1 = x_ref[...] + jnp.dot(merged.astype(BF16), wo_ref[...], preferred_element_type=F32)
    x1_ref[...] = x1
    h2_ref[...] = _rms(x1, g2_ref[...]).astype(BF16)


def _mix(z, o, x2, ln_g, ln_b, w_s, bias_full, w_a, w_b, w_o, g2):
    n = x2.shape[0]
    row = lambda c: pl.BlockSpec((TM_MIX, D_MODEL), lambda i, c=c: (i, c))
    const = lambda shape: pl.BlockSpec(shape, lambda i: (0,) * len(shape))
    return pl.pallas_call(
        _mix_kernel,
        grid=(n // TM_MIX,),
        in_specs=[
            row(ZB_U), row(ZB_V), row(ZB_GA), row(ZB_GB), row(0), row(0),
            const((1, GMLP_WIDTH)), const((1, GMLP_WIDTH)),
            const((GMLP_GROUPS, GMLP_BLOCK, GMLP_BLOCK)),
            const((GMLP_BLOCK, GMLP_WIDTH)),
            const((GMLP_WIDTH, D_MODEL)), const((FOX_WIDTH, D_MODEL)),
            const((D_MODEL, D_MODEL)), const((1, D_MODEL)),
        ],
        out_specs=[row(0), row(0)],
        out_shape=[
            jax.ShapeDtypeStruct((n, D_MODEL), F32),
            jax.ShapeDtypeStruct((n, D_MODEL), BF16),
        ],
        scratch_shapes=[
            pltpu.VMEM((TM_MIX, GMLP_WIDTH), BF16),
            pltpu.VMEM((TM_MIX, GMLP_WIDTH), BF16),
        ],
        compiler_params=pltpu.CompilerParams(
            dimension_semantics=("arbitrary",),
            vmem_limit_bytes=VMEM_LIMIT),
        name="mix",
    )(z, z, z, z, o, x2, ln_g, ln_b, w_s, bias_full, w_a, w_b, w_o, g2)


def _ffn_kernel(tiles_per_batch, h2_ref, halo_ref, x1_ref, p_ref, wup_ref, cw_ref, cb_ref,
                wd_ref, g3_ref, wpg_ref, wple_ref, gf_ref, out_ref,
                hext_scr, ug_scr, ul_scr, act_scr, acc_scr):
    i = pl.program_id(0)

    @pl.when(pl.program_id(1) == 0)
    def _():
        halo = halo_ref[...]
        hext_scr[:HALO, :] = jnp.where(i % tiles_per_batch == 0, jnp.zeros_like(halo), halo)
        hext_scr[HALO:, :] = h2_ref[...]

    def conv(scr, first_col, lanes):
        cols = slice(first_col + lanes.start, first_col + lanes.stop)
        out = cb_ref[:, cols] + cw_ref[0:1, cols] * scr[pl.ds(HALO - 2, TM_FFN), lanes]
        for t in range(1, CONV_WIDTH):
            out = out + cw_ref[t:t + 1, cols] * scr[pl.ds(HALO - 2 + t, TM_FFN), lanes]
        return out

    def chunk(start, width):
        ug_scr[:, :width] = jnp.dot(hext_scr[...], wup_ref[:, start:start + width],
                                    preferred_element_type=F32)
        ul_scr[:, :width] = jnp.dot(hext_scr[...], wup_ref[:, D_FF + start:D_FF + start + width],
                                    preferred_element_type=F32)
        for c in range(width // LANES):
            lanes = slice(c * LANES, (c + 1) * LANES)
            gate = conv(ug_scr, start, lanes)
            lin = conv(ul_scr, D_FF + start, lanes)
            act_scr[:, lanes] = (jax.nn.gelu(gate) * lin).astype(BF16)
        down = jnp.dot(act_scr[:, :width], wd_ref[start:start + width, :],
                       preferred_element_type=F32)
        if start == 0:
            acc_scr[...] = down
        else:
            acc_scr[...] += down

    j = pl.program_id(1)
    start = 0
    for step, width in enumerate(FFN_CHUNKS):
        pl.when(j == step)(functools.partial(chunk, start, width))
        start += width

    @pl.when(j == len(FFN_CHUNKS) - 1)
    def _():
        x2 = x1_ref[...] + acc_scr[...]
        h3 = _rms(x2, g3_ref[...]).astype(BF16)
        gate = jax.nn.sigmoid(jnp.dot(h3, wpg_ref[...], preferred_element_type=F32))
        ple = jnp.dot(p_ref[...].astype(BF16), wple_ref[...], preferred_element_type=F32)
        x3 = x2 + ple * gate
        out_ref[...] = _rms(x3, gf_ref[...])


def _ffn(h2, x1, p2, w_up, conv_w, conv_b, w_down, g3, w_pg, w_ple, gf, seq):
    n = x1.shape[0]
    halo_blocks = TM_FFN // HALO
    widest = max(FFN_CHUNKS)
    row = lambda width: pl.BlockSpec((TM_FFN, width), lambda i, j: (i, 0))
    const = lambda shape: pl.BlockSpec(shape, lambda i, j: (0,) * len(shape),
                                       pipeline_mode=pl.Buffered(1))
    return pl.pallas_call(
        functools.partial(_ffn_kernel, seq // TM_FFN),
        grid=(n // TM_FFN, len(FFN_CHUNKS)),
        in_specs=[
            row(D_MODEL),
            pl.BlockSpec((HALO, D_MODEL),
                         lambda i, j: (jnp.maximum(i * halo_blocks - 1, 0), 0)),
            row(D_MODEL), row(PLE_DIM),
            const((D_MODEL, 2 * D_FF)), const((CONV_WIDTH, 2 * D_FF)), const((1, 2 * D_FF)),
            const((D_FF, D_MODEL)), const((1, D_MODEL)), const((D_MODEL, D_MODEL)),
            const((PLE_DIM, D_MODEL)), const((1, D_MODEL)),
        ],
        out_specs=row(D_MODEL),
        out_shape=jax.ShapeDtypeStruct((n, D_MODEL), F32),
        scratch_shapes=[
            pltpu.VMEM((HALO + TM_FFN, D_MODEL), BF16),
            pltpu.VMEM((HALO + TM_FFN, widest), F32),
            pltpu.VMEM((HALO + TM_FFN, widest), F32),
            pltpu.VMEM((TM_FFN, widest), BF16),
            pltpu.VMEM((TM_FFN, D_MODEL), F32),
        ],
        compiler_params=pltpu.CompilerParams(
            dimension_semantics=("arbitrary", "arbitrary"),
            vmem_limit_bytes=VMEM_LIMIT),
        name="ffn",
    )(h2, h2, x1, p2, w_up, conv_w, conv_b, w_down, g3, w_pg, w_ple, gf)


def _layer(x2, p2, batch, seq, norm_mix_g, w_in, b_f, gmlp_ln_g, gmlp_ln_b, gmlp_w_s, gmlp_b_s,
           w_branch_a, w_branch_b, w_out, norm_ffn_g, w_up, conv_w, conv_b, w_down,
           norm_ple_g, w_ple, w_ple_gate, out_g):
    o_f = 2 * GMLP_WIDTH + 3 * FOX_WIDTH
    w_all = w_in.astype(BF16)
    w_gates = w_all[:, o_f + FOX_HEADS:]
    w_f = jnp.pad(w_all[:, o_f:o_f + FOX_HEADS], ((0, 0), (0, LANES - FOX_HEADS)))
    b_fp = jnp.pad(b_f, (0, LANES - FOX_HEADS)).reshape(1, LANES)
    tri = jnp.asarray(np.tril(np.ones((CUM_BLK, CUM_BLK), np.float32)), BF16)

    z, cparts = _inproj(x2, norm_mix_g.reshape(1, -1), w_all, w_gates, w_f, b_fp, tri, seq)
    o, (w_a, w_b, w_o, w_u, w_d, w_pg, w_pl) = _attn(
        z, cparts, batch, seq,
        (w_branch_a, w_branch_b, w_out, w_up, w_down, w_ple_gate, w_ple))

    bias_full = jnp.repeat(gmlp_b_s.T, GMLP_BLOCK, axis=1)
    x1, h2 = _mix(z, o, x2, gmlp_ln_g.reshape(1, -1), gmlp_ln_b.reshape(1, -1),
                  gmlp_w_s.astype(BF16), bias_full, w_a, w_b, w_o, norm_ffn_g.reshape(1, -1))

    return _ffn(h2, x1, p2, w_u, conv_w, conv_b.reshape(1, -1), w_d,
                norm_ple_g.reshape(1, -1), w_pg, w_pl, out_g.reshape(1, -1), seq)


def kernel(x, p, norm_mix_g, w_in, b_f, gmlp_ln_g, gmlp_ln_b, gmlp_w_s, gmlp_b_s, w_branch_a,
           w_branch_b, w_out, norm_ffn_g, w_up, conv_w, conv_b, w_down, norm_ple_g, w_ple,
           w_ple_gate, norm_final_g):
    batch, seq, d = x.shape
    depth = p.shape[0]
    assert d == D_MODEL and depth == 1, "the fused final norm assumes a single layer"
    assert seq % TM_IN == 0 and seq % TM_FFN == 0 and seq % TQ == 0
    x2 = x.reshape(batch * seq, d)
    out = _layer(x2, p[0].reshape(batch * seq, PLE_DIM), batch, seq,
                 norm_mix_g[0], w_in[0], b_f[0], gmlp_ln_g[0], gmlp_ln_b[0], gmlp_w_s[0],
                 gmlp_b_s[0], w_branch_a[0], w_branch_b[0], w_out[0], norm_ffn_g[0], w_up[0],
                 conv_w[0], conv_b[0], w_down[0], norm_ple_g[0], w_ple[0], w_ple_gate[0],
                 norm_final_g)
    return out.reshape(batch, seq, d)
```

```python
import functools

import numpy as np
import jax
import jax.numpy as jnp
from jax import lax
from jax.experimental import pallas as pl
from jax.experimental.pallas import tpu as pltpu

F32 = jnp.float32
BF16 = jnp.bfloat16

D_MODEL = 1024
CHUNK = 64
PLE_DIM = 256
EPS = 1e-6
GMLP_GROUPS = 8
GMLP_BLOCK = 128
GMLP_WIDTH = 1024
FOX_HEADS = 16
FOX_HEAD_DIM = 64
FOX_WIDTH = 1024
D_FF = 2816
CONV_WIDTH = 3

LANES = 128
BF16_ROWS = 16
HEAD_PAIRS = FOX_HEADS // 2
N_PARTS = 3
LOG2E = 1.4426950408889634
Q_SCALE = FOX_HEAD_DIM ** -0.5 * LOG2E
PV_ROWS = FOX_HEAD_DIM + 16
BIAS_LANES = 4 * N_PARTS
Q_BIAS_OFFSET = 16

ZB_U, ZB_V, ZB_Q, ZB_K, ZB_VA, ZB_GA, ZB_GB = range(7)
Z_COLS = 7 * D_MODEL

TM_IN = 2048
RC_IN = 256
CUM_BLK = 256
TQ = 256
ATTN_PAIRS = 2
TM_MIX = 512
TM_FFN = 512
FFN_CHUNKS = (1536, 1280)
HALO = 16

VMEM_LIMIT = 56 * 1024 * 1024


def _rms(x, g):
    return x * lax.rsqrt(jnp.mean(x * x, axis=-1, keepdims=True) + EPS) * g


def _split3(x):
    hi = x.astype(BF16)
    r1 = x - hi.astype(F32)
    mid = r1.astype(BF16)
    lo = (r1 - mid.astype(F32)).astype(BF16)
    return hi, mid, lo


def _pack_parts(x):
    hi, mid, lo = (part.astype(F32) for part in _split3(x))
    lane = lax.broadcasted_iota(jnp.int32, x.shape, 1)
    packed = jnp.where(lane < FOX_HEADS, hi,
                       jnp.where(lane < 2 * FOX_HEADS, pltpu.roll(mid, FOX_HEADS, 1),
                                 pltpu.roll(lo, 2 * FOX_HEADS, 1)))
    return packed.astype(BF16)


def _gelu(x):
    a = -2.0 * np.sqrt(2.0 / np.pi) * LOG2E
    e = jnp.exp2(x * (a + (a * 0.044715) * (x * x)))
    return x * (1.0 / (1.0 + e))


def _log_sigmoid(x):
    return jnp.minimum(x, 0.0) - jnp.log1p(jnp.exp(-jnp.abs(x)))


def _inproj_kernel(tiles_per_batch, x_ref, g_ref, w_ref, wg_ref, wf_ref, bf_ref, tri_ref,
                   z_ref, cp_ref, h_scr, carry_scr):
    i = pl.program_id(0)
    j = pl.program_id(1)

    @pl.when(j == 0)
    def _():
        hb = _rms(x_ref[...], g_ref[...]).astype(BF16)
        h_scr[...] = hb
        zf = jnp.dot(hb, wf_ref[...], preferred_element_type=F32)
        logf = _log_sigmoid(zf + bf_ref[...])

        @pl.when(i % tiles_per_batch == 0)
        def _():
            carry_scr[...] = jnp.zeros_like(carry_scr)

        carry = carry_scr[0:1, :]
        for r in range(TM_IN // CUM_BLK):
            rows = slice(r * CUM_BLK, (r + 1) * CUM_BLK)
            parts = jnp.concatenate(_split3(logf[rows]), axis=1)
            cs = jnp.dot(tri_ref[...], parts, preferred_element_type=F32)
            cum = (cs[:, :LANES] + cs[:, LANES:2 * LANES]) + cs[:, 2 * LANES:] + carry
            carry = cum[CUM_BLK - 1:CUM_BLK, :]
            cp_ref[rows, :] = _pack_parts(cum * LOG2E)
        carry_scr[...] = jnp.broadcast_to(carry, carry_scr.shape)

    def project(fn, weights=w_ref, first_block=0):
        cols = pl.ds(pl.multiple_of((j - first_block) * D_MODEL, D_MODEL), D_MODEL)
        for r in range(TM_IN // RC_IN):
            rows = slice(r * RC_IN, (r + 1) * RC_IN)
            acc = jnp.dot(h_scr[rows, :], weights[:, cols], preferred_element_type=F32)
            z_ref[rows, :] = fn(acc).astype(BF16)

    @pl.when(j <= ZB_V)
    def _():
        project(_gelu)

    @pl.when(j == ZB_Q)
    def _():
        project(lambda a: a * Q_SCALE)

    @pl.when(jnp.logical_or(j == ZB_K, j == ZB_VA))
    def _():
        project(lambda a: a)

    @pl.when(j >= ZB_GA)
    def _():
        project(jax.nn.sigmoid, wg_ref, ZB_GA)


def _inproj(x2, g, w_all, w_gates, w_f, b_f, tri, seq):
    n = x2.shape[0]
    grid = (n // TM_IN, Z_COLS // D_MODEL)
    return pl.pallas_call(
        functools.partial(_inproj_kernel, seq // TM_IN),
        grid=grid,
        in_specs=[
            pl.BlockSpec((TM_IN, D_MODEL), lambda i, j: (i, 0)),
            pl.BlockSpec((1, D_MODEL), lambda i, j: (0, 0)),
            pl.BlockSpec(w_all.shape, lambda i, j: (0, 0), pipeline_mode=pl.Buffered(1)),
            pl.BlockSpec(w_gates.shape, lambda i, j: (0, 0), pipeline_mode=pl.Buffered(1)),
            pl.BlockSpec((D_MODEL, LANES), lambda i, j: (0, 0)),
            pl.BlockSpec((1, LANES), lambda i, j: (0, 0)),
            pl.BlockSpec((CUM_BLK, CUM_BLK), lambda i, j: (0, 0)),
        ],
        out_specs=[
            pl.BlockSpec((TM_IN, D_MODEL), lambda i, j: (i, j)),
            pl.BlockSpec((TM_IN, LANES), lambda i, j: (i, 0)),
        ],
        out_shape=[
            jax.ShapeDtypeStruct((n, Z_COLS), BF16),
            jax.ShapeDtypeStruct((n, LANES), BF16),
        ],
        scratch_shapes=[
            pltpu.VMEM((TM_IN, D_MODEL), BF16),
            pltpu.VMEM((8, LANES), F32),
        ],
        compiler_params=pltpu.CompilerParams(
            dimension_semantics=("arbitrary", "arbitrary"),
            vmem_limit_bytes=VMEM_LIMIT),
        name="inproj",
    )(x2, g, w_all, w_gates, w_f, b_f, tri)


def _attn_kernel(n_qt, n_cast, q_ref, k_ref, v_ref, cp_ref, place_ref, crow_ref, eye_ref, *rest):
    cast_in, (o_ref, *cast_out) = rest[:n_cast], rest[n_cast:2 * n_cast + 1]
    kaug_scr, vt_scr, qr_scr, s_scr = rest[2 * n_cast + 1:]
    for src, dst in zip(cast_in, cast_out):
        dst[...] = src[...].astype(BF16)

    nt = (((1,), (1,)), ((), ()))
    chains = range(ATTN_PAIRS)
    lane = lax.broadcasted_iota(jnp.int32, (TQ, LANES), 1)

    for c in chains:
        lanes = slice(c * LANES, (c + 1) * LANES)
        aug = jnp.dot(cp_ref[...], place_ref[c], preferred_element_type=F32) + crow_ref[0:1, :]
        kaug_scr[c, :, :LANES] = k_ref[:, lanes]
        kaug_scr[c, :, LANES:] = aug.astype(BF16)
        bias_lane = lax.broadcasted_iota(jnp.int32, aug.shape, 1) < BIAS_LANES
        augq = [jnp.where(bias_lane, pltpu.roll(aug, LANES - Q_BIAS_OFFSET * (1 + hh), 1), 0.0)
                .astype(BF16) for hh in range(2)]
        ones = jnp.ones((PV_ROWS - FOX_HEAD_DIM, TQ), BF16)
        for t in range(n_qt):
            rows = slice(t * TQ, (t + 1) * TQ)
            vt = lax.dot_general(eye_ref[...], v_ref[rows, lanes], nt,
                                 preferred_element_type=F32).astype(BF16)
            for hh in range(2):
                vt_scr[c, t, hh, :FOX_HEAD_DIM, :] = vt[hh * FOX_HEAD_DIM:(hh + 1) * FOX_HEAD_DIM]
                vt_scr[c, t, hh, FOX_HEAD_DIM:, :] = ones
            q = q_ref[rows, lanes]
            zero = jnp.zeros_like(q)
            for hh in range(2):
                own = (lane >= FOX_HEAD_DIM) if hh else (lane < FOX_HEAD_DIM)
                qrows = slice(hh * TQ, (hh + 1) * TQ)
                qr_scr[c, t, qrows, :LANES] = jnp.where(own, q, zero)
                qr_scr[c, t, qrows, LANES:] = augq[hh][rows]

    kpos = lax.broadcasted_iota(jnp.int32, (TQ, 2 * TQ), 0)
    col = lax.broadcasted_iota(jnp.int32, (TQ, 2 * TQ), 1)
    causal = kpos <= jnp.where(col >= TQ, col - TQ, col)
    def pass1(qi):
        nk = (qi + 1) * TQ
        smax = []
        for c in chains:
            s = lax.dot_general(kaug_scr[c, :nk, :], qr_scr[c, qi], nt,
                                preferred_element_type=F32)
            diag = jnp.where(causal, s[nk - TQ:], -1e30)
            m = jnp.max(diag, axis=0, keepdims=True)
            if qi:
                s_scr[qi % 2, c, :nk - TQ, :] = s[:nk - TQ]
                m = jnp.maximum(m, jnp.max(s[:nk - TQ], axis=0, keepdims=True))
            s_scr[qi % 2, c, nk - TQ:nk, :] = diag
            smax.append(m)
        return smax

    def pass2(qi, smax):
        for c in chains:
            acc = [None, None]
            for t in range(qi + 1):
                p = jnp.exp2(s_scr[qi % 2, c, t * TQ:(t + 1) * TQ, :] - smax[c]).astype(BF16)
                for hh in range(2):
                    d = jnp.dot(vt_scr[c, t, hh], p[:, hh * TQ:(hh + 1) * TQ],
                                preferred_element_type=F32)
                    acc[hh] = d if acc[hh] is None else acc[hh] + d
            heads = [a[:FOX_HEAD_DIM] * (1.0 / a[FOX_HEAD_DIM:FOX_HEAD_DIM + 1]) for a in acc]
            ot = jnp.concatenate(heads, axis=0)
            o_ref[qi * TQ:(qi + 1) * TQ, c * LANES:(c + 1) * LANES] = ot.T.astype(BF16)

    smax = pass1(0)
    for qi in range(n_qt):
        nxt = pass1(qi + 1) if qi + 1 < n_qt else None
        pass2(qi, smax)
        smax = nxt


def _attn_consts():
    place = np.zeros((HEAD_PAIRS, LANES, LANES), np.float32)
    crow = np.zeros((8, LANES), np.float32)
    q0, q1 = Q_BIAS_OFFSET, 2 * Q_BIAS_OFFSET
    for p in range(HEAD_PAIRS):
        for a in range(N_PARTS):
            place[p, a * FOX_HEADS + 2 * p, 0 + a] = -1.0
            place[p, a * FOX_HEADS + 2 * p + 1, 6 + a] = -1.0
            place[p, a * FOX_HEADS + 2 * p, q0 + 3 + a] = 1.0
            place[p, a * FOX_HEADS + 2 * p + 1, q1 + 9 + a] = 1.0
    crow[0, 3:6] = 1.0
    crow[0, 9:12] = 1.0
    crow[0, q0 + 0:q0 + 3] = 1.0
    crow[0, q1 + 6:q1 + 9] = 1.0
    eye = np.eye(LANES, dtype=np.float32)
    return jnp.asarray(place, BF16), jnp.asarray(crow, F32), jnp.asarray(eye, BF16)


def _attn(z, cparts, batch, seq, weights):
    n = z.shape[0]
    n_qt = seq // TQ
    place, crow, eye = _attn_consts()
    n_groups = HEAD_PAIRS // ATTN_PAIRS
    cast_in_specs, cast_out_specs, cast_out_shapes = [], [], []
    for w in weights:
        rows, cols = w.shape
        n_slabs = max(s for s in range(1, batch * n_groups + 1)
                      if rows % s == 0 and (rows // s) % BF16_ROWS == 0)
        slab = rows // n_slabs
        index = lambda b, g, last=n_slabs - 1: (jnp.minimum(b * n_groups + g, last), 0)
        cast_in_specs.append(pl.BlockSpec((slab, cols), index))
        cast_out_specs.append(pl.BlockSpec((slab, cols), index))
        cast_out_shapes.append(jax.ShapeDtypeStruct(w.shape, BF16))
    width = ATTN_PAIRS * LANES
    qcol = ZB_Q * D_MODEL // width
    kcol = ZB_K * D_MODEL // width
    vcol = ZB_VA * D_MODEL // width
    outs = pl.pallas_call(
        functools.partial(_attn_kernel, n_qt, len(weights)),
        grid=(batch, n_groups),
        in_specs=[
            pl.BlockSpec((seq, width), lambda b, g: (b, qcol + g)),
            pl.BlockSpec((seq, width), lambda b, g: (b, kcol + g)),
            pl.BlockSpec((seq, width), lambda b, g: (b, vcol + g)),
            pl.BlockSpec((seq, LANES), lambda b, g: (b, 0)),
            pl.BlockSpec((ATTN_PAIRS, LANES, LANES), lambda b, g: (g, 0, 0)),
            pl.BlockSpec((8, LANES), lambda b, g: (0, 0)),
            pl.BlockSpec((LANES, LANES), lambda b, g: (0, 0)),
        ] + cast_in_specs,
        out_specs=[pl.BlockSpec((seq, width), lambda b, g: (b, g))] + cast_out_specs,
        out_shape=[jax.ShapeDtypeStruct((n, FOX_WIDTH), BF16)] + cast_out_shapes,
        scratch_shapes=[
            pltpu.VMEM((ATTN_PAIRS, seq, 2 * LANES), BF16),
            pltpu.VMEM((ATTN_PAIRS, n_qt, 2, PV_ROWS, TQ), BF16),
            pltpu.VMEM((ATTN_PAIRS, n_qt, 2 * TQ, 2 * LANES), BF16),
            pltpu.VMEM((2, ATTN_PAIRS, seq, 2 * TQ), F32),
        ],
        compiler_params=pltpu.CompilerParams(
            dimension_semantics=("arbitrary", "arbitrary"),
            vmem_limit_bytes=VMEM_LIMIT),
        name="attn",
    )(z, z, z, cparts, place, crow, eye, *weights)
    return outs[0], outs[1:]


def _mix_kernel(u_ref, v_ref, ga_ref, gb_ref, o_ref, x_ref, lng_ref, lnb_ref, ws_ref,
                bias_ref, wa_ref, wb_ref, wo_ref, g2_ref, x1_ref, h2_ref, vn_scr, a_scr):
    v = v_ref[...].astype(F32)
    mu = jnp.mean(v, axis=-1, keepdims=True)
    vc = v - mu
    var = jnp.mean(vc * vc, axis=-1, keepdims=True)
    vn_scr[...] = (vc * lax.rsqrt(var + EPS) * lng_ref[...] + lnb_ref[...]).astype(BF16)

    t_idx = lax.broadcasted_iota(jnp.int32, (GMLP_BLOCK, GMLP_BLOCK), 0)
    s_idx = lax.broadcasted_iota(jnp.int32, (GMLP_BLOCK, GMLP_BLOCK), 1)
    causal = (s_idx // CHUNK) <= (t_idx // CHUNK)
    for g in range(GMLP_GROUPS):
        cols = slice(g * GMLP_BLOCK, (g + 1) * GMLP_BLOCK)
        wg = jnp.where(causal, ws_ref[g], jnp.zeros_like(ws_ref[g]))
        for r in range(TM_MIX // GMLP_BLOCK):
            rows = slice(r * GMLP_BLOCK, (r + 1) * GMLP_BLOCK)
            mixed = jnp.dot(wg, vn_scr[rows, cols], preferred_element_type=F32) + bias_ref[:, cols]
            a_scr[rows, cols] = (u_ref[rows, cols].astype(F32) * mixed).astype(BF16)

    ya = jnp.dot(a_scr[...], wa_ref[...], preferred_element_type=F32)
    yb = jnp.dot(o_ref[...], wb_ref[...], preferred_element_type=F32)
    merged = ga_ref[...].astype(F32) * ya + gb_ref[...].astype(F32) * yb
    x1 = x_ref[...] + jnp.dot(merged.astype(BF16), wo_ref[...], preferred_element_type=F32)
    x1_ref[...] = x1
    h2_ref[...] = _rms(x1, g2_ref[...]).astype(BF16)


def _mix(z, o, x2, ln_g, ln_b, w_s, bias_full, w_a, w_b, w_o, g2):
    n = x2.shape[0]
    row = lambda c: pl.BlockSpec((TM_MIX, D_MODEL), lambda i, c=c: (i, c))
    const = lambda shape: pl.BlockSpec(shape, lambda i: (0,) * len(shape))
    return pl.pallas_call(
        _mix_kernel,
        grid=(n // TM_MIX,),
        in_specs=[
            row(ZB_U), row(ZB_V), row(ZB_GA), row(ZB_GB), row(0), row(0),
            const((1, GMLP_WIDTH)), const((1, GMLP_WIDTH)),
            const((GMLP_GROUPS, GMLP_BLOCK, GMLP_BLOCK)),
            const((GMLP_BLOCK, GMLP_WIDTH)),
            const((GMLP_WIDTH, D_MODEL)), const((FOX_WIDTH, D_MODEL)),
            const((D_MODEL, D_MODEL)), const((1, D_MODEL)),
        ],
        out_specs=[row(0), row(0)],
        out_shape=[
            jax.ShapeDtypeStruct((n, D_MODEL), F32),
            jax.ShapeDtypeStruct((n, D_MODEL), BF16),
        ],
        scratch_shapes=[
            pltpu.VMEM((TM_MIX, GMLP_WIDTH), BF16),
            pltpu.VMEM((TM_MIX, GMLP_WIDTH), BF16),
        ],
        compiler_params=pltpu.CompilerParams(
            dimension_semantics=("arbitrary",),
            vmem_limit_bytes=VMEM_LIMIT),
        name="mix",
    )(z, z, z, z, o, x2, ln_g, ln_b, w_s, bias_full, w_a, w_b, w_o, g2)


def _ffn_kernel(tiles_per_batch, h2_ref, halo_ref, x1_ref, p_ref, wup_ref, cw_ref, cb_ref,
                wd_ref, g3_ref, wpg_ref, wple_ref, gf_ref, out_ref,
                hext_scr, ug_scr, ul_scr, act_scr, acc_scr):
    i = pl.program_id(0)

    @pl.when(pl.program_id(1) == 0)
    def _():
        halo = halo_ref[...]
        hext_scr[:HALO, :] = jnp.where(i % tiles_per_batch == 0, jnp.zeros_like(halo), halo)
        hext_scr[HALO:, :] = h2_ref[...]

    def conv(scr, first_col, lanes):
        cols = slice(first_col + lanes.start, first_col + lanes.stop)
        out = cb_ref[:, cols] + cw_ref[0:1, cols] * scr[pl.ds(HALO - 2, TM_FFN), lanes]
        for t in range(1, CONV_WIDTH):
            out = out + cw_ref[t:t + 1, cols] * scr[pl.ds(HALO - 2 + t, TM_FFN), lanes]
        return out

    def chunk(start, width):
        ug_scr[:, :width] = jnp.dot(hext_scr[...], wup_ref[:, start:start + width],
                                    preferred_element_type=F32)
        ul_scr[:, :width] = jnp.dot(hext_scr[...], wup_ref[:, D_FF + start:D_FF + start + width],
                                    preferred_element_type=F32)
        for c in range(width // LANES):
            lanes = slice(c * LANES, (c + 1) * LANES)
            gate = conv(ug_scr, start, lanes)
            lin = conv(ul_scr, D_FF + start, lanes)
            act_scr[:, lanes] = (jax.nn.gelu(gate) * lin).astype(BF16)
        down = jnp.dot(act_scr[:, :width], wd_ref[start:start + width, :],
                       preferred_element_type=F32)
        if start == 0:
            acc_scr[...] = down
        else:
            acc_scr[...] += down

    j = pl.program_id(1)
    start = 0
    for step, width in enumerate(FFN_CHUNKS):
        pl.when(j == step)(functools.partial(chunk, start, width))
        start += width

    @pl.when(j == len(FFN_CHUNKS) - 1)
    def _():
        x2 = x1_ref[...] + acc_scr[...]
        h3 = _rms(x2, g3_ref[...]).astype(BF16)
        gate = jax.nn.sigmoid(jnp.dot(h3, wpg_ref[...], preferred_element_type=F32))
        ple = jnp.dot(p_ref[...].astype(BF16), wple_ref[...], preferred_element_type=F32)
        x3 = x2 + ple * gate
        out_ref[...] = _rms(x3, gf_ref[...])


def _ffn(h2, x1, p2, w_up, conv_w, conv_b, w_down, g3, w_pg, w_ple, gf, seq):
    n = x1.shape[0]
    halo_blocks = TM_FFN // HALO
    widest = max(FFN_CHUNKS)
    row = lambda width: pl.BlockSpec((TM_FFN, width), lambda i, j: (i, 0))
    const = lambda shape: pl.BlockSpec(shape, lambda i, j: (0,) * len(shape),
                                       pipeline_mode=pl.Buffered(1))
    return pl.pallas_call(
        functools.partial(_ffn_kernel, seq // TM_FFN),
        grid=(n // TM_FFN, len(FFN_CHUNKS)),
        in_specs=[
            row(D_MODEL),
            pl.BlockSpec((HALO, D_MODEL),
                         lambda i, j: (jnp.maximum(i * halo_blocks - 1, 0), 0)),
            row(D_MODEL), row(PLE_DIM),
            const((D_MODEL, 2 * D_FF)), const((CONV_WIDTH, 2 * D_FF)), const((1, 2 * D_FF)),
            const((D_FF, D_MODEL)), const((1, D_MODEL)), const((D_MODEL, D_MODEL)),
            const((PLE_DIM, D_MODEL)), const((1, D_MODEL)),
        ],
        out_specs=row(D_MODEL),
        out_shape=jax.ShapeDtypeStruct((n, D_MODEL), F32),
        scratch_shapes=[
            pltpu.VMEM((HALO + TM_FFN, D_MODEL), BF16),
            pltpu.VMEM((HALO + TM_FFN, widest), F32),
            pltpu.VMEM((HALO + TM_FFN, widest), F32),
            pltpu.VMEM((TM_FFN, widest), BF16),
            pltpu.VMEM((TM_FFN, D_MODEL), F32),
        ],
        compiler_params=pltpu.CompilerParams(
            dimension_semantics=("arbitrary", "arbitrary"),
            vmem_limit_bytes=VMEM_LIMIT),
        name="ffn",
    )(h2, h2, x1, p2, w_up, conv_w, conv_b, w_down, g3, w_pg, w_ple, gf)


def _layer(x2, p2, batch, seq, norm_mix_g, w_in, b_f, gmlp_ln_g, gmlp_ln_b, gmlp_w_s, gmlp_b_s,
           w_branch_a, w_branch_b, w_out, norm_ffn_g, w_up, conv_w, conv_b, w_down,
           norm_ple_g, w_ple, w_ple_gate, out_g):
    o_f = 2 * GMLP_WIDTH + 3 * FOX_WIDTH
    w_all = w_in.astype(BF16)
    w_gates = w_all[:, o_f + FOX_HEADS:]
    w_f = jnp.pad(w_all[:, o_f:o_f + FOX_HEADS], ((0, 0), (0, LANES - FOX_HEADS)))
    b_fp = jnp.pad(b_f, (0, LANES - FOX_HEADS)).reshape(1, LANES)
    tri = jnp.asarray(np.tril(np.ones((CUM_BLK, CUM_BLK), np.float32)), BF16)

    z, cparts = _inproj(x2, norm_mix_g.reshape(1, -1), w_all, w_gates, w_f, b_fp, tri, seq)
    o, (w_a, w_b, w_o, w_u, w_d, w_pg, w_pl) = _attn(
        z, cparts, batch, seq,
        (w_branch_a, w_branch_b, w_out, w_up, w_down, w_ple_gate, w_ple))

    bias_full = jnp.repeat(gmlp_b_s.T, GMLP_BLOCK, axis=1)
    x1, h2 = _mix(z, o, x2, gmlp_ln_g.reshape(1, -1), gmlp_ln_b.reshape(1, -1),
                  gmlp_w_s.astype(BF16), bias_full, w_a, w_b, w_o, norm_ffn_g.reshape(1, -1))

    return _ffn(h2, x1, p2, w_u, conv_w, conv_b.reshape(1, -1), w_d,
                norm_ple_g.reshape(1, -1), w_pg, w_pl, out_g.reshape(1, -1), seq)


def kernel(x, p, norm_mix_g, w_in, b_f, gmlp_ln_g, gmlp_ln_b, gmlp_w_s, gmlp_b_s, w_branch_a,
           w_branch_b, w_out, norm_ffn_g, w_up, conv_w, conv_b, w_down, norm_ple_g, w_ple,
           w_ple_gate, norm_final_g):
    batch, seq, d = x.shape
    depth = p.shape[0]
    assert d == D_MODEL and depth == 1, "the fused final norm assumes a single layer"
    assert seq % TM_IN == 0 and seq % TM_FFN == 0 and seq % TQ == 0
    x2 = x.reshape(batch * seq, d)
    out = _layer(x2, p[0].reshape(batch * seq, PLE_DIM), batch, seq,
                 norm_mix_g[0], w_in[0], b_f[0], gmlp_ln_g[0], gmlp_ln_b[0], gmlp_w_s[0],
                 gmlp_b_s[0], w_branch_a[0], w_branch_b[0], w_out[0], norm_ffn_g[0], w_up[0],
                 conv_w[0], conv_b[0], w_down[0], norm_ple_g[0], w_ple[0], w_ple_gate[0],
                 norm_final_g)
    return out.reshape(batch, seq, d)
```

```python
import functools

import numpy as np
import jax
import jax.numpy as jnp
from jax import lax
from jax.experimental import pallas as pl
from jax.experimental.pallas import tpu as pltpu

F32 = jnp.float32
BF16 = jnp.bfloat16

D_MODEL = 1024
CHUNK = 64
PLE_DIM = 256
EPS = 1e-6
GMLP_GROUPS = 8
GMLP_BLOCK = 128
GMLP_WIDTH = 1024
FOX_HEADS = 16
FOX_HEAD_DIM = 64
FOX_WIDTH = 1024
D_FF = 2816
CONV_WIDTH = 3

LANES = 128
BF16_ROWS = 16
HEAD_PAIRS = FOX_HEADS // 2
N_PARTS = 3
LOG2E = 1.4426950408889634
Q_SCALE = FOX_HEAD_DIM ** -0.5 * LOG2E
PV_ROWS = FOX_HEAD_DIM + 16
BIAS_LANES = 4 * N_PARTS
Q_BIAS_OFFSET = 16

ZB_U, ZB_V, ZB_Q, ZB_K, ZB_VA, ZB_GA, ZB_GB = range(7)
Z_COLS = 7 * D_MODEL

TM_IN = 2048
RC_IN = 256
CUM_BLK = 256
TQ = 256
ATTN_PAIRS = 2
TM_MIX = 512
TM_FFN = 512
FFN_CHUNKS = (1536, 1280)
HALO = 16

VMEM_LIMIT = 56 * 1024 * 1024


def _rms(x, g):
    return x * lax.rsqrt(jnp.mean(x * x, axis=-1, keepdims=True) + EPS) * g


def _split3(x):
    hi = x.astype(BF16)
    r1 = x - hi.astype(F32)
    mid = r1.astype(BF16)
    lo = (r1 - mid.astype(F32)).astype(BF16)
    return hi, mid, lo


def _pack_parts(x):
    hi, mid, lo = (part.astype(F32) for part in _split3(x))
    lane = lax.broadcasted_iota(jnp.int32, x.shape, 1)
    packed = jnp.where(lane < FOX_HEADS, hi,
                       jnp.where(lane < 2 * FOX_HEADS, pltpu.roll(mid, FOX_HEADS, 1),
                                 pltpu.roll(lo, 2 * FOX_HEADS, 1)))
    return packed.astype(BF16)


def _gelu(x):
    a = -2.0 * np.sqrt(2.0 / np.pi) * LOG2E
    e = jnp.exp2(x * (a + (a * 0.044715) * (x * x)))
    return x * (1.0 / (1.0 + e))


def _log_sigmoid(x):
    return jnp.minimum(x, 0.0) - jnp.log1p(jnp.exp(-jnp.abs(x)))


def _inproj_kernel(tiles_per_batch, x_ref, g_ref, w_ref, wg_ref, wf_ref, bf_ref, tri_ref,
                   z_ref, cp_ref, h_scr, carry_scr):
    i = pl.program_id(0)
    j = pl.program_id(1)

    @pl.when(j == 0)
    def _():
        hb = _rms(x_ref[...], g_ref[...]).astype(BF16)
        h_scr[...] = hb
        zf = jnp.dot(hb, wf_ref[...], preferred_element_type=F32)
        logf = _log_sigmoid(zf + bf_ref[...])

        @pl.when(i % tiles_per_batch == 0)
        def _():
            carry_scr[...] = jnp.zeros_like(carry_scr)

        carry = carry_scr[0:1, :]
        for r in range(TM_IN // CUM_BLK):
            rows = slice(r * CUM_BLK, (r + 1) * CUM_BLK)
            parts = jnp.concatenate(_split3(logf[rows]), axis=1)
            cs = jnp.dot(tri_ref[...], parts, preferred_element_type=F32)
            cum = (cs[:, :LANES] + cs[:, LANES:2 * LANES]) + cs[:, 2 * LANES:] + carry
            carry = cum[CUM_BLK - 1:CUM_BLK, :]
            cp_ref[rows, :] = _pack_parts(cum * LOG2E)
        carry_scr[...] = jnp.broadcast_to(carry, carry_scr.shape)

    def project(fn, weights=w_ref, first_block=0):
        cols = pl.ds(pl.multiple_of((j - first_block) * D_MODEL, D_MODEL), D_MODEL)
        for r in range(TM_IN // RC_IN):
            rows = slice(r * RC_IN, (r + 1) * RC_IN)
            acc = jnp.dot(h_scr[rows, :], weights[:, cols], preferred_element_type=F32)
            z_ref[rows, :] = fn(acc).astype(BF16)

    @pl.when(j <= ZB_V)
    def _():
        project(_gelu)

    @pl.when(j == ZB_Q)
    def _():
        project(lambda a: a * Q_SCALE)

    @pl.when(jnp.logical_or(j == ZB_K, j == ZB_VA))
    def _():
        project(lambda a: a)

    @pl.when(j >= ZB_GA)
    def _():
        project(jax.nn.sigmoid, wg_ref, ZB_GA)


def _inproj(x2, g, w_all, w_gates, w_f, b_f, tri, seq):
    n = x2.shape[0]
    grid = (n // TM_IN, Z_COLS // D_MODEL)
    return pl.pallas_call(
        functools.partial(_inproj_kernel, seq // TM_IN),
        grid=grid,
        in_specs=[
            pl.BlockSpec((TM_IN, D_MODEL), lambda i, j: (i, 0)),
            pl.BlockSpec((1, D_MODEL), lambda i, j: (0, 0)),
            pl.BlockSpec(w_all.shape, lambda i, j: (0, 0), pipeline_mode=pl.Buffered(1)),
            pl.BlockSpec(w_gates.shape, lambda i, j: (0, 0), pipeline_mode=pl.Buffered(1)),
            pl.BlockSpec((D_MODEL, LANES), lambda i, j: (0, 0)),
            pl.BlockSpec((1, LANES), lambda i, j: (0, 0)),
            pl.BlockSpec((CUM_BLK, CUM_BLK), lambda i, j: (0, 0)),
        ],
        out_specs=[
            pl.BlockSpec((TM_IN, D_MODEL), lambda i, j: (i, j)),
            pl.BlockSpec((TM_IN, LANES), lambda i, j: (i, 0)),
        ],
        out_shape=[
            jax.ShapeDtypeStruct((n, Z_COLS), BF16),
            jax.ShapeDtypeStruct((n, LANES), BF16),
        ],
        scratch_shapes=[
            pltpu.VMEM((TM_IN, D_MODEL), BF16),
            pltpu.VMEM((8, LANES), F32),
        ],
        compiler_params=pltpu.CompilerParams(
            dimension_semantics=("arbitrary", "arbitrary"),
            vmem_limit_bytes=VMEM_LIMIT),
        name="inproj",
    )(x2, g, w_all, w_gates, w_f, b_f, tri)


def _attn_kernel(n_qt, n_cast, q_ref, k_ref, v_ref, cp_ref, place_ref, crow_ref, eye_ref, *rest):
    cast_in, (o_ref, *cast_out) = rest[:n_cast], rest[n_cast:2 * n_cast + 1]
    kaug_scr, vt_scr, qr_scr, s_scr = rest[2 * n_cast + 1:]
    for src, dst in zip(cast_in, cast_out):
        dst[...] = src[...].astype(BF16)

    nt = (((1,), (1,)), ((), ()))
    chains = range(ATTN_PAIRS)
    lane = lax.broadcasted_iota(jnp.int32, (TQ, LANES), 1)

    for c in chains:
        lanes = slice(c * LANES, (c + 1) * LANES)
        aug = jnp.dot(cp_ref[...], place_ref[c], preferred_element_type=F32) + crow_ref[0:1, :]
        kaug_scr[c, :, :LANES] = k_ref[:, lanes]
        kaug_scr[c, :, LANES:] = aug.astype(BF16)
        bias_lane = lax.broadcasted_iota(jnp.int32, aug.shape, 1) < BIAS_LANES
        augq = [jnp.where(bias_lane, pltpu.roll(aug, LANES - Q_BIAS_OFFSET * (1 + hh), 1), 0.0)
                .astype(BF16) for hh in range(2)]
        ones = jnp.ones((PV_ROWS - FOX_HEAD_DIM, TQ), BF16)
        for t in range(n_qt):
            rows = slice(t * TQ, (t + 1) * TQ)
            vt = lax.dot_general(eye_ref[...], v_ref[rows, lanes], nt,
                                 preferred_element_type=F32).astype(BF16)
            for hh in range(2):
                vt_scr[c, t, hh, :FOX_HEAD_DIM, :] = vt[hh * FOX_HEAD_DIM:(hh + 1) * FOX_HEAD_DIM]
                vt_scr[c, t, hh, FOX_HEAD_DIM:, :] = ones
            q = q_ref[rows, lanes]
            zero = jnp.zeros_like(q)
            for hh in range(2):
                own = (lane >= FOX_HEAD_DIM) if hh else (lane < FOX_HEAD_DIM)
                qrows = slice(hh * TQ, (hh + 1) * TQ)
                qr_scr[c, t, qrows, :LANES] = jnp.where(own, q, zero)
                qr_scr[c, t, qrows, LANES:] = augq[hh][rows]

    kpos = lax.broadcasted_iota(jnp.int32, (TQ, 2 * TQ), 0)
    col = lax.broadcasted_iota(jnp.int32, (TQ, 2 * TQ), 1)
    causal = kpos <= jnp.where(col >= TQ, col - TQ, col)
    def pass1(qi):
        nk = (qi + 1) * TQ
        smax = []
        for c in chains:
            s = lax.dot_general(kaug_scr[c, :nk, :], qr_scr[c, qi], nt,
                                preferred_element_type=F32)
            diag = jnp.where(causal, s[nk - TQ:], -1e30)
            m = jnp.max(diag, axis=0, keepdims=True)
            if qi:
                s_scr[qi % 2, c, :nk - TQ, :] = s[:nk - TQ]
                m = jnp.maximum(m, jnp.max(s[:nk - TQ], axis=0, keepdims=True))
            s_scr[qi % 2, c, nk - TQ:nk, :] = diag
            smax.append(m)
        return smax

    def pass2(qi, smax):
        for c in chains:
            acc = [None, None]
            for t in range(qi + 1):
                p = jnp.exp2(s_scr[qi % 2, c, t * TQ:(t + 1) * TQ, :] - smax[c]).astype(BF16)
                for hh in range(2):
                    d = jnp.dot(vt_scr[c, t, hh], p[:, hh * TQ:(hh + 1) * TQ],
                                preferred_element_type=F32)
                    acc[hh] = d if acc[hh] is None else acc[hh] + d
            heads = [a[:FOX_HEAD_DIM] * (1.0 / a[FOX_HEAD_DIM:FOX_HEAD_DIM + 1]) for a in acc]
            ot = jnp.concatenate(heads, axis=0)
            o_ref[qi * TQ:(qi + 1) * TQ, c * LANES:(c + 1) * LANES] = ot.T.astype(BF16)

    smax = pass1(0)
    for qi in range(n_qt):
        nxt = pass1(qi + 1) if qi + 1 < n_qt else None
        pass2(qi, smax)
        smax = nxt


def _attn_consts():
    place = np.zeros((HEAD_PAIRS, LANES, LANES), np.float32)
    crow = np.zeros((8, LANES), np.float32)
    q0, q1 = Q_BIAS_OFFSET, 2 * Q_BIAS_OFFSET
    for p in range(HEAD_PAIRS):
        for a in range(N_PARTS):
            place[p, a * FOX_HEADS + 2 * p, 0 + a] = -1.0
            place[p, a * FOX_HEADS + 2 * p + 1, 6 + a] = -1.0
            place[p, a * FOX_HEADS + 2 * p, q0 + 3 + a] = 1.0
            place[p, a * FOX_HEADS + 2 * p + 1, q1 + 9 + a] = 1.0
    crow[0, 3:6] = 1.0
    crow[0, 9:12] = 1.0
    crow[0, q0 + 0:q0 + 3] = 1.0
    crow[0, q1 + 6:q1 + 9] = 1.0
    eye = np.eye(LANES, dtype=np.float32)
    return jnp.asarray(place, BF16), jnp.asarray(crow, F32), jnp.asarray(eye, BF16)


def _attn(z, cparts, batch, seq, weights):
    n = z.shape[0]
    n_qt = seq // TQ
    place, crow, eye = _attn_consts()
    n_groups = HEAD_PAIRS // ATTN_PAIRS
    cast_in_specs, cast_out_specs, cast_out_shapes = [], [], []
    for w in weights:
        rows, cols = w.shape
        n_slabs = max(s for s in range(1, batch * n_groups + 1)
                      if rows % s == 0 and (rows // s) % BF16_ROWS == 0)
        slab = rows // n_slabs
        index = lambda b, g, last=n_slabs - 1: (jnp.minimum(b * n_groups + g, last), 0)
        cast_in_specs.append(pl.BlockSpec((slab, cols), index))
        cast_out_specs.append(pl.BlockSpec((slab, cols), index))
        cast_out_shapes.append(jax.ShapeDtypeStruct(w.shape, BF16))
    width = ATTN_PAIRS * LANES
    qcol = ZB_Q * D_MODEL // width
    kcol = ZB_K * D_MODEL // width
    vcol = ZB_VA * D_MODEL // width
    outs = pl.pallas_call(
        functools.partial(_attn_kernel, n_qt, len(weights)),
        grid=(batch, n_groups),
        in_specs=[
            pl.BlockSpec((seq, width), lambda b, g: (b, qcol + g)),
            pl.BlockSpec((seq, width), lambda b, g: (b, kcol + g)),
            pl.BlockSpec((seq, width), lambda b, g: (b, vcol + g)),
            pl.BlockSpec((seq, LANES), lambda b, g: (b, 0)),
            pl.BlockSpec((ATTN_PAIRS, LANES, LANES), lambda b, g: (g, 0, 0)),
            pl.BlockSpec((8, LANES), lambda b, g: (0, 0)),
            pl.BlockSpec((LANES, LANES), lambda b, g: (0, 0)),
        ] + cast_in_specs,
        out_specs=[pl.BlockSpec((seq, width), lambda b, g: (b, g))] + cast_out_specs,
        out_shape=[jax.ShapeDtypeStruct((n, FOX_WIDTH), BF16)] + cast_out_shapes,
        scratch_shapes=[
            pltpu.VMEM((ATTN_PAIRS, seq, 2 * LANES), BF16),
            pltpu.VMEM((ATTN_PAIRS, n_qt, 2, PV_ROWS, TQ), BF16),
            pltpu.VMEM((ATTN_PAIRS, n_qt, 2 * TQ, 2 * LANES), BF16),
            pltpu.VMEM((2, ATTN_PAIRS, seq, 2 * TQ), F32),
        ],
        compiler_params=pltpu.CompilerParams(
            dimension_semantics=("arbitrary", "arbitrary"),
            vmem_limit_bytes=VMEM_LIMIT),
        name="attn",
    )(z, z, z, cparts, place, crow, eye, *weights)
    return outs[0], outs[1:]


def _mix_kernel(u_ref, v_ref, ga_ref, gb_ref, o_ref, x_ref, lng_ref, lnb_ref, ws_ref,
                bias_ref, wa_ref, wb_ref, wo_ref, x1_ref, vn_scr, a_scr):
    v = v_ref[...].astype(F32)
    mu = jnp.mean(v, axis=-1, keepdims=True)
    vc = v - mu
    var = jnp.mean(vc * vc, axis=-1, keepdims=True)
    vn_scr[...] = (vc * lax.rsqrt(var + EPS) * lng_ref[...] + lnb_ref[...]).astype(BF16)

    t_idx = lax.broadcasted_iota(jnp.int32, (GMLP_BLOCK, GMLP_BLOCK), 0)
    s_idx = lax.broadcasted_iota(jnp.int32, (GMLP_BLOCK, GMLP_BLOCK), 1)
    causal = (s_idx // CHUNK) <= (t_idx // CHUNK)
    for g in range(GMLP_GROUPS):
        cols = slice(g * GMLP_BLOCK, (g + 1) * GMLP_BLOCK)
        wg = jnp.where(causal, ws_ref[g], jnp.zeros_like(ws_ref[g]))
        for r in range(TM_MIX // GMLP_BLOCK):
            rows = slice(r * GMLP_BLOCK, (r + 1) * GMLP_BLOCK)
            mixed = jnp.dot(wg, vn_scr[rows, cols], preferred_element_type=F32) + bias_ref[:, cols]
            a_scr[rows, cols] = (u_ref[rows, cols].astype(F32) * mixed).astype(BF16)

    ya = jnp.dot(a_scr[...], wa_ref[...], preferred_element_type=F32)
    yb = jnp.dot(o_ref[...], wb_ref[...], preferred_element_type=F32)
    merged = ga_ref[...].astype(F32) * ya + gb_ref[...].astype(F32) * yb
    x1_ref[...] = x_ref[...] + jnp.dot(merged.astype(BF16), wo_ref[...],
                                       preferred_element_type=F32)


def _mix(z, o, x2, ln_g, ln_b, w_s, bias_full, w_a, w_b, w_o):
    n = x2.shape[0]
    row = lambda c: pl.BlockSpec((TM_MIX, D_MODEL), lambda i, c=c: (i, c))
    const = lambda shape: pl.BlockSpec(shape, lambda i: (0,) * len(shape))
    return pl.pallas_call(
        _mix_kernel,
        grid=(n // TM_MIX,),
        in_specs=[
            row(ZB_U), row(ZB_V), row(ZB_GA), row(ZB_GB), row(0), row(0),
            const((1, GMLP_WIDTH)), const((1, GMLP_WIDTH)),
            const((GMLP_GROUPS, GMLP_BLOCK, GMLP_BLOCK)),
            const((GMLP_BLOCK, GMLP_WIDTH)),
            const((GMLP_WIDTH, D_MODEL)), const((FOX_WIDTH, D_MODEL)),
            const((D_MODEL, D_MODEL)),
        ],
        out_specs=row(0),
        out_shape=jax.ShapeDtypeStruct((n, D_MODEL), F32),
        scratch_shapes=[
            pltpu.VMEM((TM_MIX, GMLP_WIDTH), BF16),
            pltpu.VMEM((TM_MIX, GMLP_WIDTH), BF16),
        ],
        compiler_params=pltpu.CompilerParams(
            dimension_semantics=("arbitrary",),
            vmem_limit_bytes=VMEM_LIMIT),
        name="mix",
    )(z, z, z, z, o, x2, ln_g, ln_b, w_s, bias_full, w_a, w_b, w_o)


def _ffn_kernel(tiles_per_batch, halo_ref, x1_ref, p_ref, g2_ref, wup_ref, cw_ref, cb_ref,
                wd_ref, g3_ref, wpg_ref, wple_ref, gf_ref, out_ref,
                hext_scr, ug_scr, ul_scr, act_scr, acc_scr):
    i = pl.program_id(0)

    @pl.when(pl.program_id(1) == 0)
    def _():
        halo = _rms(halo_ref[...], g2_ref[...]).astype(BF16)
        hext_scr[:HALO, :] = jnp.where(i % tiles_per_batch == 0, jnp.zeros_like(halo), halo)
        hext_scr[HALO:, :] = _rms(x1_ref[...], g2_ref[...]).astype(BF16)

    def conv(scr, first_col, lanes):
        cols = slice(first_col + lanes.start, first_col + lanes.stop)
        out = cb_ref[:, cols] + cw_ref[0:1, cols] * scr[pl.ds(HALO - 2, TM_FFN), lanes]
        for t in range(1, CONV_WIDTH):
            out = out + cw_ref[t:t + 1, cols] * scr[pl.ds(HALO - 2 + t, TM_FFN), lanes]
        return out

    def chunk(start, width):
        ug_scr[:, :width] = jnp.dot(hext_scr[...], wup_ref[:, start:start + width],
                                    preferred_element_type=F32)
        ul_scr[:, :width] = jnp.dot(hext_scr[...], wup_ref[:, D_FF + start:D_FF + start + width],
                                    preferred_element_type=F32)
        for c in range(width // LANES):
            lanes = slice(c * LANES, (c + 1) * LANES)
            gate = conv(ug_scr, start, lanes)
            lin = conv(ul_scr, D_FF + start, lanes)
            act_scr[:, lanes] = (jax.nn.gelu(gate) * lin).astype(BF16)
        down = jnp.dot(act_scr[:, :width], wd_ref[start:start + width, :],
                       preferred_element_type=F32)
        if start == 0:
            acc_scr[...] = down
        else:
            acc_scr[...] += down

    j = pl.program_id(1)
    start = 0
    for step, width in enumerate(FFN_CHUNKS):
        pl.when(j == step)(functools.partial(chunk, start, width))
        start += width

    @pl.when(j == len(FFN_CHUNKS) - 1)
    def _():
        x2 = x1_ref[...] + acc_scr[...]
        h3 = _rms(x2, g3_ref[...]).astype(BF16)
        gate = jax.nn.sigmoid(jnp.dot(h3, wpg_ref[...], preferred_element_type=F32))
        ple = jnp.dot(p_ref[...].astype(BF16), wple_ref[...], preferred_element_type=F32)
        x3 = x2 + ple * gate
        out_ref[...] = _rms(x3, gf_ref[...])


def _ffn(x1, p2, g2, w_up, conv_w, conv_b, w_down, g3, w_pg, w_ple, gf, seq):
    n = x1.shape[0]
    halo_blocks = TM_FFN // HALO
    widest = max(FFN_CHUNKS)
    row = lambda width: pl.BlockSpec((TM_FFN, width), lambda i, j: (i, 0))
    const = lambda shape: pl.BlockSpec(shape, lambda i, j: (0,) * len(shape),
                                       pipeline_mode=pl.Buffered(1))
    return pl.pallas_call(
        functools.partial(_ffn_kernel, seq // TM_FFN),
        grid=(n // TM_FFN, len(FFN_CHUNKS)),
        in_specs=[
            pl.BlockSpec((HALO, D_MODEL),
                         lambda i, j: (jnp.maximum(i * halo_blocks - 1, 0), 0)),
            row(D_MODEL), row(PLE_DIM), const((1, D_MODEL)),
            const((D_MODEL, 2 * D_FF)), const((CONV_WIDTH, 2 * D_FF)), const((1, 2 * D_FF)),
            const((D_FF, D_MODEL)), const((1, D_MODEL)), const((D_MODEL, D_MODEL)),
            const((PLE_DIM, D_MODEL)), const((1, D_MODEL)),
        ],
        out_specs=row(D_MODEL),
        out_shape=jax.ShapeDtypeStruct((n, D_MODEL), F32),
        scratch_shapes=[
            pltpu.VMEM((HALO + TM_FFN, D_MODEL), BF16),
            pltpu.VMEM((HALO + TM_FFN, widest), F32),
            pltpu.VMEM((HALO + TM_FFN, widest), F32),
            pltpu.VMEM((TM_FFN, widest), BF16),
            pltpu.VMEM((TM_FFN, D_MODEL), F32),
        ],
        compiler_params=pltpu.CompilerParams(
            dimension_semantics=("arbitrary", "arbitrary"),
            vmem_limit_bytes=VMEM_LIMIT),
        name="ffn",
    )(x1, x1, p2, g2, w_up, conv_w, conv_b, w_down, g3, w_pg, w_ple, gf)


def _layer(x2, p2, batch, seq, norm_mix_g, w_in, b_f, gmlp_ln_g, gmlp_ln_b, gmlp_w_s, gmlp_b_s,
           w_branch_a, w_branch_b, w_out, norm_ffn_g, w_up, conv_w, conv_b, w_down,
           norm_ple_g, w_ple, w_ple_gate, out_g):
    o_f = 2 * GMLP_WIDTH + 3 * FOX_WIDTH
    w_all = w_in.astype(BF16)
    w_gates = w_all[:, o_f + FOX_HEADS:]
    w_f = jnp.pad(w_all[:, o_f:o_f + FOX_HEADS], ((0, 0), (0, LANES - FOX_HEADS)))
    b_fp = jnp.pad(b_f, (0, LANES - FOX_HEADS)).reshape(1, LANES)
    tri = jnp.asarray(np.tril(np.ones((CUM_BLK, CUM_BLK), np.float32)), BF16)

    z, cparts = _inproj(x2, norm_mix_g.reshape(1, -1), w_all, w_gates, w_f, b_fp, tri, seq)
    o, (w_a, w_b, w_o, w_u, w_d, w_pg, w_pl) = _attn(
        z, cparts, batch, seq,
        (w_branch_a, w_branch_b, w_out, w_up, w_down, w_ple_gate, w_ple))

    bias_full = jnp.repeat(gmlp_b_s.T, GMLP_BLOCK, axis=1)
    x1 = _mix(z, o, x2, gmlp_ln_g.reshape(1, -1), gmlp_ln_b.reshape(1, -1),
              gmlp_w_s.astype(BF16), bias_full, w_a, w_b, w_o)

    return _ffn(x1, p2, norm_ffn_g.reshape(1, -1), w_u, conv_w, conv_b.reshape(1, -1), w_d,
                norm_ple_g.reshape(1, -1), w_pg, w_pl, out_g.reshape(1, -1), seq)


def kernel(x, p, norm_mix_g, w_in, b_f, gmlp_ln_g, gmlp_ln_b, gmlp_w_s, gmlp_b_s, w_branch_a,
           w_branch_b, w_out, norm_ffn_g, w_up, conv_w, conv_b, w_down, norm_ple_g, w_ple,
           w_ple_gate, norm_final_g):
    batch, seq, d = x.shape
    depth = p.shape[0]
    assert d == D_MODEL and depth == 1, "the fused final norm assumes a single layer"
    assert seq % TM_IN == 0 and seq % TM_FFN == 0 and seq % TQ == 0
    x2 = x.reshape(batch * seq, d)
    out = _layer(x2, p[0].reshape(batch * seq, PLE_DIM), batch, seq,
                 norm_mix_g[0], w_in[0], b_f[0], gmlp_ln_g[0], gmlp_ln_b[0], gmlp_w_s[0],
                 gmlp_b_s[0], w_branch_a[0], w_branch_b[0], w_out[0], norm_ffn_g[0], w_up[0],
                 conv_w[0], conv_b[0], w_down[0], norm_ple_g[0], w_ple[0], w_ple_gate[0],
                 norm_final_g)
    return out.reshape(batch, seq, d)
```

```python
import functools

import numpy as np
import jax
import jax.numpy as jnp
from jax import lax
from jax.experimental import pallas as pl
from jax.experimental.pallas import tpu as pltpu

F32 = jnp.float32
BF16 = jnp.bfloat16

D_MODEL = 1024
CHUNK = 64
PLE_DIM = 256
EPS = 1e-6
GMLP_GROUPS = 8
GMLP_BLOCK = 128
GMLP_WIDTH = 1024
FOX_HEADS = 16
FOX_HEAD_DIM = 64
FOX_WIDTH = 1024
D_FF = 2816
CONV_WIDTH = 3

LANES = 128
BF16_ROWS = 16
HEAD_PAIRS = FOX_HEADS // 2
N_PARTS = 3
LOG2E = 1.4426950408889634
Q_SCALE = FOX_HEAD_DIM ** -0.5 * LOG2E
PV_ROWS = FOX_HEAD_DIM + 16
BIAS_LANES = 4 * N_PARTS
Q_BIAS_OFFSET = 16

ZB_U, ZB_V, ZB_Q, ZB_K, ZB_VA, ZB_GA, ZB_GB = range(7)
Z_COLS = 7 * D_MODEL

TM_IN = 2048
RC_IN = 256
CUM_BLK = 256
TQ = 256
ATTN_PAIRS = 2
TM_MIX = 1024
TM_FFN = 512
FFN_CHUNKS = (1536, 1280)
HALO = 16

VMEM_LIMIT = 56 * 1024 * 1024


def _rms(x, g):
    return x * lax.rsqrt(jnp.mean(x * x, axis=-1, keepdims=True) + EPS) * g


def _split3(x):
    hi = x.astype(BF16)
    r1 = x - hi.astype(F32)
    mid = r1.astype(BF16)
    lo = (r1 - mid.astype(F32)).astype(BF16)
    return hi, mid, lo


def _pack_parts(x):
    hi, mid, lo = (part.astype(F32) for part in _split3(x))
    lane = lax.broadcasted_iota(jnp.int32, x.shape, 1)
    packed = jnp.where(lane < FOX_HEADS, hi,
                       jnp.where(lane < 2 * FOX_HEADS, pltpu.roll(mid, FOX_HEADS, 1),
                                 pltpu.roll(lo, 2 * FOX_HEADS, 1)))
    return packed.astype(BF16)


def _gelu(x):
    a = -2.0 * np.sqrt(2.0 / np.pi) * LOG2E
    e = jnp.exp2(x * (a + (a * 0.044715) * (x * x)))
    return x * (1.0 / (1.0 + e))


def _log_sigmoid(x):
    return jnp.minimum(x, 0.0) - jnp.log1p(jnp.exp(-jnp.abs(x)))


def _inproj_kernel(tiles_per_batch, x_ref, g_ref, w_ref, wg_ref, wf_ref, bf_ref, tri_ref,
                   z_ref, cp_ref, h_scr, carry_scr):
    i = pl.program_id(0)
    j = pl.program_id(1)

    @pl.when(j == 0)
    def _():
        hb = _rms(x_ref[...], g_ref[...]).astype(BF16)
        h_scr[...] = hb
        zf = jnp.dot(hb, wf_ref[...], preferred_element_type=F32)
        logf = _log_sigmoid(zf + bf_ref[...])

        @pl.when(i % tiles_per_batch == 0)
        def _():
            carry_scr[...] = jnp.zeros_like(carry_scr)

        carry = carry_scr[0:1, :]
        for r in range(TM_IN // CUM_BLK):
            rows = slice(r * CUM_BLK, (r + 1) * CUM_BLK)
            parts = jnp.concatenate(_split3(logf[rows]), axis=1)
            cs = jnp.dot(tri_ref[...], parts, preferred_element_type=F32)
            cum = (cs[:, :LANES] + cs[:, LANES:2 * LANES]) + cs[:, 2 * LANES:] + carry
            carry = cum[CUM_BLK - 1:CUM_BLK, :]
            cp_ref[rows, :] = _pack_parts(cum * LOG2E)
        carry_scr[...] = jnp.broadcast_to(carry, carry_scr.shape)

    def project(fn, weights=w_ref, first_block=0):
        cols = pl.ds(pl.multiple_of((j - first_block) * D_MODEL, D_MODEL), D_MODEL)
        for r in range(TM_IN // RC_IN):
            rows = slice(r * RC_IN, (r + 1) * RC_IN)
            acc = jnp.dot(h_scr[rows, :], weights[:, cols], preferred_element_type=F32)
            z_ref[rows, :] = fn(acc).astype(BF16)

    @pl.when(j <= ZB_V)
    def _():
        project(_gelu)

    @pl.when(j == ZB_Q)
    def _():
        project(lambda a: a * Q_SCALE)

    @pl.when(jnp.logical_or(j == ZB_K, j == ZB_VA))
    def _():
        project(lambda a: a)

    @pl.when(j >= ZB_GA)
    def _():
        project(jax.nn.sigmoid, wg_ref, ZB_GA)


def _inproj(x2, g, w_all, w_gates, w_f, b_f, tri, seq):
    n = x2.shape[0]
    grid = (n // TM_IN, Z_COLS // D_MODEL)
    return pl.pallas_call(
        functools.partial(_inproj_kernel, seq // TM_IN),
        grid=grid,
        in_specs=[
            pl.BlockSpec((TM_IN, D_MODEL), lambda i, j: (i, 0)),
            pl.BlockSpec((1, D_MODEL), lambda i, j: (0, 0)),
            pl.BlockSpec(w_all.shape, lambda i, j: (0, 0), pipeline_mode=pl.Buffered(1)),
            pl.BlockSpec(w_gates.shape, lambda i, j: (0, 0), pipeline_mode=pl.Buffered(1)),
            pl.BlockSpec((D_MODEL, LANES), lambda i, j: (0, 0)),
            pl.BlockSpec((1, LANES), lambda i, j: (0, 0)),
            pl.BlockSpec((CUM_BLK, CUM_BLK), lambda i, j: (0, 0)),
        ],
        out_specs=[
            pl.BlockSpec((TM_IN, D_MODEL), lambda i, j: (i, j)),
            pl.BlockSpec((TM_IN, LANES), lambda i, j: (i, 0)),
        ],
        out_shape=[
            jax.ShapeDtypeStruct((n, Z_COLS), BF16),
            jax.ShapeDtypeStruct((n, LANES), BF16),
        ],
        scratch_shapes=[
            pltpu.VMEM((TM_IN, D_MODEL), BF16),
            pltpu.VMEM((8, LANES), F32),
        ],
        compiler_params=pltpu.CompilerParams(
            dimension_semantics=("arbitrary", "arbitrary"),
            vmem_limit_bytes=VMEM_LIMIT),
        name="inproj",
    )(x2, g, w_all, w_gates, w_f, b_f, tri)


def _attn_kernel(n_qt, n_cast, q_ref, k_ref, v_ref, cp_ref, place_ref, crow_ref, eye_ref, *rest):
    cast_in, (o_ref, *cast_out) = rest[:n_cast], rest[n_cast:2 * n_cast + 1]
    kaug_scr, vt_scr, qr_scr, s_scr = rest[2 * n_cast + 1:]
    for src, dst in zip(cast_in, cast_out):
        dst[...] = src[...].astype(BF16)

    nt = (((1,), (1,)), ((), ()))
    chains = range(ATTN_PAIRS)
    lane = lax.broadcasted_iota(jnp.int32, (TQ, LANES), 1)

    for c in chains:
        lanes = slice(c * LANES, (c + 1) * LANES)
        aug = jnp.dot(cp_ref[...], place_ref[c], preferred_element_type=F32) + crow_ref[0:1, :]
        kaug_scr[c, :, :LANES] = k_ref[:, lanes]
        kaug_scr[c, :, LANES:] = aug.astype(BF16)
        bias_lane = lax.broadcasted_iota(jnp.int32, aug.shape, 1) < BIAS_LANES
        augq = [jnp.where(bias_lane, pltpu.roll(aug, LANES - Q_BIAS_OFFSET * (1 + hh), 1), 0.0)
                .astype(BF16) for hh in range(2)]
        ones = jnp.ones((PV_ROWS - FOX_HEAD_DIM, TQ), BF16)
        for t in range(n_qt):
            rows = slice(t * TQ, (t + 1) * TQ)
            vt = lax.dot_general(eye_ref[...], v_ref[rows, lanes], nt,
                                 preferred_element_type=F32).astype(BF16)
            for hh in range(2):
                vt_scr[c, t, hh, :FOX_HEAD_DIM, :] = vt[hh * FOX_HEAD_DIM:(hh + 1) * FOX_HEAD_DIM]
                vt_scr[c, t, hh, FOX_HEAD_DIM:, :] = ones
            q = q_ref[rows, lanes]
            zero = jnp.zeros_like(q)
            for hh in range(2):
                own = (lane >= FOX_HEAD_DIM) if hh else (lane < FOX_HEAD_DIM)
                qrows = slice(hh * TQ, (hh + 1) * TQ)
                qr_scr[c, t, qrows, :LANES] = jnp.where(own, q, zero)
                qr_scr[c, t, qrows, LANES:] = augq[hh][rows]

    kpos = lax.broadcasted_iota(jnp.int32, (TQ, 2 * TQ), 0)
    col = lax.broadcasted_iota(jnp.int32, (TQ, 2 * TQ), 1)
    causal = kpos <= jnp.where(col >= TQ, col - TQ, col)
    def pass1(qi):
        nk = (qi + 1) * TQ
        smax = []
        for c in chains:
            s = lax.dot_general(kaug_scr[c, :nk, :], qr_scr[c, qi], nt,
                                preferred_element_type=F32)
            diag = jnp.where(causal, s[nk - TQ:], -1e30)
            m = jnp.max(diag, axis=0, keepdims=True)
            if qi:
                s_scr[qi % 2, c, :nk - TQ, :] = s[:nk - TQ]
                m = jnp.maximum(m, jnp.max(s[:nk - TQ], axis=0, keepdims=True))
            s_scr[qi % 2, c, nk - TQ:nk, :] = diag
            smax.append(m)
        return smax

    def pass2(qi, smax):
        for c in chains:
            acc = [None, None]
            for t in range(qi + 1):
                p = jnp.exp2(s_scr[qi % 2, c, t * TQ:(t + 1) * TQ, :] - smax[c]).astype(BF16)
                for hh in range(2):
                    d = jnp.dot(vt_scr[c, t, hh], p[:, hh * TQ:(hh + 1) * TQ],
                                preferred_element_type=F32)
                    acc[hh] = d if acc[hh] is None else acc[hh] + d
            heads = [a[:FOX_HEAD_DIM] * (1.0 / a[FOX_HEAD_DIM:FOX_HEAD_DIM + 1]) for a in acc]
            ot = jnp.concatenate(heads, axis=0)
            o_ref[qi * TQ:(qi + 1) * TQ, c * LANES:(c + 1) * LANES] = ot.T.astype(BF16)

    smax = pass1(0)
    for qi in range(n_qt):
        nxt = pass1(qi + 1) if qi + 1 < n_qt else None
        pass2(qi, smax)
        smax = nxt


def _attn_consts():
    place = np.zeros((HEAD_PAIRS, LANES, LANES), np.float32)
    crow = np.zeros((8, LANES), np.float32)
    q0, q1 = Q_BIAS_OFFSET, 2 * Q_BIAS_OFFSET
    for p in range(HEAD_PAIRS):
        for a in range(N_PARTS):
            place[p, a * FOX_HEADS + 2 * p, 0 + a] = -1.0
            place[p, a * FOX_HEADS + 2 * p + 1, 6 + a] = -1.0
            place[p, a * FOX_HEADS + 2 * p, q0 + 3 + a] = 1.0
            place[p, a * FOX_HEADS + 2 * p + 1, q1 + 9 + a] = 1.0
    crow[0, 3:6] = 1.0
    crow[0, 9:12] = 1.0
    crow[0, q0 + 0:q0 + 3] = 1.0
    crow[0, q1 + 6:q1 + 9] = 1.0
    eye = np.eye(LANES, dtype=np.float32)
    return jnp.asarray(place, BF16), jnp.asarray(crow, F32), jnp.asarray(eye, BF16)


def _attn(z, cparts, batch, seq, weights):
    n = z.shape[0]
    n_qt = seq // TQ
    place, crow, eye = _attn_consts()
    n_groups = HEAD_PAIRS // ATTN_PAIRS
    cast_in_specs, cast_out_specs, cast_out_shapes = [], [], []
    for w in weights:
        rows, cols = w.shape
        n_slabs = max(s for s in range(1, batch * n_groups + 1)
                      if rows % s == 0 and (rows // s) % BF16_ROWS == 0)
        slab = rows // n_slabs
        index = lambda b, g, last=n_slabs - 1: (jnp.minimum(b * n_groups + g, last), 0)
        cast_in_specs.append(pl.BlockSpec((slab, cols), index))
        cast_out_specs.append(pl.BlockSpec((slab, cols), index))
        cast_out_shapes.append(jax.ShapeDtypeStruct(w.shape, BF16))
    width = ATTN_PAIRS * LANES
    qcol = ZB_Q * D_MODEL // width
    kcol = ZB_K * D_MODEL // width
    vcol = ZB_VA * D_MODEL // width
    outs = pl.pallas_call(
        functools.partial(_attn_kernel, n_qt, len(weights)),
        grid=(batch, n_groups),
        in_specs=[
            pl.BlockSpec((seq, width), lambda b, g: (b, qcol + g)),
            pl.BlockSpec((seq, width), lambda b, g: (b, kcol + g)),
            pl.BlockSpec((seq, width), lambda b, g: (b, vcol + g)),
            pl.BlockSpec((seq, LANES), lambda b, g: (b, 0)),
            pl.BlockSpec((ATTN_PAIRS, LANES, LANES), lambda b, g: (g, 0, 0)),
            pl.BlockSpec((8, LANES), lambda b, g: (0, 0)),
            pl.BlockSpec((LANES, LANES), lambda b, g: (0, 0)),
        ] + cast_in_specs,
        out_specs=[pl.BlockSpec((seq, width), lambda b, g: (b, g))] + cast_out_specs,
        out_shape=[jax.ShapeDtypeStruct((n, FOX_WIDTH), BF16)] + cast_out_shapes,
        scratch_shapes=[
            pltpu.VMEM((ATTN_PAIRS, seq, 2 * LANES), BF16),
            pltpu.VMEM((ATTN_PAIRS, n_qt, 2, PV_ROWS, TQ), BF16),
            pltpu.VMEM((ATTN_PAIRS, n_qt, 2 * TQ, 2 * LANES), BF16),
            pltpu.VMEM((2, ATTN_PAIRS, seq, 2 * TQ), F32),
        ],
        compiler_params=pltpu.CompilerParams(
            dimension_semantics=("arbitrary", "arbitrary"),
            vmem_limit_bytes=VMEM_LIMIT),
        name="attn",
    )(z, z, z, cparts, place, crow, eye, *weights)
    return outs[0], outs[1:]


def _mix_kernel(u_ref, v_ref, ga_ref, gb_ref, o_ref, x_ref, lng_ref, lnb_ref, ws_ref,
                bias_ref, wa_ref, wb_ref, wo_ref, x1_ref, vn_scr, a_scr):
    v = v_ref[...].astype(F32)
    mu = jnp.mean(v, axis=-1, keepdims=True)
    vc = v - mu
    var = jnp.mean(vc * vc, axis=-1, keepdims=True)
    vn_scr[...] = (vc * lax.rsqrt(var + EPS) * lng_ref[...] + lnb_ref[...]).astype(BF16)

    t_idx = lax.broadcasted_iota(jnp.int32, (GMLP_BLOCK, GMLP_BLOCK), 0)
    s_idx = lax.broadcasted_iota(jnp.int32, (GMLP_BLOCK, GMLP_BLOCK), 1)
    causal = (s_idx // CHUNK) <= (t_idx // CHUNK)
    for g in range(GMLP_GROUPS):
        cols = slice(g * GMLP_BLOCK, (g + 1) * GMLP_BLOCK)
        wg = jnp.where(causal, ws_ref[g], jnp.zeros_like(ws_ref[g]))
        for r in range(TM_MIX // GMLP_BLOCK):
            rows = slice(r * GMLP_BLOCK, (r + 1) * GMLP_BLOCK)
            mixed = jnp.dot(wg, vn_scr[rows, cols], preferred_element_type=F32) + bias_ref[:, cols]
            a_scr[rows, cols] = (u_ref[rows, cols].astype(F32) * mixed).astype(BF16)

    ya = jnp.dot(a_scr[...], wa_ref[...], preferred_element_type=F32)
    yb = jnp.dot(o_ref[...], wb_ref[...], preferred_element_type=F32)
    merged = ga_ref[...].astype(F32) * ya + gb_ref[...].astype(F32) * yb
    x1_ref[...] = x_ref[...] + jnp.dot(merged.astype(BF16), wo_ref[...],
                                       preferred_element_type=F32)


def _mix(z, o, x2, ln_g, ln_b, w_s, bias_full, w_a, w_b, w_o):
    n = x2.shape[0]
    row = lambda c: pl.BlockSpec((TM_MIX, D_MODEL), lambda i, c=c: (i, c))
    const = lambda shape: pl.BlockSpec(shape, lambda i: (0,) * len(shape),
                                       pipeline_mode=pl.Buffered(1))
    return pl.pallas_call(
        _mix_kernel,
        grid=(n // TM_MIX,),
        in_specs=[
            row(ZB_U), row(ZB_V), row(ZB_GA), row(ZB_GB), row(0), row(0),
            const((1, GMLP_WIDTH)), const((1, GMLP_WIDTH)),
            const((GMLP_GROUPS, GMLP_BLOCK, GMLP_BLOCK)),
            const((GMLP_BLOCK, GMLP_WIDTH)),
            const((GMLP_WIDTH, D_MODEL)), const((FOX_WIDTH, D_MODEL)),
            const((D_MODEL, D_MODEL)),
        ],
        out_specs=row(0),
        out_shape=jax.ShapeDtypeStruct((n, D_MODEL), F32),
        scratch_shapes=[
            pltpu.VMEM((TM_MIX, GMLP_WIDTH), BF16),
            pltpu.VMEM((TM_MIX, GMLP_WIDTH), BF16),
        ],
        compiler_params=pltpu.CompilerParams(
            dimension_semantics=("arbitrary",),
            vmem_limit_bytes=VMEM_LIMIT),
        name="mix",
    )(z, z, z, z, o, x2, ln_g, ln_b, w_s, bias_full, w_a, w_b, w_o)


def _ffn_kernel(tiles_per_batch, halo_ref, x1_ref, p_ref, g2_ref, wup_ref, cw_ref, cb_ref,
                wd_ref, g3_ref, wpg_ref, wple_ref, gf_ref, out_ref,
                hext_scr, ug_scr, ul_scr, act_scr, acc_scr):
    i = pl.program_id(0)

    @pl.when(pl.program_id(1) == 0)
    def _():
        halo = _rms(halo_ref[...], g2_ref[...]).astype(BF16)
        hext_scr[:HALO, :] = jnp.where(i % tiles_per_batch == 0, jnp.zeros_like(halo), halo)
        hext_scr[HALO:, :] = _rms(x1_ref[...], g2_ref[...]).astype(BF16)

    def conv(scr, first_col, lanes):
        cols = slice(first_col + lanes.start, first_col + lanes.stop)
        out = cb_ref[:, cols] + cw_ref[0:1, cols] * scr[pl.ds(HALO - 2, TM_FFN), lanes]
        for t in range(1, CONV_WIDTH):
            out = out + cw_ref[t:t + 1, cols] * scr[pl.ds(HALO - 2 + t, TM_FFN), lanes]
        return out

    def chunk(start, width):
        ug_scr[:, :width] = jnp.dot(hext_scr[...], wup_ref[:, start:start + width],
                                    preferred_element_type=F32)
        ul_scr[:, :width] = jnp.dot(hext_scr[...], wup_ref[:, D_FF + start:D_FF + start + width],
                                    preferred_element_type=F32)
        for c in range(width // LANES):
            lanes = slice(c * LANES, (c + 1) * LANES)
            gate = conv(ug_scr, start, lanes)
            lin = conv(ul_scr, D_FF + start, lanes)
            act_scr[:, lanes] = (jax.nn.gelu(gate) * lin).astype(BF16)
        down = jnp.dot(act_scr[:, :width], wd_ref[start:start + width, :],
                       preferred_element_type=F32)
        if start == 0:
            acc_scr[...] = down
        else:
            acc_scr[...] += down

    j = pl.program_id(1)
    start = 0
    for step, width in enumerate(FFN_CHUNKS):
        pl.when(j == step)(functools.partial(chunk, start, width))
        start += width

    @pl.when(j == len(FFN_CHUNKS) - 1)
    def _():
        x2 = x1_ref[...] + acc_scr[...]
        h3 = _rms(x2, g3_ref[...]).astype(BF16)
        gate = jax.nn.sigmoid(jnp.dot(h3, wpg_ref[...], preferred_element_type=F32))
        ple = jnp.dot(p_ref[...].astype(BF16), wple_ref[...], preferred_element_type=F32)
        x3 = x2 + ple * gate
        out_ref[...] = _rms(x3, gf_ref[...])


def _ffn(x1, p2, g2, w_up, conv_w, conv_b, w_down, g3, w_pg, w_ple, gf, seq):
    n = x1.shape[0]
    halo_blocks = TM_FFN // HALO
    widest = max(FFN_CHUNKS)
    row = lambda width: pl.BlockSpec((TM_FFN, width), lambda i, j: (i, 0))
    const = lambda shape: pl.BlockSpec(shape, lambda i, j: (0,) * len(shape),
                                       pipeline_mode=pl.Buffered(1))
    return pl.pallas_call(
        functools.partial(_ffn_kernel, seq // TM_FFN),
        grid=(n // TM_FFN, len(FFN_CHUNKS)),
        in_specs=[
            pl.BlockSpec((HALO, D_MODEL),
                         lambda i, j: (jnp.maximum(i * halo_blocks - 1, 0), 0)),
            row(D_MODEL), row(PLE_DIM), const((1, D_MODEL)),
            const((D_MODEL, 2 * D_FF)), const((CONV_WIDTH, 2 * D_FF)), const((1, 2 * D_FF)),
            const((D_FF, D_MODEL)), const((1, D_MODEL)), const((D_MODEL, D_MODEL)),
            const((PLE_DIM, D_MODEL)), const((1, D_MODEL)),
        ],
        out_specs=row(D_MODEL),
        out_shape=jax.ShapeDtypeStruct((n, D_MODEL), F32),
        scratch_shapes=[
            pltpu.VMEM((HALO + TM_FFN, D_MODEL), BF16),
            pltpu.VMEM((HALO + TM_FFN, widest), F32),
            pltpu.VMEM((HALO + TM_FFN, widest), F32),
            pltpu.VMEM((TM_FFN, widest), BF16),
            pltpu.VMEM((TM_FFN, D_MODEL), F32),
        ],
        compiler_params=pltpu.CompilerParams(
            dimension_semantics=("arbitrary", "arbitrary"),
            vmem_limit_bytes=VMEM_LIMIT),
        name="ffn",
    )(x1, x1, p2, g2, w_up, conv_w, conv_b, w_down, g3, w_pg, w_ple, gf)


def _layer(x2, p2, batch, seq, norm_mix_g, w_in, b_f, gmlp_ln_g, gmlp_ln_b, gmlp_w_s, gmlp_b_s,
           w_branch_a, w_branch_b, w_out, norm_ffn_g, w_up, conv_w, conv_b, w_down,
           norm_ple_g, w_ple, w_ple_gate, out_g):
    o_f = 2 * GMLP_WIDTH + 3 * FOX_WIDTH
    w_all = w_in.astype(BF16)
    w_gates = w_all[:, o_f + FOX_HEADS:]
    w_f = jnp.pad(w_all[:, o_f:o_f + FOX_HEADS], ((0, 0), (0, LANES - FOX_HEADS)))
    b_fp = jnp.pad(b_f, (0, LANES - FOX_HEADS)).reshape(1, LANES)
    tri = jnp.asarray(np.tril(np.ones((CUM_BLK, CUM_BLK), np.float32)), BF16)

    z, cparts = _inproj(x2, norm_mix_g.reshape(1, -1), w_all, w_gates, w_f, b_fp, tri, seq)
    o, (w_a, w_b, w_o, w_u, w_d, w_pg, w_pl) = _attn(
        z, cparts, batch, seq,
        (w_branch_a, w_branch_b, w_out, w_up, w_down, w_ple_gate, w_ple))

    bias_full = jnp.repeat(gmlp_b_s.T, GMLP_BLOCK, axis=1)
    x1 = _mix(z, o, x2, gmlp_ln_g.reshape(1, -1), gmlp_ln_b.reshape(1, -1),
              gmlp_w_s.astype(BF16), bias_full, w_a, w_b, w_o)

    return _ffn(x1, p2, norm_ffn_g.reshape(1, -1), w_u, conv_w, conv_b.reshape(1, -1), w_d,
                norm_ple_g.reshape(1, -1), w_pg, w_pl, out_g.reshape(1, -1), seq)


def kernel(x, p, norm_mix_g, w_in, b_f, gmlp_ln_g, gmlp_ln_b, gmlp_w_s, gmlp_b_s, w_branch_a,
           w_branch_b, w_out, norm_ffn_g, w_up, conv_w, conv_b, w_down, norm_ple_g, w_ple,
           w_ple_gate, norm_final_g):
    batch, seq, d = x.shape
    depth = p.shape[0]
    assert d == D_MODEL and depth == 1, "the fused final norm assumes a single layer"
    assert seq % TM_IN == 0 and seq % TM_FFN == 0 and seq % TQ == 0
    x2 = x.reshape(batch * seq, d)
    out = _layer(x2, p[0].reshape(batch * seq, PLE_DIM), batch, seq,
                 norm_mix_g[0], w_in[0], b_f[0], gmlp_ln_g[0], gmlp_ln_b[0], gmlp_w_s[0],
                 gmlp_b_s[0], w_branch_a[0], w_branch_b[0], w_out[0], norm_ffn_g[0], w_up[0],
                 conv_w[0], conv_b[0], w_down[0], norm_ple_g[0], w_ple[0], w_ple_gate[0],
                 norm_final_g)
    return out.reshape(batch, seq, d)
```

```python
import functools

import numpy as np
import jax
import jax.numpy as jnp
from jax import lax
from jax.experimental import pallas as pl
from jax.experimental.pallas import tpu as pltpu

F32 = jnp.float32
BF16 = jnp.bfloat16

D_MODEL = 1024
CHUNK = 64
PLE_DIM = 256
EPS = 1e-6
GMLP_GROUPS = 8
GMLP_BLOCK = 128
GMLP_WIDTH = 1024
FOX_HEADS = 16
FOX_HEAD_DIM = 64
FOX_WIDTH = 1024
D_FF = 2816
CONV_WIDTH = 3

LANES = 128
BF16_ROWS = 16
HEAD_PAIRS = FOX_HEADS // 2
N_PARTS = 3
LOG2E = 1.4426950408889634
Q_SCALE = FOX_HEAD_DIM ** -0.5 * LOG2E
PV_ROWS = FOX_HEAD_DIM + 16
BIAS_LANES = 4 * N_PARTS
Q_BIAS_OFFSET = 16

ZB_U, ZB_V, ZB_Q, ZB_K, ZB_VA, ZB_GA, ZB_GB = range(7)
Z_COLS = 7 * D_MODEL

TM_IN = 2048
RC_IN = 256
CUM_BLK = 256
TQ = 256
ATTN_PAIRS = 2
TM_MIX = 512
TM_FFN = 512
FFN_CHUNKS = (1536, 1280)
HALO = 16

VMEM_LIMIT = 56 * 1024 * 1024


def _rms(x, g):
    return x * lax.rsqrt(jnp.mean(x * x, axis=-1, keepdims=True) + EPS) * g


def _split3(x):
    hi = x.astype(BF16)
    r1 = x - hi.astype(F32)
    mid = r1.astype(BF16)
    lo = (r1 - mid.astype(F32)).astype(BF16)
    return hi, mid, lo


def _pack_parts(x):
    hi, mid, lo = (part.astype(F32) for part in _split3(x))
    lane = lax.broadcasted_iota(jnp.int32, x.shape, 1)
    packed = jnp.where(lane < FOX_HEADS, hi,
                       jnp.where(lane < 2 * FOX_HEADS, pltpu.roll(mid, FOX_HEADS, 1),
                                 pltpu.roll(lo, 2 * FOX_HEADS, 1)))
    return packed.astype(BF16)


def _gelu(x):
    a = -2.0 * np.sqrt(2.0 / np.pi) * LOG2E
    e = jnp.exp2(x * (a + (a * 0.044715) * (x * x)))
    return x * (1.0 / (1.0 + e))


def _log_sigmoid(x):
    return jnp.minimum(x, 0.0) - jnp.log1p(jnp.exp(-jnp.abs(x)))


def _inproj_kernel(tiles_per_batch, x_ref, g_ref, w_ref, wg_ref, wf_ref, bf_ref, tri_ref,
                   z_ref, cp_ref, h_scr, carry_scr):
    i = pl.program_id(0)
    j = pl.program_id(1)

    @pl.when(j == 0)
    def _():
        hb = _rms(x_ref[...], g_ref[...]).astype(BF16)
        h_scr[...] = hb
        zf = jnp.dot(hb, wf_ref[...], preferred_element_type=F32)
        logf = _log_sigmoid(zf + bf_ref[...])

        @pl.when(i % tiles_per_batch == 0)
        def _():
            carry_scr[...] = jnp.zeros_like(carry_scr)

        carry = carry_scr[0:1, :]
        for r in range(TM_IN // CUM_BLK):
            rows = slice(r * CUM_BLK, (r + 1) * CUM_BLK)
            parts = jnp.concatenate(_split3(logf[rows]), axis=1)
            cs = jnp.dot(tri_ref[...], parts, preferred_element_type=F32)
            cum = (cs[:, :LANES] + cs[:, LANES:2 * LANES]) + cs[:, 2 * LANES:] + carry
            carry = cum[CUM_BLK - 1:CUM_BLK, :]
            cp_ref[rows, :] = _pack_parts(cum * LOG2E)
        carry_scr[...] = jnp.broadcast_to(carry, carry_scr.shape)

    def project(fn, weights=w_ref, first_block=0):
        cols = pl.ds(pl.multiple_of((j - first_block) * D_MODEL, D_MODEL), D_MODEL)
        for r in range(TM_IN // RC_IN):
            rows = slice(r * RC_IN, (r + 1) * RC_IN)
            acc = jnp.dot(h_scr[rows, :], weights[:, cols], preferred_element_type=F32)
            z_ref[rows, :] = fn(acc).astype(BF16)

    @pl.when(j <= ZB_V)
    def _():
        project(_gelu)

    @pl.when(j == ZB_Q)
    def _():
        project(lambda a: a * Q_SCALE)

    @pl.when(jnp.logical_or(j == ZB_K, j == ZB_VA))
    def _():
        project(lambda a: a)

    @pl.when(j >= ZB_GA)
    def _():
        project(jax.nn.sigmoid, wg_ref, ZB_GA)


def _inproj(x2, g, w_all, w_gates, w_f, b_f, tri, seq):
    n = x2.shape[0]
    grid = (n // TM_IN, Z_COLS // D_MODEL)
    return pl.pallas_call(
        functools.partial(_inproj_kernel, seq // TM_IN),
        grid=grid,
        in_specs=[
            pl.BlockSpec((TM_IN, D_MODEL), lambda i, j: (i, 0)),
            pl.BlockSpec((1, D_MODEL), lambda i, j: (0, 0)),
            pl.BlockSpec(w_all.shape, lambda i, j: (0, 0), pipeline_mode=pl.Buffered(1)),
            pl.BlockSpec(w_gates.shape, lambda i, j: (0, 0), pipeline_mode=pl.Buffered(1)),
            pl.BlockSpec((D_MODEL, LANES), lambda i, j: (0, 0)),
            pl.BlockSpec((1, LANES), lambda i, j: (0, 0)),
            pl.BlockSpec((CUM_BLK, CUM_BLK), lambda i, j: (0, 0)),
        ],
        out_specs=[
            pl.BlockSpec((TM_IN, D_MODEL), lambda i, j: (i, j)),
            pl.BlockSpec((TM_IN, LANES), lambda i, j: (i, 0)),
        ],
        out_shape=[
            jax.ShapeDtypeStruct((n, Z_COLS), BF16),
            jax.ShapeDtypeStruct((n, LANES), BF16),
        ],
        scratch_shapes=[
            pltpu.VMEM((TM_IN, D_MODEL), BF16),
            pltpu.VMEM((8, LANES), F32),
        ],
        compiler_params=pltpu.CompilerParams(
            dimension_semantics=("arbitrary", "arbitrary"),
            vmem_limit_bytes=VMEM_LIMIT),
        name="inproj",
    )(x2, g, w_all, w_gates, w_f, b_f, tri)


def _attn_kernel(n_qt, n_cast, q_ref, k_ref, v_ref, cp_ref, place_ref, crow_ref, eye_ref, *rest):
    cast_in, (o_ref, *cast_out) = rest[:n_cast], rest[n_cast:2 * n_cast + 1]
    kaug_scr, vt_scr, qr_scr, s_scr = rest[2 * n_cast + 1:]
    for src, dst in zip(cast_in, cast_out):
        dst[...] = src[...].astype(BF16)

    nt = (((1,), (1,)), ((), ()))
    chains = range(ATTN_PAIRS)
    lane = lax.broadcasted_iota(jnp.int32, (TQ, LANES), 1)

    for c in chains:
        lanes = slice(c * LANES, (c + 1) * LANES)
        aug = jnp.dot(cp_ref[...], place_ref[c], preferred_element_type=F32) + crow_ref[0:1, :]
        kaug_scr[c, :, :LANES] = k_ref[:, lanes]
        kaug_scr[c, :, LANES:] = aug.astype(BF16)
        bias_lane = lax.broadcasted_iota(jnp.int32, aug.shape, 1) < BIAS_LANES
        augq = [jnp.where(bias_lane, pltpu.roll(aug, LANES - Q_BIAS_OFFSET * (1 + hh), 1), 0.0)
                .astype(BF16) for hh in range(2)]
        ones = jnp.ones((PV_ROWS - FOX_HEAD_DIM, TQ), BF16)
        for t in range(n_qt):
            rows = slice(t * TQ, (t + 1) * TQ)
            vt = lax.dot_general(eye_ref[...], v_ref[rows, lanes], nt,
                                 preferred_element_type=F32).astype(BF16)
            for hh in range(2):
                vt_scr[c, t, hh, :FOX_HEAD_DIM, :] = vt[hh * FOX_HEAD_DIM:(hh + 1) * FOX_HEAD_DIM]
                vt_scr[c, t, hh, FOX_HEAD_DIM:, :] = ones
            q = q_ref[rows, lanes]
            zero = jnp.zeros_like(q)
            for hh in range(2):
                own = (lane >= FOX_HEAD_DIM) if hh else (lane < FOX_HEAD_DIM)
                qrows = slice(hh * TQ, (hh + 1) * TQ)
                qr_scr[c, t, qrows, :LANES] = jnp.where(own, q, zero)
                qr_scr[c, t, qrows, LANES:] = augq[hh][rows]

    kpos = lax.broadcasted_iota(jnp.int32, (TQ, 2 * TQ), 0)
    col = lax.broadcasted_iota(jnp.int32, (TQ, 2 * TQ), 1)
    causal = kpos <= jnp.where(col >= TQ, col - TQ, col)
    def pass1(qi):
        nk = (qi + 1) * TQ
        smax = []
        for c in chains:
            s = lax.dot_general(kaug_scr[c, :nk, :], qr_scr[c, qi], nt,
                                preferred_element_type=F32)
            diag = jnp.where(causal, s[nk - TQ:], -1e30)
            m = jnp.max(diag, axis=0, keepdims=True)
            if qi:
                s_scr[qi % 2, c, :nk - TQ, :] = s[:nk - TQ]
                m = jnp.maximum(m, jnp.max(s[:nk - TQ], axis=0, keepdims=True))
            s_scr[qi % 2, c, nk - TQ:nk, :] = diag
            smax.append(m)
        return smax

    def pass2(qi, smax):
        for c in chains:
            acc = [None, None]
            for t in range(qi + 1):
                p = jnp.exp2(s_scr[qi % 2, c, t * TQ:(t + 1) * TQ, :] - smax[c]).astype(BF16)
                for hh in range(2):
                    d = jnp.dot(vt_scr[c, t, hh], p[:, hh * TQ:(hh + 1) * TQ],
                                preferred_element_type=F32)
                    acc[hh] = d if acc[hh] is None else acc[hh] + d
            heads = [a[:FOX_HEAD_DIM] * (1.0 / a[FOX_HEAD_DIM:FOX_HEAD_DIM + 1]) for a in acc]
            ot = jnp.concatenate(heads, axis=0)
            o_ref[qi * TQ:(qi + 1) * TQ, c * LANES:(c + 1) * LANES] = ot.T.astype(BF16)

    smax = pass1(0)
    for qi in range(n_qt):
        nxt = pass1(qi + 1) if qi + 1 < n_qt else None
        pass2(qi, smax)
        smax = nxt


def _attn_consts():
    place = np.zeros((HEAD_PAIRS, LANES, LANES), np.float32)
    crow = np.zeros((8, LANES), np.float32)
    q0, q1 = Q_BIAS_OFFSET, 2 * Q_BIAS_OFFSET
    for p in range(HEAD_PAIRS):
        for a in range(N_PARTS):
            place[p, a * FOX_HEADS + 2 * p, 0 + a] = -1.0
            place[p, a * FOX_HEADS + 2 * p + 1, 6 + a] = -1.0
            place[p, a * FOX_HEADS + 2 * p, q0 + 3 + a] = 1.0
            place[p, a * FOX_HEADS + 2 * p + 1, q1 + 9 + a] = 1.0
    crow[0, 3:6] = 1.0
    crow[0, 9:12] = 1.0
    crow[0, q0 + 0:q0 + 3] = 1.0
    crow[0, q1 + 6:q1 + 9] = 1.0
    eye = np.eye(LANES, dtype=np.float32)
    return jnp.asarray(place, BF16), jnp.asarray(crow, F32), jnp.asarray(eye, BF16)


def _attn(z, cparts, batch, seq, weights):
    n = z.shape[0]
    n_qt = seq // TQ
    place, crow, eye = _attn_consts()
    n_groups = HEAD_PAIRS // ATTN_PAIRS
    cast_in_specs, cast_out_specs, cast_out_shapes = [], [], []
    for w in weights:
        rows, cols = w.shape
        n_slabs = max(s for s in range(1, batch * n_groups + 1)
                      if rows % s == 0 and (rows // s) % BF16_ROWS == 0)
        slab = rows // n_slabs
        index = lambda b, g, last=n_slabs - 1: (jnp.minimum(b * n_groups + g, last), 0)
        cast_in_specs.append(pl.BlockSpec((slab, cols), index))
        cast_out_specs.append(pl.BlockSpec((slab, cols), index))
        cast_out_shapes.append(jax.ShapeDtypeStruct(w.shape, BF16))
    width = ATTN_PAIRS * LANES
    qcol = ZB_Q * D_MODEL // width
    kcol = ZB_K * D_MODEL // width
    vcol = ZB_VA * D_MODEL // width
    outs = pl.pallas_call(
        functools.partial(_attn_kernel, n_qt, len(weights)),
        grid=(batch, n_groups),
        in_specs=[
            pl.BlockSpec((seq, width), lambda b, g: (b, qcol + g)),
            pl.BlockSpec((seq, width), lambda b, g: (b, kcol + g)),
            pl.BlockSpec((seq, width), lambda b, g: (b, vcol + g)),
            pl.BlockSpec((seq, LANES), lambda b, g: (b, 0)),
            pl.BlockSpec((ATTN_PAIRS, LANES, LANES), lambda b, g: (g, 0, 0)),
            pl.BlockSpec((8, LANES), lambda b, g: (0, 0)),
            pl.BlockSpec((LANES, LANES), lambda b, g: (0, 0)),
        ] + cast_in_specs,
        out_specs=[pl.BlockSpec((seq, width), lambda b, g: (b, g))] + cast_out_specs,
        out_shape=[jax.ShapeDtypeStruct((n, FOX_WIDTH), BF16)] + cast_out_shapes,
        scratch_shapes=[
            pltpu.VMEM((ATTN_PAIRS, seq, 2 * LANES), BF16),
            pltpu.VMEM((ATTN_PAIRS, n_qt, 2, PV_ROWS, TQ), BF16),
            pltpu.VMEM((ATTN_PAIRS, n_qt, 2 * TQ, 2 * LANES), BF16),
            pltpu.VMEM((2, ATTN_PAIRS, seq, 2 * TQ), F32),
        ],
        compiler_params=pltpu.CompilerParams(
            dimension_semantics=("arbitrary", "arbitrary"),
            vmem_limit_bytes=VMEM_LIMIT),
        name="attn",
    )(z, z, z, cparts, place, crow, eye, *weights)
    return outs[0], outs[1:]


def _mix_kernel(u_ref, v_ref, ga_ref, gb_ref, o_ref, x_ref, lng_ref, lnb_ref, ws_ref,
                bias_ref, wa_ref, wb_ref, wo_ref, x1_ref, vn_scr, a_scr, yb_scr):
    yb_scr[...] = gb_ref[...].astype(F32) * jnp.dot(o_ref[...], wb_ref[...],
                                                     preferred_element_type=F32)

    v = v_ref[...].astype(F32)
    mu = jnp.mean(v, axis=-1, keepdims=True)
    vc = v - mu
    var = jnp.mean(vc * vc, axis=-1, keepdims=True)
    vn_scr[...] = (vc * lax.rsqrt(var + EPS) * lng_ref[...] + lnb_ref[...]).astype(BF16)

    t_idx = lax.broadcasted_iota(jnp.int32, (GMLP_BLOCK, GMLP_BLOCK), 0)
    s_idx = lax.broadcasted_iota(jnp.int32, (GMLP_BLOCK, GMLP_BLOCK), 1)
    causal = (s_idx // CHUNK) <= (t_idx // CHUNK)
    for g in range(GMLP_GROUPS):
        cols = slice(g * GMLP_BLOCK, (g + 1) * GMLP_BLOCK)
        wg = jnp.where(causal, ws_ref[g], jnp.zeros_like(ws_ref[g]))
        for r in range(TM_MIX // GMLP_BLOCK):
            rows = slice(r * GMLP_BLOCK, (r + 1) * GMLP_BLOCK)
            mixed = jnp.dot(wg, vn_scr[rows, cols], preferred_element_type=F32) + bias_ref[:, cols]
            a_scr[rows, cols] = (u_ref[rows, cols].astype(F32) * mixed).astype(BF16)

    ya = jnp.dot(a_scr[...], wa_ref[...], preferred_element_type=F32)
    merged = ga_ref[...].astype(F32) * ya + yb_scr[...]
    x1_ref[...] = x_ref[...] + jnp.dot(merged.astype(BF16), wo_ref[...],
                                       preferred_element_type=F32)


def _mix(z, o, x2, ln_g, ln_b, w_s, bias_full, w_a, w_b, w_o):
    n = x2.shape[0]
    row = lambda c: pl.BlockSpec((TM_MIX, D_MODEL), lambda i, c=c: (i, c))
    const = lambda shape: pl.BlockSpec(shape, lambda i: (0,) * len(shape))
    return pl.pallas_call(
        _mix_kernel,
        grid=(n // TM_MIX,),
        in_specs=[
            row(ZB_U), row(ZB_V), row(ZB_GA), row(ZB_GB), row(0), row(0),
            const((1, GMLP_WIDTH)), const((1, GMLP_WIDTH)),
            const((GMLP_GROUPS, GMLP_BLOCK, GMLP_BLOCK)),
            const((GMLP_BLOCK, GMLP_WIDTH)),
            const((GMLP_WIDTH, D_MODEL)), const((FOX_WIDTH, D_MODEL)),
            const((D_MODEL, D_MODEL)),
        ],
        out_specs=row(0),
        out_shape=jax.ShapeDtypeStruct((n, D_MODEL), F32),
        scratch_shapes=[
            pltpu.VMEM((TM_MIX, GMLP_WIDTH), BF16),
            pltpu.VMEM((TM_MIX, GMLP_WIDTH), BF16),
            pltpu.VMEM((TM_MIX, D_MODEL), F32),
        ],
        compiler_params=pltpu.CompilerParams(
            dimension_semantics=("arbitrary",),
            vmem_limit_bytes=VMEM_LIMIT),
        name="mix",
    )(z, z, z, z, o, x2, ln_g, ln_b, w_s, bias_full, w_a, w_b, w_o)


def _ffn_kernel(tiles_per_batch, halo_ref, x1_ref, p_ref, g2_ref, wup_ref, cw_ref, cb_ref,
                wd_ref, g3_ref, wpg_ref, wple_ref, gf_ref, out_ref,
                hext_scr, ug_scr, ul_scr, act_scr, acc_scr):
    i = pl.program_id(0)

    @pl.when(pl.program_id(1) == 0)
    def _():
        halo = _rms(halo_ref[...], g2_ref[...]).astype(BF16)
        hext_scr[:HALO, :] = jnp.where(i % tiles_per_batch == 0, jnp.zeros_like(halo), halo)
        hext_scr[HALO:, :] = _rms(x1_ref[...], g2_ref[...]).astype(BF16)

    def conv(scr, first_col, lanes):
        cols = slice(first_col + lanes.start, first_col + lanes.stop)
        out = cb_ref[:, cols] + cw_ref[0:1, cols] * scr[pl.ds(HALO - 2, TM_FFN), lanes]
        for t in range(1, CONV_WIDTH):
            out = out + cw_ref[t:t + 1, cols] * scr[pl.ds(HALO - 2 + t, TM_FFN), lanes]
        return out

    def chunk(start, width):
        ug_scr[:, :width] = jnp.dot(hext_scr[...], wup_ref[:, start:start + width],
                                    preferred_element_type=F32)
        ul_scr[:, :width] = jnp.dot(hext_scr[...], wup_ref[:, D_FF + start:D_FF + start + width],
                                    preferred_element_type=F32)
        for c in range(width // LANES):
            lanes = slice(c * LANES, (c + 1) * LANES)
            gate = conv(ug_scr, start, lanes)
            lin = conv(ul_scr, D_FF + start, lanes)
            act_scr[:, lanes] = (jax.nn.gelu(gate) * lin).astype(BF16)
        down = jnp.dot(act_scr[:, :width], wd_ref[start:start + width, :],
                       preferred_element_type=F32)
        if start == 0:
            acc_scr[...] = down
        else:
            acc_scr[...] += down

    j = pl.program_id(1)
    start = 0
    for step, width in enumerate(FFN_CHUNKS):
        pl.when(j == step)(functools.partial(chunk, start, width))
        start += width

    @pl.when(j == len(FFN_CHUNKS) - 1)
    def _():
        x2 = x1_ref[...] + acc_scr[...]
        h3 = _rms(x2, g3_ref[...]).astype(BF16)
        gate = jax.nn.sigmoid(jnp.dot(h3, wpg_ref[...], preferred_element_type=F32))
        ple = jnp.dot(p_ref[...].astype(BF16), wple_ref[...], preferred_element_type=F32)
        x3 = x2 + ple * gate
        out_ref[...] = _rms(x3, gf_ref[...])


def _ffn(x1, p2, g2, w_up, conv_w, conv_b, w_down, g3, w_pg, w_ple, gf, seq):
    n = x1.shape[0]
    halo_blocks = TM_FFN // HALO
    widest = max(FFN_CHUNKS)
    row = lambda width: pl.BlockSpec((TM_FFN, width), lambda i, j: (i, 0))
    const = lambda shape: pl.BlockSpec(shape, lambda i, j: (0,) * len(shape),
                                       pipeline_mode=pl.Buffered(1))
    return pl.pallas_call(
        functools.partial(_ffn_kernel, seq // TM_FFN),
        grid=(n // TM_FFN, len(FFN_CHUNKS)),
        in_specs=[
            pl.BlockSpec((HALO, D_MODEL),
                         lambda i, j: (jnp.maximum(i * halo_blocks - 1, 0), 0)),
            row(D_MODEL), row(PLE_DIM), const((1, D_MODEL)),
            const((D_MODEL, 2 * D_FF)), const((CONV_WIDTH, 2 * D_FF)), const((1, 2 * D_FF)),
            const((D_FF, D_MODEL)), const((1, D_MODEL)), const((D_MODEL, D_MODEL)),
            const((PLE_DIM, D_MODEL)), const((1, D_MODEL)),
        ],
        out_specs=row(D_MODEL),
        out_shape=jax.ShapeDtypeStruct((n, D_MODEL), F32),
        scratch_shapes=[
            pltpu.VMEM((HALO + TM_FFN, D_MODEL), BF16),
            pltpu.VMEM((HALO + TM_FFN, widest), F32),
            pltpu.VMEM((HALO + TM_FFN, widest), F32),
            pltpu.VMEM((TM_FFN, widest), BF16),
            pltpu.VMEM((TM_FFN, D_MODEL), F32),
        ],
        compiler_params=pltpu.CompilerParams(
            dimension_semantics=("arbitrary", "arbitrary"),
            vmem_limit_bytes=VMEM_LIMIT),
        name="ffn",
    )(x1, x1, p2, g2, w_up, conv_w, conv_b, w_down, g3, w_pg, w_ple, gf)


def _layer(x2, p2, batch, seq, norm_mix_g, w_in, b_f, gmlp_ln_g, gmlp_ln_b, gmlp_w_s, gmlp_b_s,
           w_branch_a, w_branch_b, w_out, norm_ffn_g, w_up, conv_w, conv_b, w_down,
           norm_ple_g, w_ple, w_ple_gate, out_g):
    o_f = 2 * GMLP_WIDTH + 3 * FOX_WIDTH
    w_all = w_in.astype(BF16)
    w_gates = w_all[:, o_f + FOX_HEADS:]
    w_f = jnp.pad(w_all[:, o_f:o_f + FOX_HEADS], ((0, 0), (0, LANES - FOX_HEADS)))
    b_fp = jnp.pad(b_f, (0, LANES - FOX_HEADS)).reshape(1, LANES)
    tri = jnp.asarray(np.tril(np.ones((CUM_BLK, CUM_BLK), np.float32)), BF16)

    z, cparts = _inproj(x2, norm_mix_g.reshape(1, -1), w_all, w_gates, w_f, b_fp, tri, seq)
    o, (w_a, w_b, w_o, w_u, w_d, w_pg, w_pl) = _attn(
        z, cparts, batch, seq,
        (w_branch_a, w_branch_b, w_out, w_up, w_down, w_ple_gate, w_ple))

    bias_full = jnp.repeat(gmlp_b_s.T, GMLP_BLOCK, axis=1)
    x1 = _mix(z, o, x2, gmlp_ln_g.reshape(1, -1), gmlp_ln_b.reshape(1, -1),
              gmlp_w_s.astype(BF16), bias_full, w_a, w_b, w_o)

    return _ffn(x1, p2, norm_ffn_g.reshape(1, -1), w_u, conv_w, conv_b.reshape(1, -1), w_d,
                norm_ple_g.reshape(1, -1), w_pg, w_pl, out_g.reshape(1, -1), seq)


def kernel(x, p, norm_mix_g, w_in, b_f, gmlp_ln_g, gmlp_ln_b, gmlp_w_s, gmlp_b_s, w_branch_a,
           w_branch_b, w_out, norm_ffn_g, w_up, conv_w, conv_b, w_down, norm_ple_g, w_ple,
           w_ple_gate, norm_final_g):
    batch, seq, d = x.shape
    depth = p.shape[0]
    assert d == D_MODEL and depth == 1, "the fused final norm assumes a single layer"
    assert seq % TM_IN == 0 and seq % TM_FFN == 0 and seq % TQ == 0
    x2 = x.reshape(batch * seq, d)
    out = _layer(x2, p[0].reshape(batch * seq, PLE_DIM), batch, seq,
                 norm_mix_g[0], w_in[0], b_f[0], gmlp_ln_g[0], gmlp_ln_b[0], gmlp_w_s[0],
                 gmlp_b_s[0], w_branch_a[0], w_branch_b[0], w_out[0], norm_ffn_g[0], w_up[0],
                 conv_w[0], conv_b[0], w_down[0], norm_ple_g[0], w_ple[0], w_ple_gate[0],
                 norm_final_g)
    return out.reshape(batch, seq, d)
```

```python
import functools

import numpy as np
import jax
import jax.numpy as jnp
from jax import lax
from jax.experimental import pallas as pl
from jax.experimental.pallas import tpu as pltpu

F32 = jnp.float32
BF16 = jnp.bfloat16

D_MODEL = 1024
CHUNK = 64
PLE_DIM = 256
EPS = 1e-6
GMLP_GROUPS = 8
GMLP_BLOCK = 128
GMLP_WIDTH = 1024
FOX_HEADS = 16
FOX_HEAD_DIM = 64
FOX_WIDTH = 1024
D_FF = 2816
CONV_WIDTH = 3

LANES = 128
BF16_ROWS = 16
HEAD_PAIRS = FOX_HEADS // 2
N_PARTS = 3
LOG2E = 1.4426950408889634
Q_SCALE = FOX_HEAD_DIM ** -0.5 * LOG2E
PV_ROWS = FOX_HEAD_DIM + BF16_ROWS
BIAS_LANES = 4 * N_PARTS
Q_BIAS_OFFSET = 16

ZB_U, ZB_V, ZB_Q, ZB_K, ZB_VA, ZB_GA, ZB_GB = range(7)
Z_COLS = 7 * D_MODEL

TM_IN = 2048
RC_IN = 256
CUM_BLK = 256
TQ = 256
ATTN_PAIRS = 2
TM_MIX = 512
TM_FFN = 512
FFN_CHUNKS = (1536, 1280)
HALO = BF16_ROWS

VMEM_CAPACITY = 64 * 1024 * 1024
VMEM_LIMIT = VMEM_CAPACITY - 8 * 1024 * 1024


def _rms(x, g):
    return x * lax.rsqrt(jnp.mean(x * x, axis=-1, keepdims=True) + EPS) * g


def _split3(x):
    hi = x.astype(BF16)
    r1 = x - hi.astype(F32)
    mid = r1.astype(BF16)
    lo = (r1 - mid.astype(F32)).astype(BF16)
    return hi, mid, lo


def _pack_parts(x):
    hi, mid, lo = (part.astype(F32) for part in _split3(x))
    lane = lax.broadcasted_iota(jnp.int32, x.shape, 1)
    packed = jnp.where(lane < FOX_HEADS, hi,
                       jnp.where(lane < 2 * FOX_HEADS, pltpu.roll(mid, FOX_HEADS, 1),
                                 pltpu.roll(lo, 2 * FOX_HEADS, 1)))
    return packed.astype(BF16)


def _gelu(x):
    a = -2.0 * np.sqrt(2.0 / np.pi) * LOG2E
    e = jnp.exp2(x * (a + (a * 0.044715) * (x * x)))
    return x * (1.0 / (1.0 + e))


def _log_sigmoid(x):
    return jnp.minimum(x, 0.0) - jnp.log1p(jnp.exp(-jnp.abs(x)))


def _inproj_kernel(tiles_per_batch, x_ref, g_ref, w_ref, wg_ref, wf_ref, bf_ref, tri_ref,
                   z_ref, cp_ref, h_scr, carry_scr):
    i = pl.program_id(0)
    j = pl.program_id(1)

    @pl.when(j == 0)
    def _():
        hb = _rms(x_ref[...], g_ref[...]).astype(BF16)
        h_scr[...] = hb
        zf = jnp.dot(hb, wf_ref[...], preferred_element_type=F32)
        logf = _log_sigmoid(zf + bf_ref[...])

        @pl.when(i % tiles_per_batch == 0)
        def _():
            carry_scr[...] = jnp.zeros_like(carry_scr)

        carry = carry_scr[0:1, :]
        for r in range(TM_IN // CUM_BLK):
            rows = slice(r * CUM_BLK, (r + 1) * CUM_BLK)
            parts = jnp.concatenate(_split3(logf[rows]), axis=1)
            cs = jnp.dot(tri_ref[...], parts, preferred_element_type=F32)
            cum = (cs[:, :LANES] + cs[:, LANES:2 * LANES]) + cs[:, 2 * LANES:] + carry
            carry = cum[CUM_BLK - 1:CUM_BLK, :]
            cp_ref[rows, :] = _pack_parts(cum * LOG2E)
        carry_scr[...] = jnp.broadcast_to(carry, carry_scr.shape)

    def project(fn, weights=w_ref, first_block=0):
        cols = pl.ds(pl.multiple_of((j - first_block) * D_MODEL, D_MODEL), D_MODEL)
        for r in range(TM_IN // RC_IN):
            rows = slice(r * RC_IN, (r + 1) * RC_IN)
            acc = jnp.dot(h_scr[rows, :], weights[:, cols], preferred_element_type=F32)
            z_ref[rows, :] = fn(acc).astype(BF16)

    @pl.when(j <= ZB_V)
    def _():
        project(_gelu)

    @pl.when(j == ZB_Q)
    def _():
        project(lambda a: a * Q_SCALE)

    @pl.when(jnp.logical_or(j == ZB_K, j == ZB_VA))
    def _():
        project(lambda a: a)

    @pl.when(j >= ZB_GA)
    def _():
        project(jax.nn.sigmoid, wg_ref, ZB_GA)


def _inproj(x2, g, w_all, w_gates, w_f, b_f, tri, seq):
    n = x2.shape[0]
    grid = (n // TM_IN, Z_COLS // D_MODEL)
    return pl.pallas_call(
        functools.partial(_inproj_kernel, seq // TM_IN),
        grid=grid,
        in_specs=[
            pl.BlockSpec((TM_IN, D_MODEL), lambda i, j: (i, 0)),
            pl.BlockSpec((1, D_MODEL), lambda i, j: (0, 0)),
            pl.BlockSpec(w_all.shape, lambda i, j: (0, 0), pipeline_mode=pl.Buffered(1)),
            pl.BlockSpec(w_gates.shape, lambda i, j: (0, 0), pipeline_mode=pl.Buffered(1)),
            pl.BlockSpec((D_MODEL, LANES), lambda i, j: (0, 0)),
            pl.BlockSpec((1, LANES), lambda i, j: (0, 0)),
            pl.BlockSpec((CUM_BLK, CUM_BLK), lambda i, j: (0, 0)),
        ],
        out_specs=[
            pl.BlockSpec((TM_IN, D_MODEL), lambda i, j: (i, j)),
            pl.BlockSpec((TM_IN, LANES), lambda i, j: (i, 0)),
        ],
        out_shape=[
            jax.ShapeDtypeStruct((n, Z_COLS), BF16),
            jax.ShapeDtypeStruct((n, LANES), BF16),
        ],
        scratch_shapes=[
            pltpu.VMEM((TM_IN, D_MODEL), BF16),
            pltpu.VMEM((8, LANES), F32),
        ],
        compiler_params=pltpu.CompilerParams(
            dimension_semantics=("arbitrary", "arbitrary"),
            vmem_limit_bytes=VMEM_LIMIT),
        name="inproj",
    )(x2, g, w_all, w_gates, w_f, b_f, tri)


def _attn_kernel(n_qt, n_cast, q_ref, k_ref, v_ref, cp_ref, place_ref, crow_ref, eye_ref, *rest):
    cast_in, (o_ref, *cast_out) = rest[:n_cast], rest[n_cast:2 * n_cast + 1]
    kaug_scr, vt_scr, qr_scr, s_scr = rest[2 * n_cast + 1:]
    for src, dst in zip(cast_in, cast_out):
        dst[...] = src[...].astype(BF16)

    nt = (((1,), (1,)), ((), ()))
    chains = range(ATTN_PAIRS)
    lane = lax.broadcasted_iota(jnp.int32, (TQ, LANES), 1)

    for c in chains:
        lanes = slice(c * LANES, (c + 1) * LANES)
        aug = jnp.dot(cp_ref[...], place_ref[c], preferred_element_type=F32) + crow_ref[0:1, :]
        kaug_scr[c, :, :LANES] = k_ref[:, lanes]
        kaug_scr[c, :, LANES:] = aug.astype(BF16)
        bias_lane = lax.broadcasted_iota(jnp.int32, aug.shape, 1) < BIAS_LANES
        augq = [jnp.where(bias_lane, pltpu.roll(aug, LANES - Q_BIAS_OFFSET * (1 + hh), 1), 0.0)
                .astype(BF16) for hh in range(2)]
        ones = jnp.ones((PV_ROWS - FOX_HEAD_DIM, TQ), BF16)
        for t in range(n_qt):
            rows = slice(t * TQ, (t + 1) * TQ)
            vt = lax.dot_general(eye_ref[...], v_ref[rows, lanes], nt,
                                 preferred_element_type=F32).astype(BF16)
            for hh in range(2):
                vt_scr[c, t, hh, :FOX_HEAD_DIM, :] = vt[hh * FOX_HEAD_DIM:(hh + 1) * FOX_HEAD_DIM]
                vt_scr[c, t, hh, FOX_HEAD_DIM:, :] = ones
            q = q_ref[rows, lanes]
            zero = jnp.zeros_like(q)
            for hh in range(2):
                own = (lane >= FOX_HEAD_DIM) if hh else (lane < FOX_HEAD_DIM)
                qrows = slice(hh * TQ, (hh + 1) * TQ)
                qr_scr[c, t, qrows, :LANES] = jnp.where(own, q, zero)
                qr_scr[c, t, qrows, LANES:] = augq[hh][rows]

    kpos = lax.broadcasted_iota(jnp.int32, (TQ, 2 * TQ), 0)
    col = lax.broadcasted_iota(jnp.int32, (TQ, 2 * TQ), 1)
    causal = kpos <= jnp.where(col >= TQ, col - TQ, col)

    def pass1(qi):
        nk = (qi + 1) * TQ
        smax = []
        for c in chains:
            s = lax.dot_general(kaug_scr[c, :nk, :], qr_scr[c, qi], nt,
                                preferred_element_type=F32)
            diag = jnp.where(causal, s[nk - TQ:], -1e30)
            m = jnp.max(diag, axis=0, keepdims=True)
            if qi:
                s_scr[qi % 2, c, :nk - TQ, :] = s[:nk - TQ]
                m = jnp.maximum(m, jnp.max(s[:nk - TQ], axis=0, keepdims=True))
            s_scr[qi % 2, c, nk - TQ:nk, :] = diag
            smax.append(m)
        return smax

    def pass2(qi, smax):
        for c in chains:
            acc = [None, None]
            for t in range(qi + 1):
                p = jnp.exp2(s_scr[qi % 2, c, t * TQ:(t + 1) * TQ, :] - smax[c]).astype(BF16)
                for hh in range(2):
                    d = jnp.dot(vt_scr[c, t, hh], p[:, hh * TQ:(hh + 1) * TQ],
                                preferred_element_type=F32)
                    acc[hh] = d if acc[hh] is None else acc[hh] + d
            heads = [a[:FOX_HEAD_DIM] * (1.0 / a[FOX_HEAD_DIM:FOX_HEAD_DIM + 1]) for a in acc]
            ot = jnp.concatenate(heads, axis=0)
            o_ref[qi * TQ:(qi + 1) * TQ, c * LANES:(c + 1) * LANES] = ot.T.astype(BF16)

    smax = pass1(0)
    for qi in range(n_qt):
        nxt = pass1(qi + 1) if qi + 1 < n_qt else None
        pass2(qi, smax)
        smax = nxt


def _attn_consts():
    place = np.zeros((HEAD_PAIRS, LANES, LANES), np.float32)
    crow = np.zeros((8, LANES), np.float32)
    q0, q1 = Q_BIAS_OFFSET, 2 * Q_BIAS_OFFSET
    for p in range(HEAD_PAIRS):
        for a in range(N_PARTS):
            place[p, a * FOX_HEADS + 2 * p, 0 + a] = -1.0
            place[p, a * FOX_HEADS + 2 * p + 1, 6 + a] = -1.0
            place[p, a * FOX_HEADS + 2 * p, q0 + 3 + a] = 1.0
            place[p, a * FOX_HEADS + 2 * p + 1, q1 + 9 + a] = 1.0
    crow[0, 3:6] = 1.0
    crow[0, 9:12] = 1.0
    crow[0, q0 + 0:q0 + 3] = 1.0
    crow[0, q1 + 6:q1 + 9] = 1.0
    eye = np.eye(LANES, dtype=np.float32)
    return jnp.asarray(place, BF16), jnp.asarray(crow, F32), jnp.asarray(eye, BF16)


def _attn(z, cparts, batch, seq, weights):
    n = z.shape[0]
    n_qt = seq // TQ
    place, crow, eye = _attn_consts()
    n_groups = HEAD_PAIRS // ATTN_PAIRS
    cast_in_specs, cast_out_specs, cast_out_shapes = [], [], []
    for w in weights:
        rows, cols = w.shape
        n_slabs = max(s for s in range(1, batch * n_groups + 1)
                      if rows % s == 0 and (rows // s) % BF16_ROWS == 0)
        slab = rows // n_slabs
        index = lambda b, g, last=n_slabs - 1: (jnp.minimum(b * n_groups + g, last), 0)
        cast_in_specs.append(pl.BlockSpec((slab, cols), index))
        cast_out_specs.append(pl.BlockSpec((slab, cols), index))
        cast_out_shapes.append(jax.ShapeDtypeStruct(w.shape, BF16))
    width = ATTN_PAIRS * LANES
    qcol = ZB_Q * D_MODEL // width
    kcol = ZB_K * D_MODEL // width
    vcol = ZB_VA * D_MODEL // width
    outs = pl.pallas_call(
        functools.partial(_attn_kernel, n_qt, len(weights)),
        grid=(batch, n_groups),
        in_specs=[
            pl.BlockSpec((seq, width), lambda b, g: (b, qcol + g)),
            pl.BlockSpec((seq, width), lambda b, g: (b, kcol + g)),
            pl.BlockSpec((seq, width), lambda b, g: (b, vcol + g)),
            pl.BlockSpec((seq, LANES), lambda b, g: (b, 0)),
            pl.BlockSpec((ATTN_PAIRS, LANES, LANES), lambda b, g: (g, 0, 0)),
            pl.BlockSpec((8, LANES), lambda b, g: (0, 0)),
            pl.BlockSpec((LANES, LANES), lambda b, g: (0, 0)),
        ] + cast_in_specs,
        out_specs=[pl.BlockSpec((seq, width), lambda b, g: (b, g))] + cast_out_specs,
        out_shape=[jax.ShapeDtypeStruct((n, FOX_WIDTH), BF16)] + cast_out_shapes,
        scratch_shapes=[
            pltpu.VMEM((ATTN_PAIRS, seq, 2 * LANES), BF16),
            pltpu.VMEM((ATTN_PAIRS, n_qt, 2, PV_ROWS, TQ), BF16),
            pltpu.VMEM((ATTN_PAIRS, n_qt, 2 * TQ, 2 * LANES), BF16),
            pltpu.VMEM((2, ATTN_PAIRS, seq, 2 * TQ), F32),
        ],
        compiler_params=pltpu.CompilerParams(
            dimension_semantics=("arbitrary", "arbitrary"),
            vmem_limit_bytes=VMEM_LIMIT),
        name="attn",
    )(z, z, z, cparts, place, crow, eye, *weights)
    return outs[0], outs[1:]


def _mix_kernel(u_ref, v_ref, ga_ref, gb_ref, o_ref, x_ref, lng_ref, lnb_ref, ws_ref,
                bias_ref, wa_ref, wb_ref, wo_ref, x1_ref, vn_scr, a_scr, yb_scr):
    yb_scr[...] = gb_ref[...].astype(F32) * jnp.dot(o_ref[...], wb_ref[...],
                                                     preferred_element_type=F32)

    v = v_ref[...].astype(F32)
    mu = jnp.mean(v, axis=-1, keepdims=True)
    vc = v - mu
    var = jnp.mean(vc * vc, axis=-1, keepdims=True)
    vn_scr[...] = (vc * lax.rsqrt(var + EPS) * lng_ref[...] + lnb_ref[...]).astype(BF16)

    t_idx = lax.broadcasted_iota(jnp.int32, (GMLP_BLOCK, GMLP_BLOCK), 0)
    s_idx = lax.broadcasted_iota(jnp.int32, (GMLP_BLOCK, GMLP_BLOCK), 1)
    causal = (s_idx // CHUNK) <= (t_idx // CHUNK)
    for g in range(GMLP_GROUPS):
        cols = slice(g * GMLP_BLOCK, (g + 1) * GMLP_BLOCK)
        wg = jnp.where(causal, ws_ref[g], jnp.zeros_like(ws_ref[g]))
        for r in range(TM_MIX // GMLP_BLOCK):
            rows = slice(r * GMLP_BLOCK, (r + 1) * GMLP_BLOCK)
            mixed = jnp.dot(wg, vn_scr[rows, cols], preferred_element_type=F32) + bias_ref[:, cols]
            a_scr[rows, cols] = (u_ref[rows, cols].astype(F32) * mixed).astype(BF16)

    ya = jnp.dot(a_scr[...], wa_ref[...], preferred_element_type=F32)
    merged = ga_ref[...].astype(F32) * ya + yb_scr[...]
    x1_ref[...] = x_ref[...] + jnp.dot(merged.astype(BF16), wo_ref[...],
                                       preferred_element_type=F32)


def _mix(z, o, x2, ln_g, ln_b, w_s, bias_full, w_a, w_b, w_o):
    n = x2.shape[0]
    row = lambda c: pl.BlockSpec((TM_MIX, D_MODEL), lambda i, c=c: (i, c))
    const = lambda shape: pl.BlockSpec(shape, lambda i: (0,) * len(shape))
    return pl.pallas_call(
        _mix_kernel,
        grid=(n // TM_MIX,),
        in_specs=[
            row(ZB_U), row(ZB_V), row(ZB_GA), row(ZB_GB), row(0), row(0),
            const((1, GMLP_WIDTH)), const((1, GMLP_WIDTH)),
            const((GMLP_GROUPS, GMLP_BLOCK, GMLP_BLOCK)),
            const((GMLP_BLOCK, GMLP_WIDTH)),
            const((GMLP_WIDTH, D_MODEL)), const((FOX_WIDTH, D_MODEL)),
            const((D_MODEL, D_MODEL)),
        ],
        out_specs=row(0),
        out_shape=jax.ShapeDtypeStruct((n, D_MODEL), F32),
        scratch_shapes=[
            pltpu.VMEM((TM_MIX, GMLP_WIDTH), BF16),
            pltpu.VMEM((TM_MIX, GMLP_WIDTH), BF16),
            pltpu.VMEM((TM_MIX, D_MODEL), F32),
        ],
        compiler_params=pltpu.CompilerParams(
            dimension_semantics=("arbitrary",),
            vmem_limit_bytes=VMEM_LIMIT),
        name="mix",
    )(z, z, z, z, o, x2, ln_g, ln_b, w_s, bias_full, w_a, w_b, w_o)


def _ffn_kernel(tiles_per_batch, halo_ref, x1_ref, p_ref, g2_ref, wup_ref, cw_ref, cb_ref,
                wd_ref, g3_ref, wpg_ref, wple_ref, gf_ref, out_ref,
                hext_scr, ug_scr, ul_scr, act_scr, acc_scr):
    i = pl.program_id(0)

    @pl.when(pl.program_id(1) == 0)
    def _():
        halo = _rms(halo_ref[...], g2_ref[...]).astype(BF16)
        hext_scr[:HALO, :] = jnp.where(i % tiles_per_batch == 0, jnp.zeros_like(halo), halo)
        hext_scr[HALO:, :] = _rms(x1_ref[...], g2_ref[...]).astype(BF16)

    def conv(scr, first_col, lanes):
        cols = slice(first_col + lanes.start, first_col + lanes.stop)
        first = HALO - (CONV_WIDTH - 1)
        out = cb_ref[:, cols] + cw_ref[0:1, cols] * scr[pl.ds(first, TM_FFN), lanes]
        for t in range(1, CONV_WIDTH):
            out = out + cw_ref[t:t + 1, cols] * scr[pl.ds(first + t, TM_FFN), lanes]
        return out

    def chunk(start, width):
        ug_scr[:, :width] = jnp.dot(hext_scr[...], wup_ref[:, start:start + width],
                                    preferred_element_type=F32)
        ul_scr[:, :width] = jnp.dot(hext_scr[...], wup_ref[:, D_FF + start:D_FF + start + width],
                                    preferred_element_type=F32)
        for c in range(width // LANES):
            lanes = slice(c * LANES, (c + 1) * LANES)
            gate = conv(ug_scr, start, lanes)
            lin = conv(ul_scr, D_FF + start, lanes)
            act_scr[:, lanes] = (jax.nn.gelu(gate) * lin).astype(BF16)
        down = jnp.dot(act_scr[:, :width], wd_ref[start:start + width, :],
                       preferred_element_type=F32)
        if start == 0:
            acc_scr[...] = down
        else:
            acc_scr[...] += down

    j = pl.program_id(1)
    start = 0
    for step, width in enumerate(FFN_CHUNKS):
        pl.when(j == step)(functools.partial(chunk, start, width))
        start += width

    @pl.when(j == len(FFN_CHUNKS) - 1)
    def _():
        x2 = x1_ref[...] + acc_scr[...]
        h3 = _rms(x2, g3_ref[...]).astype(BF16)
        gate = jax.nn.sigmoid(jnp.dot(h3, wpg_ref[...], preferred_element_type=F32))
        ple = jnp.dot(p_ref[...].astype(BF16), wple_ref[...], preferred_element_type=F32)
        x3 = x2 + ple * gate
        out_ref[...] = _rms(x3, gf_ref[...])


def _ffn(x1, p2, g2, w_up, conv_w, conv_b, w_down, g3, w_pg, w_ple, gf, seq):
    n = x1.shape[0]
    halo_blocks = TM_FFN // HALO
    widest = max(FFN_CHUNKS)
    row = lambda width: pl.BlockSpec((TM_FFN, width), lambda i, j: (i, 0))
    const = lambda shape: pl.BlockSpec(shape, lambda i, j: (0,) * len(shape),
                                       pipeline_mode=pl.Buffered(1))
    return pl.pallas_call(
        functools.partial(_ffn_kernel, seq // TM_FFN),
        grid=(n // TM_FFN, len(FFN_CHUNKS)),
        in_specs=[
            pl.BlockSpec((HALO, D_MODEL),
                         lambda i, j: (jnp.maximum(i * halo_blocks - 1, 0), 0)),
            row(D_MODEL), row(PLE_DIM), const((1, D_MODEL)),
            const((D_MODEL, 2 * D_FF)), const((CONV_WIDTH, 2 * D_FF)), const((1, 2 * D_FF)),
            const((D_FF, D_MODEL)), const((1, D_MODEL)), const((D_MODEL, D_MODEL)),
            const((PLE_DIM, D_MODEL)), const((1, D_MODEL)),
        ],
        out_specs=row(D_MODEL),
        out_shape=jax.ShapeDtypeStruct((n, D_MODEL), F32),
        scratch_shapes=[
            pltpu.VMEM((HALO + TM_FFN, D_MODEL), BF16),
            pltpu.VMEM((HALO + TM_FFN, widest), F32),
            pltpu.VMEM((HALO + TM_FFN, widest), F32),
            pltpu.VMEM((TM_FFN, widest), BF16),
            pltpu.VMEM((TM_FFN, D_MODEL), F32),
        ],
        compiler_params=pltpu.CompilerParams(
            dimension_semantics=("arbitrary", "arbitrary"),
            vmem_limit_bytes=VMEM_LIMIT),
        name="ffn",
    )(x1, x1, p2, g2, w_up, conv_w, conv_b, w_down, g3, w_pg, w_ple, gf)


def _layer(x2, p2, batch, seq, norm_mix_g, w_in, b_f, gmlp_ln_g, gmlp_ln_b, gmlp_w_s, gmlp_b_s,
           w_branch_a, w_branch_b, w_out, norm_ffn_g, w_up, conv_w, conv_b, w_down,
           norm_ple_g, w_ple, w_ple_gate, out_g):
    o_f = 2 * GMLP_WIDTH + 3 * FOX_WIDTH
    w_all = w_in.astype(BF16)
    w_gates = w_all[:, o_f + FOX_HEADS:]
    w_f = jnp.pad(w_all[:, o_f:o_f + FOX_HEADS], ((0, 0), (0, LANES - FOX_HEADS)))
    b_fp = jnp.pad(b_f, (0, LANES - FOX_HEADS)).reshape(1, LANES)
    tri = jnp.asarray(np.tril(np.ones((CUM_BLK, CUM_BLK), np.float32)), BF16)

    z, cparts = _inproj(x2, norm_mix_g.reshape(1, -1), w_all, w_gates, w_f, b_fp, tri, seq)
    o, (w_a, w_b, w_o, w_u, w_d, w_pg, w_pl) = _attn(
        z, cparts, batch, seq,
        (w_branch_a, w_branch_b, w_out, w_up, w_down, w_ple_gate, w_ple))

    bias_full = jnp.repeat(gmlp_b_s.T, GMLP_BLOCK, axis=1)
    x1 = _mix(z, o, x2, gmlp_ln_g.reshape(1, -1), gmlp_ln_b.reshape(1, -1),
              gmlp_w_s.astype(BF16), bias_full, w_a, w_b, w_o)

    return _ffn(x1, p2, norm_ffn_g.reshape(1, -1), w_u, conv_w, conv_b.reshape(1, -1), w_d,
                norm_ple_g.reshape(1, -1), w_pg, w_pl, out_g.reshape(1, -1), seq)


def kernel(x, p, norm_mix_g, w_in, b_f, gmlp_ln_g, gmlp_ln_b, gmlp_w_s, gmlp_b_s, w_branch_a,
           w_branch_b, w_out, norm_ffn_g, w_up, conv_w, conv_b, w_down, norm_ple_g, w_ple,
           w_ple_gate, norm_final_g):
    batch, seq, d = x.shape
    depth = p.shape[0]
    assert d == D_MODEL and depth == 1, "the fused final norm assumes a single layer"
    assert seq % TM_IN == 0 and seq % TM_FFN == 0 and seq % TQ == 0
    x2 = x.reshape(batch * seq, d)
    out = _layer(x2, p[0].reshape(batch * seq, PLE_DIM), batch, seq,
                 norm_mix_g[0], w_in[0], b_f[0], gmlp_ln_g[0], gmlp_ln_b[0], gmlp_w_s[0],
                 gmlp_b_s[0], w_branch_a[0], w_branch_b[0], w_out[0], norm_ffn_g[0], w_up[0],
                 conv_w[0], conv_b[0], w_down[0], norm_ple_g[0], w_ple[0], w_ple_gate[0],
                 norm_final_g)
    return out.reshape(batch, seq, d)
```

```python
import functools

import numpy as np
import jax
import jax.numpy as jnp
from jax import lax
from jax.experimental import pallas as pl
from jax.experimental.pallas import tpu as pltpu

F32 = jnp.float32
BF16 = jnp.bfloat16

D_MODEL = 1024
CHUNK = 64
PLE_DIM = 256
EPS = 1e-6
GMLP_GROUPS = 8
GMLP_BLOCK = 128
GMLP_WIDTH = 1024
FOX_HEADS = 16
FOX_HEAD_DIM = 64
FOX_WIDTH = 1024
D_FF = 2816
CONV_WIDTH = 3

LANES = 128
BF16_ROWS = 16
HEAD_PAIRS = FOX_HEADS // 2
N_PARTS = 3
LOG2E = 1.4426950408889634
Q_SCALE = FOX_HEAD_DIM ** -0.5 * LOG2E
PV_ROWS = FOX_HEAD_DIM + BF16_ROWS
BIAS_LANES = 4 * N_PARTS
Q_BIAS_OFFSET = 16

ZB_U, ZB_V, ZB_Q, ZB_K, ZB_VA, ZB_GA, ZB_GB = range(7)
Z_COLS = 7 * D_MODEL

TM_IN = 2048
RC_IN = 256
CUM_BLK = 256
TQ = 256
ATTN_PAIRS = 2
TM_MIX = 512
TM_FFN = 512
FFN_CHUNKS = (1536, 1280)
HALO = BF16_ROWS

VMEM_CAPACITY = 64 * 1024 * 1024
VMEM_LIMIT = VMEM_CAPACITY - 8 * 1024 * 1024


def _rms(x, g):
    return x * lax.rsqrt(jnp.mean(x * x, axis=-1, keepdims=True) + EPS) * g


def _split3(x):
    hi = x.astype(BF16)
    r1 = x - hi.astype(F32)
    mid = r1.astype(BF16)
    lo = (r1 - mid.astype(F32)).astype(BF16)
    return hi, mid, lo


def _pack_parts(x):
    hi, mid, lo = (part.astype(F32) for part in _split3(x))
    lane = lax.broadcasted_iota(jnp.int32, x.shape, 1)
    packed = jnp.where(lane < FOX_HEADS, hi,
                       jnp.where(lane < 2 * FOX_HEADS, pltpu.roll(mid, FOX_HEADS, 1),
                                 pltpu.roll(lo, 2 * FOX_HEADS, 1)))
    return packed.astype(BF16)


def _gelu(x):
    a = -2.0 * np.sqrt(2.0 / np.pi) * LOG2E
    e = jnp.exp2(x * (a + (a * 0.044715) * (x * x)))
    return x * (1.0 / (1.0 + e))


def _log_sigmoid(x):
    return jnp.minimum(x, 0.0) - jnp.log1p(jnp.exp(-jnp.abs(x)))


def _inproj_kernel(tiles_per_batch, x_ref, g_ref, w_ref, wg_ref, wf_ref, bf_ref, tri_ref,
                   z_ref, cp_ref, h_scr, carry_scr):
    i = pl.program_id(0)
    j = pl.program_id(1)

    @pl.when(j == 0)
    def _():
        hb = _rms(x_ref[...], g_ref[...]).astype(BF16)
        h_scr[...] = hb
        zf = jnp.dot(hb, wf_ref[...], preferred_element_type=F32)
        logf = _log_sigmoid(zf + bf_ref[...])

        @pl.when(i % tiles_per_batch == 0)
        def _():
            carry_scr[...] = jnp.zeros_like(carry_scr)

        carry = carry_scr[0:1, :]
        for r in range(TM_IN // CUM_BLK):
            rows = slice(r * CUM_BLK, (r + 1) * CUM_BLK)
            parts = jnp.concatenate(_split3(logf[rows]), axis=1)
            cs = jnp.dot(tri_ref[...], parts, preferred_element_type=F32)
            cum = (cs[:, :LANES] + cs[:, LANES:2 * LANES]) + cs[:, 2 * LANES:] + carry
            carry = cum[CUM_BLK - 1:CUM_BLK, :]
            cp_ref[rows, :] = _pack_parts(cum * LOG2E)
        carry_scr[...] = jnp.broadcast_to(carry, carry_scr.shape)

    def project(fn, weights=w_ref, first_block=0):
        cols = pl.ds(pl.multiple_of((j - first_block) * D_MODEL, D_MODEL), D_MODEL)
        for r in range(TM_IN // RC_IN):
            rows = slice(r * RC_IN, (r + 1) * RC_IN)
            acc = jnp.dot(h_scr[rows, :], weights[:, cols], preferred_element_type=F32)
            z_ref[rows, :] = fn(acc).astype(BF16)

    @pl.when(j <= ZB_V)
    def _():
        project(_gelu)

    @pl.when(j == ZB_Q)
    def _():
        project(lambda a: a * Q_SCALE)

    @pl.when(jnp.logical_or(j == ZB_K, j == ZB_VA))
    def _():
        project(lambda a: a)

    @pl.when(j >= ZB_GA)
    def _():
        project(jax.nn.sigmoid, wg_ref, ZB_GA)


def _inproj(x2, g, w_all, w_gates, w_f, b_f, tri, seq):
    n = x2.shape[0]
    grid = (n // TM_IN, Z_COLS // D_MODEL)
    return pl.pallas_call(
        functools.partial(_inproj_kernel, seq // TM_IN),
        grid=grid,
        in_specs=[
            pl.BlockSpec((TM_IN, D_MODEL), lambda i, j: (i, 0)),
            pl.BlockSpec((1, D_MODEL), lambda i, j: (0, 0)),
            pl.BlockSpec(w_all.shape, lambda i, j: (0, 0), pipeline_mode=pl.Buffered(1)),
            pl.BlockSpec(w_gates.shape, lambda i, j: (0, 0), pipeline_mode=pl.Buffered(1)),
            pl.BlockSpec((D_MODEL, LANES), lambda i, j: (0, 0)),
            pl.BlockSpec((1, LANES), lambda i, j: (0, 0)),
            pl.BlockSpec((CUM_BLK, CUM_BLK), lambda i, j: (0, 0)),
        ],
        out_specs=[
            pl.BlockSpec((TM_IN, D_MODEL), lambda i, j: (i, j)),
            pl.BlockSpec((TM_IN, LANES), lambda i, j: (i, 0)),
        ],
        out_shape=[
            jax.ShapeDtypeStruct((n, Z_COLS), BF16),
            jax.ShapeDtypeStruct((n, LANES), BF16),
        ],
        scratch_shapes=[
            pltpu.VMEM((TM_IN, D_MODEL), BF16),
            pltpu.VMEM((8, LANES), F32),
        ],
        compiler_params=pltpu.CompilerParams(
            dimension_semantics=("arbitrary", "arbitrary"),
            vmem_limit_bytes=VMEM_LIMIT),
        name="inproj",
    )(x2, g, w_all, w_gates, w_f, b_f, tri)


def _attn_kernel(n_qt, n_cast, q_ref, k_ref, v_ref, cp_ref, place_ref, crow_ref, eye_ref, *rest):
    cast_in, (o_ref, *cast_out) = rest[:n_cast], rest[n_cast:2 * n_cast + 1]
    kaug_scr, vt_scr, qr_scr, s_scr = rest[2 * n_cast + 1:]
    for src, dst in zip(cast_in, cast_out):
        dst[...] = src[...].astype(BF16)

    nt = (((1,), (1,)), ((), ()))
    chains = range(ATTN_PAIRS)
    lane = lax.broadcasted_iota(jnp.int32, (TQ, LANES), 1)

    for c in chains:
        lanes = slice(c * LANES, (c + 1) * LANES)
        aug = jnp.dot(cp_ref[...], place_ref[c], preferred_element_type=F32) + crow_ref[0:1, :]
        kaug_scr[c, :, :LANES] = k_ref[:, lanes]
        kaug_scr[c, :, LANES:] = aug.astype(BF16)
        bias_lane = lax.broadcasted_iota(jnp.int32, aug.shape, 1) < BIAS_LANES
        augq = [jnp.where(bias_lane, pltpu.roll(aug, LANES - Q_BIAS_OFFSET * (1 + hh), 1), 0.0)
                .astype(BF16) for hh in range(2)]
        ones = jnp.ones((PV_ROWS - FOX_HEAD_DIM, TQ), BF16)
        for t in range(n_qt):
            rows = slice(t * TQ, (t + 1) * TQ)
            vt = lax.dot_general(eye_ref[...], v_ref[rows, lanes], nt,
                                 preferred_element_type=F32).astype(BF16)
            for hh in range(2):
                vt_scr[c, t, hh, :FOX_HEAD_DIM, :] = vt[hh * FOX_HEAD_DIM:(hh + 1) * FOX_HEAD_DIM]
                vt_scr[c, t, hh, FOX_HEAD_DIM:, :] = ones
            q = q_ref[rows, lanes]
            zero = jnp.zeros_like(q)
            for hh in range(2):
                own = (lane >= FOX_HEAD_DIM) if hh else (lane < FOX_HEAD_DIM)
                qrows = slice(hh * TQ, (hh + 1) * TQ)
                qr_scr[c, t, qrows, :LANES] = jnp.where(own, q, zero)
                qr_scr[c, t, qrows, LANES:] = augq[hh][rows]

    kpos = lax.broadcasted_iota(jnp.int32, (TQ, 2 * TQ), 0)
    col = lax.broadcasted_iota(jnp.int32, (TQ, 2 * TQ), 1)
    causal = kpos <= jnp.where(col >= TQ, col - TQ, col)

    def pass1(qi):
        nk = (qi + 1) * TQ
        smax = []
        for c in chains:
            s = lax.dot_general(kaug_scr[c, :nk, :], qr_scr[c, qi], nt,
                                preferred_element_type=F32)
            diag = jnp.where(causal, s[nk - TQ:], -1e30)
            m = jnp.max(diag, axis=0, keepdims=True)
            if qi:
                s_scr[qi % 2, c, :nk - TQ, :] = s[:nk - TQ]
                m = jnp.maximum(m, jnp.max(s[:nk - TQ], axis=0, keepdims=True))
            s_scr[qi % 2, c, nk - TQ:nk, :] = diag
            smax.append(m)
        return smax

    def pass2(qi, smax):
        for c in chains:
            acc = [None, None]
            for t in range(qi + 1):
                p = jnp.exp2(s_scr[qi % 2, c, t * TQ:(t + 1) * TQ, :] - smax[c]).astype(BF16)
                for hh in range(2):
                    d = jnp.dot(vt_scr[c, t, hh], p[:, hh * TQ:(hh + 1) * TQ],
                                preferred_element_type=F32)
                    acc[hh] = d if acc[hh] is None else acc[hh] + d
            heads = [a[:FOX_HEAD_DIM] * (1.0 / a[FOX_HEAD_DIM:FOX_HEAD_DIM + 1]) for a in acc]
            ot = jnp.concatenate(heads, axis=0)
            o_ref[qi * TQ:(qi + 1) * TQ, c * LANES:(c + 1) * LANES] = ot.T.astype(BF16)

    smax = pass1(0)
    for qi in range(n_qt):
        nxt = pass1(qi + 1) if qi + 1 < n_qt else None
        pass2(qi, smax)
        smax = nxt


def _attn_consts():
    place = np.zeros((HEAD_PAIRS, LANES, LANES), np.float32)
    crow = np.zeros((8, LANES), np.float32)
    q0, q1 = Q_BIAS_OFFSET, 2 * Q_BIAS_OFFSET
    for p in range(HEAD_PAIRS):
        for a in range(N_PARTS):
            place[p, a * FOX_HEADS + 2 * p, 0 + a] = -1.0
            place[p, a * FOX_HEADS + 2 * p + 1, 6 + a] = -1.0
            place[p, a * FOX_HEADS + 2 * p, q0 + 3 + a] = 1.0
            place[p, a * FOX_HEADS + 2 * p + 1, q1 + 9 + a] = 1.0
    crow[0, 3:6] = 1.0
    crow[0, 9:12] = 1.0
    crow[0, q0 + 0:q0 + 3] = 1.0
    crow[0, q1 + 6:q1 + 9] = 1.0
    eye = np.eye(LANES, dtype=np.float32)
    return jnp.asarray(place, BF16), jnp.asarray(crow, F32), jnp.asarray(eye, BF16)


def _attn(z, cparts, batch, seq, weights):
    n = z.shape[0]
    n_qt = seq // TQ
    place, crow, eye = _attn_consts()
    n_groups = HEAD_PAIRS // ATTN_PAIRS
    cast_in_specs, cast_out_specs, cast_out_shapes = [], [], []
    for w in weights:
        rows, cols = w.shape
        n_slabs = max(s for s in range(1, batch * n_groups + 1)
                      if rows % s == 0 and (rows // s) % BF16_ROWS == 0)
        slab = rows // n_slabs
        index = lambda b, g, last=n_slabs - 1: (jnp.minimum(b * n_groups + g, last), 0)
        cast_in_specs.append(pl.BlockSpec((slab, cols), index))
        cast_out_specs.append(pl.BlockSpec((slab, cols), index))
        cast_out_shapes.append(jax.ShapeDtypeStruct(w.shape, BF16))
    width = ATTN_PAIRS * LANES
    qcol = ZB_Q * D_MODEL // width
    kcol = ZB_K * D_MODEL // width
    vcol = ZB_VA * D_MODEL // width
    outs = pl.pallas_call(
        functools.partial(_attn_kernel, n_qt, len(weights)),
        grid=(batch, n_groups),
        in_specs=[
            pl.BlockSpec((seq, width), lambda b, g: (b, qcol + g)),
            pl.BlockSpec((seq, width), lambda b, g: (b, kcol + g)),
            pl.BlockSpec((seq, width), lambda b, g: (b, vcol + g)),
            pl.BlockSpec((seq, LANES), lambda b, g: (b, 0)),
            pl.BlockSpec((ATTN_PAIRS, LANES, LANES), lambda b, g: (g, 0, 0)),
            pl.BlockSpec((8, LANES), lambda b, g: (0, 0)),
            pl.BlockSpec((LANES, LANES), lambda b, g: (0, 0)),
        ] + cast_in_specs,
        out_specs=[pl.BlockSpec((seq, width), lambda b, g: (b, g))] + cast_out_specs,
        out_shape=[jax.ShapeDtypeStruct((n, FOX_WIDTH), BF16)] + cast_out_shapes,
        scratch_shapes=[
            pltpu.VMEM((ATTN_PAIRS, seq, 2 * LANES), BF16),
            pltpu.VMEM((ATTN_PAIRS, n_qt, 2, PV_ROWS, TQ), BF16),
            pltpu.VMEM((ATTN_PAIRS, n_qt, 2 * TQ, 2 * LANES), BF16),
            pltpu.VMEM((2, ATTN_PAIRS, seq, 2 * TQ), F32),
        ],
        compiler_params=pltpu.CompilerParams(
            dimension_semantics=("arbitrary", "arbitrary"),
            vmem_limit_bytes=VMEM_LIMIT),
        name="attn",
    )(z, z, z, cparts, place, crow, eye, *weights)
    return outs[0], outs[1:]


def _mix_kernel(u_ref, v_ref, ga_ref, gb_ref, o_ref, x_ref, lng_ref, lnb_ref, ws_ref,
                bias_ref, wa_ref, wb_ref, wo_ref, x1_ref, vn_scr, a_scr, yb_scr):
    yb_scr[...] = gb_ref[...].astype(F32) * jnp.dot(o_ref[...], wb_ref[...],
                                                     preferred_element_type=F32)

    v = v_ref[...].astype(F32)
    mu = jnp.mean(v, axis=-1, keepdims=True)
    vc = v - mu
    var = jnp.mean(vc * vc, axis=-1, keepdims=True)
    vn_scr[...] = (vc * lax.rsqrt(var + EPS) * lng_ref[...] + lnb_ref[...]).astype(BF16)

    t_idx = lax.broadcasted_iota(jnp.int32, (GMLP_BLOCK, GMLP_BLOCK), 0)
    s_idx = lax.broadcasted_iota(jnp.int32, (GMLP_BLOCK, GMLP_BLOCK), 1)
    causal = (s_idx // CHUNK) <= (t_idx // CHUNK)
    for g in range(GMLP_GROUPS):
        cols = slice(g * GMLP_BLOCK, (g + 1) * GMLP_BLOCK)
        wg = jnp.where(causal, ws_ref[g], jnp.zeros_like(ws_ref[g]))
        for r in range(TM_MIX // GMLP_BLOCK):
            rows = slice(r * GMLP_BLOCK, (r + 1) * GMLP_BLOCK)
            mixed = jnp.dot(wg, vn_scr[rows, cols], preferred_element_type=F32) + bias_ref[:, cols]
            a_scr[rows, cols] = (u_ref[rows, cols].astype(F32) * mixed).astype(BF16)

    ya = jnp.dot(a_scr[...], wa_ref[...], preferred_element_type=F32)
    merged = ga_ref[...].astype(F32) * ya + yb_scr[...]
    x1_ref[...] = x_ref[...] + jnp.dot(merged.astype(BF16), wo_ref[...],
                                       preferred_element_type=F32)


def _mix(z, o, x2, ln_g, ln_b, w_s, bias_full, w_a, w_b, w_o):
    n = x2.shape[0]
    row = lambda c: pl.BlockSpec((TM_MIX, D_MODEL), lambda i, c=c: (i, c))
    const = lambda shape: pl.BlockSpec(shape, lambda i: (0,) * len(shape))
    return pl.pallas_call(
        _mix_kernel,
        grid=(n // TM_MIX,),
        in_specs=[
            row(ZB_U), row(ZB_V), row(ZB_GA), row(ZB_GB), row(0), row(0),
            const((1, GMLP_WIDTH)), const((1, GMLP_WIDTH)),
            const((GMLP_GROUPS, GMLP_BLOCK, GMLP_BLOCK)),
            const((GMLP_BLOCK, GMLP_WIDTH)),
            const((GMLP_WIDTH, D_MODEL)), const((FOX_WIDTH, D_MODEL)),
            const((D_MODEL, D_MODEL)),
        ],
        out_specs=row(0),
        out_shape=jax.ShapeDtypeStruct((n, D_MODEL), F32),
        scratch_shapes=[
            pltpu.VMEM((TM_MIX, GMLP_WIDTH), BF16),
            pltpu.VMEM((TM_MIX, GMLP_WIDTH), BF16),
            pltpu.VMEM((TM_MIX, D_MODEL), F32),
        ],
        compiler_params=pltpu.CompilerParams(
            dimension_semantics=("arbitrary",),
            vmem_limit_bytes=VMEM_LIMIT),
        name="mix",
    )(z, z, z, z, o, x2, ln_g, ln_b, w_s, bias_full, w_a, w_b, w_o)


def _ffn_kernel(tiles_per_batch, halo_ref, x1_ref, p_ref, g2_ref, wup_ref, cw_ref, cb_ref,
                wd_ref, g3_ref, wpg_ref, wple_ref, gf_ref, out_ref,
                hext_scr, ug_scr, ul_scr, act_scr, acc_scr):
    i = pl.program_id(0)

    @pl.when(pl.program_id(1) == 0)
    def _():
        halo = _rms(halo_ref[...], g2_ref[...]).astype(BF16)
        hext_scr[:HALO, :] = jnp.where(i % tiles_per_batch == 0, jnp.zeros_like(halo), halo)
        hext_scr[HALO:, :] = _rms(x1_ref[...], g2_ref[...]).astype(BF16)

    def conv(scr, first_col, lanes):
        cols = slice(first_col + lanes.start, first_col + lanes.stop)
        u = scr[:, lanes]
        out = cb_ref[:, cols]
        for t in range(CONV_WIDTH):
            back = CONV_WIDTH - 1 - t
            shifted = pltpu.roll(u, back, 0) if back else u
            out = out + cw_ref[t:t + 1, cols] * shifted[HALO:]
        return out

    def chunk(start, width):
        ug_scr[:, :width] = jnp.dot(hext_scr[...], wup_ref[:, start:start + width],
                                    preferred_element_type=F32)
        ul_scr[:, :width] = jnp.dot(hext_scr[...], wup_ref[:, D_FF + start:D_FF + start + width],
                                    preferred_element_type=F32)
        for c in range(width // LANES):
            lanes = slice(c * LANES, (c + 1) * LANES)
            gate = conv(ug_scr, start, lanes)
            lin = conv(ul_scr, D_FF + start, lanes)
            act_scr[:, lanes] = (jax.nn.gelu(gate) * lin).astype(BF16)
        down = jnp.dot(act_scr[:, :width], wd_ref[start:start + width, :],
                       preferred_element_type=F32)
        if start == 0:
            acc_scr[...] = down
        else:
            acc_scr[...] += down

    j = pl.program_id(1)
    start = 0
    for step, width in enumerate(FFN_CHUNKS):
        pl.when(j == step)(functools.partial(chunk, start, width))
        start += width

    @pl.when(j == len(FFN_CHUNKS) - 1)
    def _():
        x2 = x1_ref[...] + acc_scr[...]
        h3 = _rms(x2, g3_ref[...]).astype(BF16)
        gate = jax.nn.sigmoid(jnp.dot(h3, wpg_ref[...], preferred_element_type=F32))
        ple = jnp.dot(p_ref[...].astype(BF16), wple_ref[...], preferred_element_type=F32)
        x3 = x2 + ple * gate
        out_ref[...] = _rms(x3, gf_ref[...])


def _ffn(x1, p2, g2, w_up, conv_w, conv_b, w_down, g3, w_pg, w_ple, gf, seq):
    n = x1.shape[0]
    halo_blocks = TM_FFN // HALO
    widest = max(FFN_CHUNKS)
    row = lambda width: pl.BlockSpec((TM_FFN, width), lambda i, j: (i, 0))
    const = lambda shape: pl.BlockSpec(shape, lambda i, j: (0,) * len(shape),
                                       pipeline_mode=pl.Buffered(1))
    return pl.pallas_call(
        functools.partial(_ffn_kernel, seq // TM_FFN),
        grid=(n // TM_FFN, len(FFN_CHUNKS)),
        in_specs=[
            pl.BlockSpec((HALO, D_MODEL),
                         lambda i, j: (jnp.maximum(i * halo_blocks - 1, 0), 0)),
            row(D_MODEL), row(PLE_DIM), const((1, D_MODEL)),
            const((D_MODEL, 2 * D_FF)), const((CONV_WIDTH, 2 * D_FF)), const((1, 2 * D_FF)),
            const((D_FF, D_MODEL)), const((1, D_MODEL)), const((D_MODEL, D_MODEL)),
            const((PLE_DIM, D_MODEL)), const((1, D_MODEL)),
        ],
        out_specs=row(D_MODEL),
        out_shape=jax.ShapeDtypeStruct((n, D_MODEL), F32),
        scratch_shapes=[
            pltpu.VMEM((HALO + TM_FFN, D_MODEL), BF16),
            pltpu.VMEM((HALO + TM_FFN, widest), F32),
            pltpu.VMEM((HALO + TM_FFN, widest), F32),
            pltpu.VMEM((TM_FFN, widest), BF16),
            pltpu.VMEM((TM_FFN, D_MODEL), F32),
        ],
        compiler_params=pltpu.CompilerParams(
            dimension_semantics=("arbitrary", "arbitrary"),
            vmem_limit_bytes=VMEM_LIMIT),
        name="ffn",
    )(x1, x1, p2, g2, w_up, conv_w, conv_b, w_down, g3, w_pg, w_ple, gf)


def _layer(x2, p2, batch, seq, norm_mix_g, w_in, b_f, gmlp_ln_g, gmlp_ln_b, gmlp_w_s, gmlp_b_s,
           w_branch_a, w_branch_b, w_out, norm_ffn_g, w_up, conv_w, conv_b, w_down,
           norm_ple_g, w_ple, w_ple_gate, out_g):
    o_f = 2 * GMLP_WIDTH + 3 * FOX_WIDTH
    w_all = w_in.astype(BF16)
    w_gates = w_all[:, o_f + FOX_HEADS:]
    w_f = jnp.pad(w_all[:, o_f:o_f + FOX_HEADS], ((0, 0), (0, LANES - FOX_HEADS)))
    b_fp = jnp.pad(b_f, (0, LANES - FOX_HEADS)).reshape(1, LANES)
    tri = jnp.asarray(np.tril(np.ones((CUM_BLK, CUM_BLK), np.float32)), BF16)

    z, cparts = _inproj(x2, norm_mix_g.reshape(1, -1), w_all, w_gates, w_f, b_fp, tri, seq)
    o, (w_a, w_b, w_o, w_u, w_d, w_pg, w_pl) = _attn(
        z, cparts, batch, seq,
        (w_branch_a, w_branch_b, w_out, w_up, w_down, w_ple_gate, w_ple))

    bias_full = jnp.repeat(gmlp_b_s.T, GMLP_BLOCK, axis=1)
    x1 = _mix(z, o, x2, gmlp_ln_g.reshape(1, -1), gmlp_ln_b.reshape(1, -1),
              gmlp_w_s.astype(BF16), bias_full, w_a, w_b, w_o)

    return _ffn(x1, p2, norm_ffn_g.reshape(1, -1), w_u, conv_w, conv_b.reshape(1, -1), w_d,
                norm_ple_g.reshape(1, -1), w_pg, w_pl, out_g.reshape(1, -1), seq)


def kernel(x, p, norm_mix_g, w_in, b_f, gmlp_ln_g, gmlp_ln_b, gmlp_w_s, gmlp_b_s, w_branch_a,
           w_branch_b, w_out, norm_ffn_g, w_up, conv_w, conv_b, w_down, norm_ple_g, w_ple,
           w_ple_gate, norm_final_g):
    batch, seq, d = x.shape
    depth = p.shape[0]
    assert d == D_MODEL and depth == 1, "the fused final norm assumes a single layer"
    assert seq % TM_IN == 0 and seq % TM_FFN == 0 and seq % TQ == 0
    x2 = x.reshape(batch * seq, d)
    out = _layer(x2, p[0].reshape(batch * seq, PLE_DIM), batch, seq,
                 norm_mix_g[0], w_in[0], b_f[0], gmlp_ln_g[0], gmlp_ln_b[0], gmlp_w_s[0],
                 gmlp_b_s[0], w_branch_a[0], w_branch_b[0], w_out[0], norm_ffn_g[0], w_up[0],
                 conv_w[0], conv_b[0], w_down[0], norm_ple_g[0], w_ple[0], w_ple_gate[0],
                 norm_final_g)
    return out.reshape(batch, seq, d)
```

```python
import functools

import numpy as np
import jax
import jax.numpy as jnp
from jax import lax
from jax.experimental import pallas as pl
from jax.experimental.pallas import tpu as pltpu

F32 = jnp.float32
BF16 = jnp.bfloat16

D_MODEL = 1024
CHUNK = 64
PLE_DIM = 256
EPS = 1e-6
GMLP_GROUPS = 8
GMLP_BLOCK = 128
GMLP_WIDTH = 1024
FOX_HEADS = 16
FOX_HEAD_DIM = 64
FOX_WIDTH = 1024
D_FF = 2816
CONV_WIDTH = 3

LANES = 128
BF16_ROWS = 16
HEAD_PAIRS = FOX_HEADS // 2
N_PARTS = 3
LOG2E = 1.4426950408889634
Q_SCALE = FOX_HEAD_DIM ** -0.5 * LOG2E
PV_ROWS = FOX_HEAD_DIM + BF16_ROWS
BIAS_LANES = 4 * N_PARTS
Q_BIAS_OFFSET = 16

ZB_U, ZB_V, ZB_Q, ZB_K, ZB_VA, ZB_GA, ZB_GB = range(7)
Z_COLS = 7 * D_MODEL

TM_IN = 2048
RC_IN = 256
CUM_BLK = 256
TQ = 256
ATTN_PAIRS = 2
TM_MIX = 512
TM_FFN = 512
FFN_CHUNKS = (1536, 1280)
HALO = BF16_ROWS

VMEM_CAPACITY = 64 * 1024 * 1024
VMEM_LIMIT = VMEM_CAPACITY - 8 * 1024 * 1024


def _rms(x, g):
    return x * lax.rsqrt(jnp.mean(x * x, axis=-1, keepdims=True) + EPS) * g


def _split3(x):
    hi = x.astype(BF16)
    r1 = x - hi.astype(F32)
    mid = r1.astype(BF16)
    lo = (r1 - mid.astype(F32)).astype(BF16)
    return hi, mid, lo


def _pack_parts(x):
    hi, mid, lo = (part.astype(F32) for part in _split3(x))
    lane = lax.broadcasted_iota(jnp.int32, x.shape, 1)
    packed = jnp.where(lane < FOX_HEADS, hi,
                       jnp.where(lane < 2 * FOX_HEADS, pltpu.roll(mid, FOX_HEADS, 1),
                                 pltpu.roll(lo, 2 * FOX_HEADS, 1)))
    return packed.astype(BF16)


def _gelu(x):
    a = -2.0 * np.sqrt(2.0 / np.pi) * LOG2E
    e = jnp.exp2(x * (a + (a * 0.044715) * (x * x)))
    return x * (1.0 / (1.0 + e))


def _gelu_times(x, half_y):
    c = np.sqrt(2.0 / np.pi)
    t = jnp.tanh(x * (c + (c * 0.044715) * (x * x)))
    xy = x * half_y
    return xy + xy * t


def _log_sigmoid(x):
    return jnp.minimum(x, 0.0) - jnp.log1p(jnp.exp(-jnp.abs(x)))


def _inproj_kernel(tiles_per_batch, x_ref, g_ref, w_ref, wg_ref, wf_ref, bf_ref, tri_ref,
                   z_ref, cp_ref, h_scr, carry_scr):
    i = pl.program_id(0)
    j = pl.program_id(1)

    @pl.when(j == 0)
    def _():
        hb = _rms(x_ref[...], g_ref[...]).astype(BF16)
        h_scr[...] = hb
        zf = jnp.dot(hb, wf_ref[...], preferred_element_type=F32)
        logf = _log_sigmoid(zf + bf_ref[...])

        @pl.when(i % tiles_per_batch == 0)
        def _():
            carry_scr[...] = jnp.zeros_like(carry_scr)

        carry = carry_scr[0:1, :]
        for r in range(TM_IN // CUM_BLK):
            rows = slice(r * CUM_BLK, (r + 1) * CUM_BLK)
            parts = jnp.concatenate(_split3(logf[rows]), axis=1)
            cs = jnp.dot(tri_ref[...], parts, preferred_element_type=F32)
            cum = (cs[:, :LANES] + cs[:, LANES:2 * LANES]) + cs[:, 2 * LANES:] + carry
            carry = cum[CUM_BLK - 1:CUM_BLK, :]
            cp_ref[rows, :] = _pack_parts(cum * LOG2E)
        carry_scr[...] = jnp.broadcast_to(carry, carry_scr.shape)

    def project(fn, weights=w_ref, first_block=0):
        cols = pl.ds(pl.multiple_of((j - first_block) * D_MODEL, D_MODEL), D_MODEL)
        for r in range(TM_IN // RC_IN):
            rows = slice(r * RC_IN, (r + 1) * RC_IN)
            acc = jnp.dot(h_scr[rows, :], weights[:, cols], preferred_element_type=F32)
            z_ref[rows, :] = fn(acc).astype(BF16)

    @pl.when(j <= ZB_V)
    def _():
        project(_gelu)

    @pl.when(j == ZB_Q)
    def _():
        project(lambda a: a * Q_SCALE)

    @pl.when(jnp.logical_or(j == ZB_K, j == ZB_VA))
    def _():
        project(lambda a: a)

    @pl.when(j >= ZB_GA)
    def _():
        project(jax.nn.sigmoid, wg_ref, ZB_GA)


def _inproj(x2, g, w_all, w_gates, w_f, b_f, tri, seq):
    n = x2.shape[0]
    grid = (n // TM_IN, Z_COLS // D_MODEL)
    return pl.pallas_call(
        functools.partial(_inproj_kernel, seq // TM_IN),
        grid=grid,
        in_specs=[
            pl.BlockSpec((TM_IN, D_MODEL), lambda i, j: (i, 0)),
            pl.BlockSpec((1, D_MODEL), lambda i, j: (0, 0)),
            pl.BlockSpec(w_all.shape, lambda i, j: (0, 0), pipeline_mode=pl.Buffered(1)),
            pl.BlockSpec(w_gates.shape, lambda i, j: (0, 0), pipeline_mode=pl.Buffered(1)),
            pl.BlockSpec((D_MODEL, LANES), lambda i, j: (0, 0)),
            pl.BlockSpec((1, LANES), lambda i, j: (0, 0)),
            pl.BlockSpec((CUM_BLK, CUM_BLK), lambda i, j: (0, 0)),
        ],
        out_specs=[
            pl.BlockSpec((TM_IN, D_MODEL), lambda i, j: (i, j)),
            pl.BlockSpec((TM_IN, LANES), lambda i, j: (i, 0)),
        ],
        out_shape=[
            jax.ShapeDtypeStruct((n, Z_COLS), BF16),
            jax.ShapeDtypeStruct((n, LANES), BF16),
        ],
        scratch_shapes=[
            pltpu.VMEM((TM_IN, D_MODEL), BF16),
            pltpu.VMEM((8, LANES), F32),
        ],
        compiler_params=pltpu.CompilerParams(
            dimension_semantics=("arbitrary", "arbitrary"),
            vmem_limit_bytes=VMEM_LIMIT),
        name="inproj",
    )(x2, g, w_all, w_gates, w_f, b_f, tri)


def _attn_kernel(n_qt, n_cast, q_ref, k_ref, v_ref, cp_ref, place_ref, crow_ref, eye_ref, *rest):
    cast_in, (o_ref, *cast_out) = rest[:n_cast], rest[n_cast:2 * n_cast + 1]
    kaug_scr, vt_scr, qr_scr, s_scr = rest[2 * n_cast + 1:]
    for src, dst in zip(cast_in, cast_out):
        dst[...] = src[...].astype(BF16)

    nt = (((1,), (1,)), ((), ()))
    chains = range(ATTN_PAIRS)
    lane = lax.broadcasted_iota(jnp.int32, (TQ, LANES), 1)

    for c in chains:
        lanes = slice(c * LANES, (c + 1) * LANES)
        aug = jnp.dot(cp_ref[...], place_ref[c], preferred_element_type=F32) + crow_ref[0:1, :]
        kaug_scr[c, :, :LANES] = k_ref[:, lanes]
        kaug_scr[c, :, LANES:] = aug.astype(BF16)
        bias_lane = lax.broadcasted_iota(jnp.int32, aug.shape, 1) < BIAS_LANES
        augq = [jnp.where(bias_lane, pltpu.roll(aug, LANES - Q_BIAS_OFFSET * (1 + hh), 1), 0.0)
                .astype(BF16) for hh in range(2)]
        ones = jnp.ones((PV_ROWS - FOX_HEAD_DIM, TQ), BF16)
        for t in range(n_qt):
            rows = slice(t * TQ, (t + 1) * TQ)
            vt = lax.dot_general(eye_ref[...], v_ref[rows, lanes], nt,
                                 preferred_element_type=F32).astype(BF16)
            for hh in range(2):
                vt_scr[c, t, hh, :FOX_HEAD_DIM, :] = vt[hh * FOX_HEAD_DIM:(hh + 1) * FOX_HEAD_DIM]
                vt_scr[c, t, hh, FOX_HEAD_DIM:, :] = ones
            q = q_ref[rows, lanes]
            zero = jnp.zeros_like(q)
            for hh in range(2):
                own = (lane >= FOX_HEAD_DIM) if hh else (lane < FOX_HEAD_DIM)
                qrows = slice(hh * TQ, (hh + 1) * TQ)
                qr_scr[c, t, qrows, :LANES] = jnp.where(own, q, zero)
                qr_scr[c, t, qrows, LANES:] = augq[hh][rows]

    kpos = lax.broadcasted_iota(jnp.int32, (TQ, 2 * TQ), 0)
    col = lax.broadcasted_iota(jnp.int32, (TQ, 2 * TQ), 1)
    causal = kpos <= jnp.where(col >= TQ, col - TQ, col)

    def pass1(qi):
        nk = (qi + 1) * TQ
        smax = []
        for c in chains:
            s = lax.dot_general(kaug_scr[c, :nk, :], qr_scr[c, qi], nt,
                                preferred_element_type=F32)
            diag = jnp.where(causal, s[nk - TQ:], -1e30)
            m = jnp.max(diag, axis=0, keepdims=True)
            if qi:
                s_scr[qi % 2, c, :nk - TQ, :] = s[:nk - TQ]
                m = jnp.maximum(m, jnp.max(s[:nk - TQ], axis=0, keepdims=True))
            s_scr[qi % 2, c, nk - TQ:nk, :] = diag
            smax.append(m)
        return smax

    def pass2(qi, smax):
        for c in chains:
            acc = [None, None]
            for t in range(qi + 1):
                p = jnp.exp2(s_scr[qi % 2, c, t * TQ:(t + 1) * TQ, :] - smax[c]).astype(BF16)
                for hh in range(2):
                    d = jnp.dot(vt_scr[c, t, hh], p[:, hh * TQ:(hh + 1) * TQ],
                                preferred_element_type=F32)
                    acc[hh] = d if acc[hh] is None else acc[hh] + d
            heads = [a[:FOX_HEAD_DIM] * (1.0 / a[FOX_HEAD_DIM:FOX_HEAD_DIM + 1]) for a in acc]
            ot = jnp.concatenate(heads, axis=0)
            o_ref[qi * TQ:(qi + 1) * TQ, c * LANES:(c + 1) * LANES] = ot.T.astype(BF16)

    smax = pass1(0)
    for qi in range(n_qt):
        nxt = pass1(qi + 1) if qi + 1 < n_qt else None
        pass2(qi, smax)
        smax = nxt


def _attn_consts():
    place = np.zeros((HEAD_PAIRS, LANES, LANES), np.float32)
    crow = np.zeros((8, LANES), np.float32)
    q0, q1 = Q_BIAS_OFFSET, 2 * Q_BIAS_OFFSET
    for p in range(HEAD_PAIRS):
        for a in range(N_PARTS):
            place[p, a * FOX_HEADS + 2 * p, 0 + a] = -1.0
            place[p, a * FOX_HEADS + 2 * p + 1, 6 + a] = -1.0
            place[p, a * FOX_HEADS + 2 * p, q0 + 3 + a] = 1.0
            place[p, a * FOX_HEADS + 2 * p + 1, q1 + 9 + a] = 1.0
    crow[0, 3:6] = 1.0
    crow[0, 9:12] = 1.0
    crow[0, q0 + 0:q0 + 3] = 1.0
    crow[0, q1 + 6:q1 + 9] = 1.0
    eye = np.eye(LANES, dtype=np.float32)
    return jnp.asarray(place, BF16), jnp.asarray(crow, F32), jnp.asarray(eye, BF16)


def _attn(z, cparts, batch, seq, weights):
    n = z.shape[0]
    n_qt = seq // TQ
    place, crow, eye = _attn_consts()
    n_groups = HEAD_PAIRS // ATTN_PAIRS
    cast_in_specs, cast_out_specs, cast_out_shapes = [], [], []
    for w in weights:
        rows, cols = w.shape
        n_slabs = max(s for s in range(1, batch * n_groups + 1)
                      if rows % s == 0 and (rows // s) % BF16_ROWS == 0)
        slab = rows // n_slabs
        index = lambda b, g, last=n_slabs - 1: (jnp.minimum(b * n_groups + g, last), 0)
        cast_in_specs.append(pl.BlockSpec((slab, cols), index))
        cast_out_specs.append(pl.BlockSpec((slab, cols), index))
        cast_out_shapes.append(jax.ShapeDtypeStruct(w.shape, BF16))
    width = ATTN_PAIRS * LANES
    qcol = ZB_Q * D_MODEL // width
    kcol = ZB_K * D_MODEL // width
    vcol = ZB_VA * D_MODEL // width
    outs = pl.pallas_call(
        functools.partial(_attn_kernel, n_qt, len(weights)),
        grid=(batch, n_groups),
        in_specs=[
            pl.BlockSpec((seq, width), lambda b, g: (b, qcol + g)),
            pl.BlockSpec((seq, width), lambda b, g: (b, kcol + g)),
            pl.BlockSpec((seq, width), lambda b, g: (b, vcol + g)),
            pl.BlockSpec((seq, LANES), lambda b, g: (b, 0)),
            pl.BlockSpec((ATTN_PAIRS, LANES, LANES), lambda b, g: (g, 0, 0)),
            pl.BlockSpec((8, LANES), lambda b, g: (0, 0)),
            pl.BlockSpec((LANES, LANES), lambda b, g: (0, 0)),
        ] + cast_in_specs,
        out_specs=[pl.BlockSpec((seq, width), lambda b, g: (b, g))] + cast_out_specs,
        out_shape=[jax.ShapeDtypeStruct((n, FOX_WIDTH), BF16)] + cast_out_shapes,
        scratch_shapes=[
            pltpu.VMEM((ATTN_PAIRS, seq, 2 * LANES), BF16),
            pltpu.VMEM((ATTN_PAIRS, n_qt, 2, PV_ROWS, TQ), BF16),
            pltpu.VMEM((ATTN_PAIRS, n_qt, 2 * TQ, 2 * LANES), BF16),
            pltpu.VMEM((2, ATTN_PAIRS, seq, 2 * TQ), F32),
        ],
        compiler_params=pltpu.CompilerParams(
            dimension_semantics=("arbitrary", "arbitrary"),
            vmem_limit_bytes=VMEM_LIMIT),
        name="attn",
    )(z, z, z, cparts, place, crow, eye, *weights)
    return outs[0], outs[1:]


def _mix_kernel(u_ref, v_ref, ga_ref, gb_ref, o_ref, x_ref, lng_ref, lnb_ref, ws_ref,
                bias_ref, wa_ref, wb_ref, wo_ref, x1_ref, vn_scr, a_scr, yb_scr):
    yb_scr[...] = gb_ref[...].astype(F32) * jnp.dot(o_ref[...], wb_ref[...],
                                                     preferred_element_type=F32)

    v = v_ref[...].astype(F32)
    mu = jnp.mean(v, axis=-1, keepdims=True)
    vc = v - mu
    var = jnp.mean(vc * vc, axis=-1, keepdims=True)
    vn_scr[...] = (vc * lax.rsqrt(var + EPS) * lng_ref[...] + lnb_ref[...]).astype(BF16)

    t_idx = lax.broadcasted_iota(jnp.int32, (GMLP_BLOCK, GMLP_BLOCK), 0)
    s_idx = lax.broadcasted_iota(jnp.int32, (GMLP_BLOCK, GMLP_BLOCK), 1)
    causal = (s_idx // CHUNK) <= (t_idx // CHUNK)
    for g in range(GMLP_GROUPS):
        cols = slice(g * GMLP_BLOCK, (g + 1) * GMLP_BLOCK)
        wg = jnp.where(causal, ws_ref[g], jnp.zeros_like(ws_ref[g]))
        for r in range(TM_MIX // GMLP_BLOCK):
            rows = slice(r * GMLP_BLOCK, (r + 1) * GMLP_BLOCK)
            mixed = jnp.dot(wg, vn_scr[rows, cols], preferred_element_type=F32) + bias_ref[:, cols]
            a_scr[rows, cols] = (u_ref[rows, cols].astype(F32) * mixed).astype(BF16)

    ya = jnp.dot(a_scr[...], wa_ref[...], preferred_element_type=F32)
    merged = ga_ref[...].astype(F32) * ya + yb_scr[...]
    x1_ref[...] = x_ref[...] + jnp.dot(merged.astype(BF16), wo_ref[...],
                                       preferred_element_type=F32)


def _mix(z, o, x2, ln_g, ln_b, w_s, bias_full, w_a, w_b, w_o):
    n = x2.shape[0]
    row = lambda c: pl.BlockSpec((TM_MIX, D_MODEL), lambda i, c=c: (i, c))
    const = lambda shape: pl.BlockSpec(shape, lambda i: (0,) * len(shape))
    return pl.pallas_call(
        _mix_kernel,
        grid=(n // TM_MIX,),
        in_specs=[
            row(ZB_U), row(ZB_V), row(ZB_GA), row(ZB_GB), row(0), row(0),
            const((1, GMLP_WIDTH)), const((1, GMLP_WIDTH)),
            const((GMLP_GROUPS, GMLP_BLOCK, GMLP_BLOCK)),
            const((GMLP_BLOCK, GMLP_WIDTH)),
            const((GMLP_WIDTH, D_MODEL)), const((FOX_WIDTH, D_MODEL)),
            const((D_MODEL, D_MODEL)),
        ],
        out_specs=row(0),
        out_shape=jax.ShapeDtypeStruct((n, D_MODEL), F32),
        scratch_shapes=[
            pltpu.VMEM((TM_MIX, GMLP_WIDTH), BF16),
            pltpu.VMEM((TM_MIX, GMLP_WIDTH), BF16),
            pltpu.VMEM((TM_MIX, D_MODEL), F32),
        ],
        compiler_params=pltpu.CompilerParams(
            dimension_semantics=("arbitrary",),
            vmem_limit_bytes=VMEM_LIMIT),
        name="mix",
    )(z, z, z, z, o, x2, ln_g, ln_b, w_s, bias_full, w_a, w_b, w_o)


def _ffn_kernel(tiles_per_batch, halo_ref, x1_ref, p_ref, g2_ref, wup_ref, cw_ref, cb_ref,
                wd_ref, g3_ref, wpg_ref, wple_ref, gf_ref, out_ref,
                hext_scr, ug_scr, ul_scr, act_scr, acc_scr):
    i = pl.program_id(0)

    @pl.when(pl.program_id(1) == 0)
    def _():
        halo = _rms(halo_ref[...], g2_ref[...]).astype(BF16)
        hext_scr[:HALO, :] = jnp.where(i % tiles_per_batch == 0, jnp.zeros_like(halo), halo)
        hext_scr[HALO:, :] = _rms(x1_ref[...], g2_ref[...]).astype(BF16)

    def conv(scr, first_col, lanes, scale=None):
        cols = slice(first_col + lanes.start, first_col + lanes.stop)
        fold = (lambda w: w) if scale is None else (lambda w: scale * w)
        u = scr[:, lanes]
        out = fold(cb_ref[:, cols])
        for t in range(CONV_WIDTH):
            back = CONV_WIDTH - 1 - t
            shifted = pltpu.roll(u, back, 0) if back else u
            out = out + fold(cw_ref[t:t + 1, cols]) * shifted[HALO:]
        return out

    def chunk(start, width):
        ug_scr[:, :width] = jnp.dot(hext_scr[...], wup_ref[:, start:start + width],
                                    preferred_element_type=F32)
        ul_scr[:, :width] = jnp.dot(hext_scr[...], wup_ref[:, D_FF + start:D_FF + start + width],
                                    preferred_element_type=F32)
        for c in range(width // LANES):
            lanes = slice(c * LANES, (c + 1) * LANES)
            gate = conv(ug_scr, start, lanes)
            half_lin = conv(ul_scr, D_FF + start, lanes, scale=0.5)
            act_scr[:, lanes] = _gelu_times(gate, half_lin).astype(BF16)
        down = jnp.dot(act_scr[:, :width], wd_ref[start:start + width, :],
                       preferred_element_type=F32)
        if start == 0:
            acc_scr[...] = down
        else:
            acc_scr[...] += down

    j = pl.program_id(1)
    start = 0
    for step, width in enumerate(FFN_CHUNKS):
        pl.when(j == step)(functools.partial(chunk, start, width))
        start += width

    @pl.when(j == len(FFN_CHUNKS) - 1)
    def _():
        x2 = x1_ref[...] + acc_scr[...]
        h3 = _rms(x2, g3_ref[...]).astype(BF16)
        gate = jax.nn.sigmoid(jnp.dot(h3, wpg_ref[...], preferred_element_type=F32))
        ple = jnp.dot(p_ref[...].astype(BF16), wple_ref[...], preferred_element_type=F32)
        x3 = x2 + ple * gate
        out_ref[...] = _rms(x3, gf_ref[...])


def _ffn(x1, p2, g2, w_up, conv_w, conv_b, w_down, g3, w_pg, w_ple, gf, seq):
    n = x1.shape[0]
    halo_blocks = TM_FFN // HALO
    widest = max(FFN_CHUNKS)
    row = lambda width: pl.BlockSpec((TM_FFN, width), lambda i, j: (i, 0))
    const = lambda shape: pl.BlockSpec(shape, lambda i, j: (0,) * len(shape),
                                       pipeline_mode=pl.Buffered(1))
    return pl.pallas_call(
        functools.partial(_ffn_kernel, seq // TM_FFN),
        grid=(n // TM_FFN, len(FFN_CHUNKS)),
        in_specs=[
            pl.BlockSpec((HALO, D_MODEL),
                         lambda i, j: (jnp.maximum(i * halo_blocks - 1, 0), 0)),
            row(D_MODEL), row(PLE_DIM), const((1, D_MODEL)),
            const((D_MODEL, 2 * D_FF)), const((CONV_WIDTH, 2 * D_FF)), const((1, 2 * D_FF)),
            const((D_FF, D_MODEL)), const((1, D_MODEL)), const((D_MODEL, D_MODEL)),
            const((PLE_DIM, D_MODEL)), const((1, D_MODEL)),
        ],
        out_specs=row(D_MODEL),
        out_shape=jax.ShapeDtypeStruct((n, D_MODEL), F32),
        scratch_shapes=[
            pltpu.VMEM((HALO + TM_FFN, D_MODEL), BF16),
            pltpu.VMEM((HALO + TM_FFN, widest), F32),
            pltpu.VMEM((HALO + TM_FFN, widest), F32),
            pltpu.VMEM((TM_FFN, widest), BF16),
            pltpu.VMEM((TM_FFN, D_MODEL), F32),
        ],
        compiler_params=pltpu.CompilerParams(
            dimension_semantics=("arbitrary", "arbitrary"),
            vmem_limit_bytes=VMEM_LIMIT),
        name="ffn",
    )(x1, x1, p2, g2, w_up, conv_w, conv_b, w_down, g3, w_pg, w_ple, gf)


def _layer(x2, p2, batch, seq, norm_mix_g, w_in, b_f, gmlp_ln_g, gmlp_ln_b, gmlp_w_s, gmlp_b_s,
           w_branch_a, w_branch_b, w_out, norm_ffn_g, w_up, conv_w, conv_b, w_down,
           norm_ple_g, w_ple, w_ple_gate, out_g):
    o_f = 2 * GMLP_WIDTH + 3 * FOX_WIDTH
    w_all = w_in.astype(BF16)
    w_gates = w_all[:, o_f + FOX_HEADS:]
    w_f = jnp.pad(w_all[:, o_f:o_f + FOX_HEADS], ((0, 0), (0, LANES - FOX_HEADS)))
    b_fp = jnp.pad(b_f, (0, LANES - FOX_HEADS)).reshape(1, LANES)
    tri = jnp.asarray(np.tril(np.ones((CUM_BLK, CUM_BLK), np.float32)), BF16)

    z, cparts = _inproj(x2, norm_mix_g.reshape(1, -1), w_all, w_gates, w_f, b_fp, tri, seq)
    o, (w_a, w_b, w_o, w_u, w_d, w_pg, w_pl) = _attn(
        z, cparts, batch, seq,
        (w_branch_a, w_branch_b, w_out, w_up, w_down, w_ple_gate, w_ple))

    bias_full = jnp.repeat(gmlp_b_s.T, GMLP_BLOCK, axis=1)
    x1 = _mix(z, o, x2, gmlp_ln_g.reshape(1, -1), gmlp_ln_b.reshape(1, -1),
              gmlp_w_s.astype(BF16), bias_full, w_a, w_b, w_o)

    return _ffn(x1, p2, norm_ffn_g.reshape(1, -1), w_u, conv_w, conv_b.reshape(1, -1), w_d,
                norm_ple_g.reshape(1, -1), w_pg, w_pl, out_g.reshape(1, -1), seq)


def kernel(x, p, norm_mix_g, w_in, b_f, gmlp_ln_g, gmlp_ln_b, gmlp_w_s, gmlp_b_s, w_branch_a,
           w_branch_b, w_out, norm_ffn_g, w_up, conv_w, conv_b, w_down, norm_ple_g, w_ple,
           w_ple_gate, norm_final_g):
    batch, seq, d = x.shape
    depth = p.shape[0]
    assert d == D_MODEL and depth == 1, "the fused final norm assumes a single layer"
    assert seq % TM_IN == 0 and seq % TM_FFN == 0 and seq % TQ == 0
    x2 = x.reshape(batch * seq, d)
    out = _layer(x2, p[0].reshape(batch * seq, PLE_DIM), batch, seq,
                 norm_mix_g[0], w_in[0], b_f[0], gmlp_ln_g[0], gmlp_ln_b[0], gmlp_w_s[0],
                 gmlp_b_s[0], w_branch_a[0], w_branch_b[0], w_out[0], norm_ffn_g[0], w_up[0],
                 conv_w[0], conv_b[0], w_down[0], norm_ple_g[0], w_ple[0], w_ple_gate[0],
                 norm_final_g)
    return out.reshape(batch, seq, d)
```

```python
import functools

import numpy as np
import jax
import jax.numpy as jnp
from jax import lax
from jax.experimental import pallas as pl
from jax.experimental.pallas import tpu as pltpu

F32 = jnp.float32
BF16 = jnp.bfloat16

D_MODEL = 1024
CHUNK = 64
PLE_DIM = 256
EPS = 1e-6
GMLP_GROUPS = 8
GMLP_BLOCK = 128
GMLP_WIDTH = 1024
FOX_HEADS = 16
FOX_HEAD_DIM = 64
FOX_WIDTH = 1024
D_FF = 2816
CONV_WIDTH = 3

LANES = 128
SUBLANES = 8
BF16_ROWS = 16
HEAD_PAIRS = FOX_HEADS // 2
N_PARTS = 3
LOG2E = 1.4426950408889634
Q_SCALE = FOX_HEAD_DIM ** -0.5 * LOG2E
PV_ROWS = FOX_HEAD_DIM + BF16_ROWS
BIAS_LANES = 4 * N_PARTS
Q_BIAS_OFFSET = 16

ZB_U, ZB_V, ZB_Q, ZB_K, ZB_VA, ZB_GA, ZB_GB = range(7)
Z_COLS = 7 * D_MODEL

TM_IN = 2048
RC_IN = 256
CUM_BLK = 256
TQ = 256
ATTN_PAIRS = 2
TM_MIX = 512
TM_FFN = 512
FFN_CHUNKS = (1536, 1280)
HALO = BF16_ROWS

VMEM_CAPACITY = 64 * 1024 * 1024
VMEM_LIMIT = VMEM_CAPACITY - 8 * 1024 * 1024


def _rms(x, g):
    return x * lax.rsqrt(jnp.mean(x * x, axis=-1, keepdims=True) + EPS) * g


def _split3(x):
    hi = x.astype(BF16)
    r1 = x - hi.astype(F32)
    mid = r1.astype(BF16)
    lo = (r1 - mid.astype(F32)).astype(BF16)
    return hi, mid, lo


def _pack_parts(x):
    hi, mid, lo = (part.astype(F32) for part in _split3(x))
    lane = lax.broadcasted_iota(jnp.int32, x.shape, 1)
    packed = jnp.where(lane < FOX_HEADS, hi,
                       jnp.where(lane < 2 * FOX_HEADS, pltpu.roll(mid, FOX_HEADS, 1),
                                 pltpu.roll(lo, 2 * FOX_HEADS, 1)))
    return packed.astype(BF16)


def _gelu(x):
    a = -2.0 * np.sqrt(2.0 / np.pi) * LOG2E
    e = jnp.exp2(x * (a + (a * 0.044715) * (x * x)))
    return x * (1.0 / (1.0 + e))


def _gelu_times(x, half_y):
    c = np.sqrt(2.0 / np.pi)
    t = jnp.tanh(x * (c + (c * 0.044715) * (x * x)))
    xy = x * half_y
    return xy + xy * t


def _log_sigmoid(x):
    return jnp.minimum(x, 0.0) - jnp.log1p(jnp.exp(-jnp.abs(x)))


def _inproj_kernel(tiles_per_batch, x_ref, g_ref, w_ref, wg_ref, wf_ref, bf_ref, tri_ref,
                   z_ref, cp_ref, h_scr, carry_scr):
    i = pl.program_id(0)
    j = pl.program_id(1)

    @pl.when(j == 0)
    def _():
        hb = _rms(x_ref[...], g_ref[...]).astype(BF16)
        h_scr[...] = hb
        zf = jnp.dot(hb, wf_ref[...], preferred_element_type=F32)
        logf = _log_sigmoid(zf + bf_ref[...])

        @pl.when(i % tiles_per_batch == 0)
        def _():
            carry_scr[...] = jnp.zeros_like(carry_scr)

        carry = carry_scr[0:1, :]
        for r in range(TM_IN // CUM_BLK):
            rows = slice(r * CUM_BLK, (r + 1) * CUM_BLK)
            parts = jnp.concatenate(_split3(logf[rows]), axis=1)
            cs = jnp.dot(tri_ref[...], parts, preferred_element_type=F32)
            cum = (cs[:, :LANES] + cs[:, LANES:2 * LANES]) + cs[:, 2 * LANES:] + carry
            carry = cum[CUM_BLK - 1:CUM_BLK, :]
            cp_ref[rows, :] = _pack_parts(cum * LOG2E)
        carry_scr[...] = jnp.broadcast_to(carry, carry_scr.shape)

    def project(fn, weights=w_ref, first_block=0):
        cols = pl.ds(pl.multiple_of((j - first_block) * D_MODEL, D_MODEL), D_MODEL)
        for r in range(TM_IN // RC_IN):
            rows = slice(r * RC_IN, (r + 1) * RC_IN)
            acc = jnp.dot(h_scr[rows, :], weights[:, cols], preferred_element_type=F32)
            z_ref[rows, :] = fn(acc).astype(BF16)

    @pl.when(j <= ZB_V)
    def _():
        project(_gelu)

    @pl.when(j == ZB_Q)
    def _():
        project(lambda a: a * Q_SCALE)

    @pl.when(jnp.logical_or(j == ZB_K, j == ZB_VA))
    def _():
        project(lambda a: a)

    @pl.when(j >= ZB_GA)
    def _():
        project(jax.nn.sigmoid, wg_ref, ZB_GA)


def _inproj(x2, g, w_all, w_gates, w_f, b_f, tri, seq):
    n = x2.shape[0]
    grid = (n // TM_IN, Z_COLS // D_MODEL)
    return pl.pallas_call(
        functools.partial(_inproj_kernel, seq // TM_IN),
        grid=grid,
        in_specs=[
            pl.BlockSpec((TM_IN, D_MODEL), lambda i, j: (i, 0)),
            pl.BlockSpec((1, D_MODEL), lambda i, j: (0, 0)),
            pl.BlockSpec(w_all.shape, lambda i, j: (0, 0), pipeline_mode=pl.Buffered(1)),
            pl.BlockSpec(w_gates.shape, lambda i, j: (0, 0), pipeline_mode=pl.Buffered(1)),
            pl.BlockSpec((D_MODEL, LANES), lambda i, j: (0, 0)),
            pl.BlockSpec((1, LANES), lambda i, j: (0, 0)),
            pl.BlockSpec((CUM_BLK, CUM_BLK), lambda i, j: (0, 0)),
        ],
        out_specs=[
            pl.BlockSpec((TM_IN, D_MODEL), lambda i, j: (i, j)),
            pl.BlockSpec((TM_IN, LANES), lambda i, j: (i, 0)),
        ],
        out_shape=[
            jax.ShapeDtypeStruct((n, Z_COLS), BF16),
            jax.ShapeDtypeStruct((n, LANES), BF16),
        ],
        scratch_shapes=[
            pltpu.VMEM((TM_IN, D_MODEL), BF16),
            pltpu.VMEM((8, LANES), F32),
        ],
        compiler_params=pltpu.CompilerParams(
            dimension_semantics=("arbitrary", "arbitrary"),
            vmem_limit_bytes=VMEM_LIMIT),
        name="inproj",
    )(x2, g, w_all, w_gates, w_f, b_f, tri)


def _attn_kernel(n_qt, n_cast, q_ref, k_ref, v_ref, cp_ref, place_ref, crow_ref, eye_ref, *rest):
    cast_in, (o_ref, *cast_out) = rest[:n_cast], rest[n_cast:2 * n_cast + 1]
    kaug_scr, vt_scr, qr_scr, s_scr = rest[2 * n_cast + 1:]
    for src, dst in zip(cast_in, cast_out):
        dst[...] = src[...].astype(BF16)

    nt = (((1,), (1,)), ((), ()))
    chains = range(ATTN_PAIRS)
    lane = lax.broadcasted_iota(jnp.int32, (TQ, LANES), 1)

    for c in chains:
        lanes = slice(c * LANES, (c + 1) * LANES)
        aug = jnp.dot(cp_ref[...], place_ref[c], preferred_element_type=F32) + crow_ref[0:1, :]
        kaug_scr[c, :, :LANES] = k_ref[:, lanes]
        kaug_scr[c, :, LANES:] = aug.astype(BF16)
        bias_lane = lax.broadcasted_iota(jnp.int32, aug.shape, 1) < BIAS_LANES
        augq = [jnp.where(bias_lane, pltpu.roll(aug, LANES - Q_BIAS_OFFSET * (1 + hh), 1), 0.0)
                .astype(BF16) for hh in range(2)]
        ones = jnp.ones((PV_ROWS - FOX_HEAD_DIM, TQ), BF16)
        for t in range(n_qt):
            rows = slice(t * TQ, (t + 1) * TQ)
            vt = lax.dot_general(eye_ref[...], v_ref[rows, lanes], nt,
                                 preferred_element_type=F32).astype(BF16)
            for hh in range(2):
                vt_scr[c, t, hh, :FOX_HEAD_DIM, :] = vt[hh * FOX_HEAD_DIM:(hh + 1) * FOX_HEAD_DIM]
                vt_scr[c, t, hh, FOX_HEAD_DIM:, :] = ones
            q = q_ref[rows, lanes]
            zero = jnp.zeros_like(q)
            for hh in range(2):
                own = (lane >= FOX_HEAD_DIM) if hh else (lane < FOX_HEAD_DIM)
                qrows = slice(hh * TQ, (hh + 1) * TQ)
                qr_scr[c, t, qrows, :LANES] = jnp.where(own, q, zero)
                qr_scr[c, t, qrows, LANES:] = augq[hh][rows]

    kpos = lax.broadcasted_iota(jnp.int32, (TQ, 2 * TQ), 0)
    col = lax.broadcasted_iota(jnp.int32, (TQ, 2 * TQ), 1)
    causal = kpos <= jnp.where(col >= TQ, col - TQ, col)

    def pass1(qi):
        nk = (qi + 1) * TQ
        smax = []
        for c in chains:
            s = lax.dot_general(kaug_scr[c, :nk, :], qr_scr[c, qi], nt,
                                preferred_element_type=F32)
            diag = jnp.where(causal, s[nk - TQ:], -1e30)
            m = jnp.max(diag, axis=0, keepdims=True)
            if qi:
                s_scr[qi % 2, c, :nk - TQ, :] = s[:nk - TQ]
                m = jnp.maximum(m, jnp.max(s[:nk - TQ], axis=0, keepdims=True))
            s_scr[qi % 2, c, nk - TQ:nk, :] = diag
            smax.append(m)
        return smax

    def pass2(qi, smax):
        for c in chains:
            acc = [None, None]
            for t in range(qi + 1):
                p = jnp.exp2(s_scr[qi % 2, c, t * TQ:(t + 1) * TQ, :] - smax[c]).astype(BF16)
                for hh in range(2):
                    d = jnp.dot(vt_scr[c, t, hh], p[:, hh * TQ:(hh + 1) * TQ],
                                preferred_element_type=F32)
                    acc[hh] = d if acc[hh] is None else acc[hh] + d
            heads = [a[:FOX_HEAD_DIM] * (1.0 / a[FOX_HEAD_DIM:FOX_HEAD_DIM + 1]) for a in acc]
            ot = jnp.concatenate(heads, axis=0)
            o_ref[qi * TQ:(qi + 1) * TQ, c * LANES:(c + 1) * LANES] = ot.T.astype(BF16)

    smax = pass1(0)
    for qi in range(n_qt):
        nxt = pass1(qi + 1) if qi + 1 < n_qt else None
        pass2(qi, smax)
        smax = nxt


def _attn_consts():
    place = np.zeros((HEAD_PAIRS, LANES, LANES), np.float32)
    crow = np.zeros((8, LANES), np.float32)
    q0, q1 = Q_BIAS_OFFSET, 2 * Q_BIAS_OFFSET
    for p in range(HEAD_PAIRS):
        for a in range(N_PARTS):
            place[p, a * FOX_HEADS + 2 * p, 0 + a] = -1.0
            place[p, a * FOX_HEADS + 2 * p + 1, 6 + a] = -1.0
            place[p, a * FOX_HEADS + 2 * p, q0 + 3 + a] = 1.0
            place[p, a * FOX_HEADS + 2 * p + 1, q1 + 9 + a] = 1.0
    crow[0, 3:6] = 1.0
    crow[0, 9:12] = 1.0
    crow[0, q0 + 0:q0 + 3] = 1.0
    crow[0, q1 + 6:q1 + 9] = 1.0
    eye = np.eye(LANES, dtype=np.float32)
    return jnp.asarray(place, BF16), jnp.asarray(crow, F32), jnp.asarray(eye, BF16)


def _attn(z, cparts, batch, seq, weights):
    n = z.shape[0]
    n_qt = seq // TQ
    place, crow, eye = _attn_consts()
    n_groups = HEAD_PAIRS // ATTN_PAIRS
    cast_in_specs, cast_out_specs, cast_out_shapes = [], [], []
    for w in weights:
        rows, cols = w.shape
        n_slabs = max(s for s in range(1, batch * n_groups + 1)
                      if rows % s == 0 and (rows // s) % BF16_ROWS == 0)
        slab = rows // n_slabs
        index = lambda b, g, last=n_slabs - 1: (jnp.minimum(b * n_groups + g, last), 0)
        cast_in_specs.append(pl.BlockSpec((slab, cols), index))
        cast_out_specs.append(pl.BlockSpec((slab, cols), index))
        cast_out_shapes.append(jax.ShapeDtypeStruct(w.shape, BF16))
    width = ATTN_PAIRS * LANES
    qcol = ZB_Q * D_MODEL // width
    kcol = ZB_K * D_MODEL // width
    vcol = ZB_VA * D_MODEL // width
    outs = pl.pallas_call(
        functools.partial(_attn_kernel, n_qt, len(weights)),
        grid=(batch, n_groups),
        in_specs=[
            pl.BlockSpec((seq, width), lambda b, g: (b, qcol + g)),
            pl.BlockSpec((seq, width), lambda b, g: (b, kcol + g)),
            pl.BlockSpec((seq, width), lambda b, g: (b, vcol + g)),
            pl.BlockSpec((seq, LANES), lambda b, g: (b, 0)),
            pl.BlockSpec((ATTN_PAIRS, LANES, LANES), lambda b, g: (g, 0, 0)),
            pl.BlockSpec((8, LANES), lambda b, g: (0, 0)),
            pl.BlockSpec((LANES, LANES), lambda b, g: (0, 0)),
        ] + cast_in_specs,
        out_specs=[pl.BlockSpec((seq, width), lambda b, g: (b, g))] + cast_out_specs,
        out_shape=[jax.ShapeDtypeStruct((n, FOX_WIDTH), BF16)] + cast_out_shapes,
        scratch_shapes=[
            pltpu.VMEM((ATTN_PAIRS, seq, 2 * LANES), BF16),
            pltpu.VMEM((ATTN_PAIRS, n_qt, 2, PV_ROWS, TQ), BF16),
            pltpu.VMEM((ATTN_PAIRS, n_qt, 2 * TQ, 2 * LANES), BF16),
            pltpu.VMEM((2, ATTN_PAIRS, seq, 2 * TQ), F32),
        ],
        compiler_params=pltpu.CompilerParams(
            dimension_semantics=("arbitrary", "arbitrary"),
            vmem_limit_bytes=VMEM_LIMIT),
        name="attn",
    )(z, z, z, cparts, place, crow, eye, *weights)
    return outs[0], outs[1:]


def _mix_kernel(u_ref, v_ref, ga_ref, gb_ref, o_ref, x_ref, lng_ref, lnb_ref, ws_ref,
                bias_ref, wa_ref, wb_ref, wo_ref, x1_ref, vn_scr, a_scr, yb_scr):
    yb_scr[...] = gb_ref[...].astype(F32) * jnp.dot(o_ref[...], wb_ref[...],
                                                     preferred_element_type=F32)

    v = v_ref[...].astype(F32)
    mu = jnp.mean(v, axis=-1, keepdims=True)
    vc = v - mu
    var = jnp.mean(vc * vc, axis=-1, keepdims=True)
    vn_scr[...] = (vc * lax.rsqrt(var + EPS) * lng_ref[...] + lnb_ref[...]).astype(BF16)

    t_idx = lax.broadcasted_iota(jnp.int32, (GMLP_BLOCK, GMLP_BLOCK), 0)
    s_idx = lax.broadcasted_iota(jnp.int32, (GMLP_BLOCK, GMLP_BLOCK), 1)
    causal = (s_idx // CHUNK) <= (t_idx // CHUNK)
    for g in range(GMLP_GROUPS):
        cols = slice(g * GMLP_BLOCK, (g + 1) * GMLP_BLOCK)
        wg = jnp.where(causal, ws_ref[g], jnp.zeros_like(ws_ref[g]))
        for r in range(TM_MIX // GMLP_BLOCK):
            rows = slice(r * GMLP_BLOCK, (r + 1) * GMLP_BLOCK)
            mixed = jnp.dot(wg, vn_scr[rows, cols], preferred_element_type=F32) + bias_ref[:, cols]
            a_scr[rows, cols] = (u_ref[rows, cols].astype(F32) * mixed).astype(BF16)

    ya = jnp.dot(a_scr[...], wa_ref[...], preferred_element_type=F32)
    merged = ga_ref[...].astype(F32) * ya + yb_scr[...]
    x1_ref[...] = x_ref[...] + jnp.dot(merged.astype(BF16), wo_ref[...],
                                       preferred_element_type=F32)


def _mix(z, o, x2, ln_g, ln_b, w_s, bias_full, w_a, w_b, w_o):
    n = x2.shape[0]
    row = lambda c: pl.BlockSpec((TM_MIX, D_MODEL), lambda i, c=c: (i, c))
    const = lambda shape: pl.BlockSpec(shape, lambda i: (0,) * len(shape))
    return pl.pallas_call(
        _mix_kernel,
        grid=(n // TM_MIX,),
        in_specs=[
            row(ZB_U), row(ZB_V), row(ZB_GA), row(ZB_GB), row(0), row(0),
            const((1, GMLP_WIDTH)), const((1, GMLP_WIDTH)),
            const((GMLP_GROUPS, GMLP_BLOCK, GMLP_BLOCK)),
            const((GMLP_BLOCK, GMLP_WIDTH)),
            const((GMLP_WIDTH, D_MODEL)), const((FOX_WIDTH, D_MODEL)),
            const((D_MODEL, D_MODEL)),
        ],
        out_specs=row(0),
        out_shape=jax.ShapeDtypeStruct((n, D_MODEL), F32),
        scratch_shapes=[
            pltpu.VMEM((TM_MIX, GMLP_WIDTH), BF16),
            pltpu.VMEM((TM_MIX, GMLP_WIDTH), BF16),
            pltpu.VMEM((TM_MIX, D_MODEL), F32),
        ],
        compiler_params=pltpu.CompilerParams(
            dimension_semantics=("arbitrary",),
            vmem_limit_bytes=VMEM_LIMIT),
        name="mix",
    )(z, z, z, z, o, x2, ln_g, ln_b, w_s, bias_full, w_a, w_b, w_o)


def _ffn_kernel(tiles_per_batch, halo_ref, x1_ref, p_ref, g2_ref, perm_ref, wup_ref, cw_ref, cb_ref,
                wd_ref, g3_ref, wpg_ref, wple_ref, gf_ref, out_ref,
                hext_scr, ug_scr, ul_scr, act_scr, acc_scr):
    i = pl.program_id(0)
    n_vregs = TM_FFN // SUBLANES

    @pl.when(pl.program_id(1) == 0)
    def _():
        halo = _rms(halo_ref[...], g2_ref[...]).astype(BF16)
        hext_scr[:HALO, :] = jnp.where(i % tiles_per_batch == 0, jnp.zeros_like(halo), halo)
        h = _rms(x1_ref[...], g2_ref[...]).astype(BF16)
        hext_scr[HALO:, :] = jnp.dot(perm_ref[...], h, preferred_element_type=F32).astype(BF16)

    def conv(scr, first_col, lanes, scale=None):
        cols = slice(first_col + lanes.start, first_col + lanes.stop)
        fold = (lambda w: w) if scale is None else (lambda w: scale * w)
        u = scr[:, lanes]
        body = u[HALO:]
        sublane = lax.broadcasted_iota(jnp.int32, (SUBLANES, LANES), 0)
        before = []
        for back in range(1, CONV_WIDTH):
            tile = body[(n_vregs - back) * SUBLANES:(n_vregs - back + 1) * SUBLANES]
            halo_row = jnp.broadcast_to(u[HALO - back:HALO - back + 1], (SUBLANES, LANES))
            before.insert(0, jnp.where(sublane == 0, halo_row, pltpu.roll(tile, 1, 0)))
        ext = jnp.concatenate(before + [body], axis=0)
        out = fold(cb_ref[:, cols])
        for t in range(CONV_WIDTH):
            out = out + fold(cw_ref[t:t + 1, cols]) * ext[t * SUBLANES:t * SUBLANES + TM_FFN]
        return out

    def chunk(start, width):
        ug_scr[:, :width] = jnp.dot(hext_scr[...], wup_ref[:, start:start + width],
                                    preferred_element_type=F32)
        ul_scr[:, :width] = jnp.dot(hext_scr[...], wup_ref[:, D_FF + start:D_FF + start + width],
                                    preferred_element_type=F32)
        for c in range(width // LANES):
            lanes = slice(c * LANES, (c + 1) * LANES)
            gate = conv(ug_scr, start, lanes)
            half_lin = conv(ul_scr, D_FF + start, lanes, scale=0.5)
            act_scr[:, lanes] = _gelu_times(gate, half_lin).astype(BF16)
        down = jnp.dot(act_scr[:, :width], wd_ref[start:start + width, :],
                       preferred_element_type=F32)
        for c in range(D_MODEL // LANES):
            piece = down[:, c * LANES:(c + 1) * LANES]
            if start == 0:
                acc_scr[c] = piece
            else:
                acc_scr[c] += piece

    j = pl.program_id(1)
    start = 0
    for step, width in enumerate(FFN_CHUNKS):
        pl.when(j == step)(functools.partial(chunk, start, width))
        start += width

    @pl.when(j == len(FFN_CHUNKS) - 1)
    def _():
        def rows_in_order(c):
            tiles = []
            for n in range(n_vregs):
                base = (n % SUBLANES) * n_vregs + n // SUBLANES
                tiles.append(acc_scr[c, pl.ds(base, SUBLANES, stride=SUBLANES), :])
            return jnp.concatenate(tiles, axis=0)

        acc = jnp.concatenate([rows_in_order(c) for c in range(D_MODEL // LANES)], axis=1)
        x2 = x1_ref[...] + acc
        h3 = _rms(x2, g3_ref[...]).astype(BF16)
        gate = jax.nn.sigmoid(jnp.dot(h3, wpg_ref[...], preferred_element_type=F32))
        ple = jnp.dot(p_ref[...].astype(BF16), wple_ref[...], preferred_element_type=F32)
        x3 = x2 + ple * gate
        out_ref[...] = _rms(x3, gf_ref[...])


def _ffn(x1, p2, g2, w_up, conv_w, conv_b, w_down, g3, w_pg, w_ple, gf, seq):
    n = x1.shape[0]
    halo_blocks = TM_FFN // HALO
    widest = max(FFN_CHUNKS)
    row = lambda width: pl.BlockSpec((TM_FFN, width), lambda i, j: (i, 0))
    const = lambda shape: pl.BlockSpec(shape, lambda i, j: (0,) * len(shape),
                                       pipeline_mode=pl.Buffered(1))
    r = np.arange(TM_FFN)
    n_vregs = TM_FFN // SUBLANES
    perm = np.zeros((TM_FFN, TM_FFN), np.float32)
    perm[SUBLANES * (r % n_vregs) + r // n_vregs, r] = 1.0
    return pl.pallas_call(
        functools.partial(_ffn_kernel, seq // TM_FFN),
        grid=(n // TM_FFN, len(FFN_CHUNKS)),
        in_specs=[
            pl.BlockSpec((HALO, D_MODEL),
                         lambda i, j: (jnp.maximum(i * halo_blocks - 1, 0), 0)),
            row(D_MODEL), row(PLE_DIM), const((1, D_MODEL)), const((TM_FFN, TM_FFN)),
            const((D_MODEL, 2 * D_FF)), const((CONV_WIDTH, 2 * D_FF)), const((1, 2 * D_FF)),
            const((D_FF, D_MODEL)), const((1, D_MODEL)), const((D_MODEL, D_MODEL)),
            const((PLE_DIM, D_MODEL)), const((1, D_MODEL)),
        ],
        out_specs=row(D_MODEL),
        out_shape=jax.ShapeDtypeStruct((n, D_MODEL), F32),
        scratch_shapes=[
            pltpu.VMEM((HALO + TM_FFN, D_MODEL), BF16),
            pltpu.VMEM((HALO + TM_FFN, widest), F32),
            pltpu.VMEM((HALO + TM_FFN, widest), F32),
            pltpu.VMEM((TM_FFN, widest), BF16),
            pltpu.VMEM((D_MODEL // LANES, TM_FFN, LANES), F32),
        ],
        compiler_params=pltpu.CompilerParams(
            dimension_semantics=("arbitrary", "arbitrary"),
            vmem_limit_bytes=VMEM_LIMIT),
        name="ffn",
    )(x1, x1, p2, g2, jnp.asarray(perm, BF16), w_up, conv_w, conv_b, w_down, g3, w_pg, w_ple, gf)


def _layer(x2, p2, batch, seq, norm_mix_g, w_in, b_f, gmlp_ln_g, gmlp_ln_b, gmlp_w_s, gmlp_b_s,
           w_branch_a, w_branch_b, w_out, norm_ffn_g, w_up, conv_w, conv_b, w_down,
           norm_ple_g, w_ple, w_ple_gate, out_g):
    o_f = 2 * GMLP_WIDTH + 3 * FOX_WIDTH
    w_all = w_in.astype(BF16)
    w_gates = w_all[:, o_f + FOX_HEADS:]
    w_f = jnp.pad(w_all[:, o_f:o_f + FOX_HEADS], ((0, 0), (0, LANES - FOX_HEADS)))
    b_fp = jnp.pad(b_f, (0, LANES - FOX_HEADS)).reshape(1, LANES)
    tri = jnp.asarray(np.tril(np.ones((CUM_BLK, CUM_BLK), np.float32)), BF16)

    z, cparts = _inproj(x2, norm_mix_g.reshape(1, -1), w_all, w_gates, w_f, b_fp, tri, seq)
    o, (w_a, w_b, w_o, w_u, w_d, w_pg, w_pl) = _attn(
        z, cparts, batch, seq,
        (w_branch_a, w_branch_b, w_out, w_up, w_down, w_ple_gate, w_ple))

    bias_full = jnp.repeat(gmlp_b_s.T, GMLP_BLOCK, axis=1)
    x1 = _mix(z, o, x2, gmlp_ln_g.reshape(1, -1), gmlp_ln_b.reshape(1, -1),
              gmlp_w_s.astype(BF16), bias_full, w_a, w_b, w_o)

    return _ffn(x1, p2, norm_ffn_g.reshape(1, -1), w_u, conv_w, conv_b.reshape(1, -1), w_d,
                norm_ple_g.reshape(1, -1), w_pg, w_pl, out_g.reshape(1, -1), seq)


def kernel(x, p, norm_mix_g, w_in, b_f, gmlp_ln_g, gmlp_ln_b, gmlp_w_s, gmlp_b_s, w_branch_a,
           w_branch_b, w_out, norm_ffn_g, w_up, conv_w, conv_b, w_down, norm_ple_g, w_ple,
           w_ple_gate, norm_final_g):
    batch, seq, d = x.shape
    depth = p.shape[0]
    assert d == D_MODEL and depth == 1, "the fused final norm assumes a single layer"
    assert seq % TM_IN == 0 and seq % TM_FFN == 0 and seq % TQ == 0
    x2 = x.reshape(batch * seq, d)
    out = _layer(x2, p[0].reshape(batch * seq, PLE_DIM), batch, seq,
                 norm_mix_g[0], w_in[0], b_f[0], gmlp_ln_g[0], gmlp_ln_b[0], gmlp_w_s[0],
                 gmlp_b_s[0], w_branch_a[0], w_branch_b[0], w_out[0], norm_ffn_g[0], w_up[0],
                 conv_w[0], conv_b[0], w_down[0], norm_ple_g[0], w_ple[0], w_ple_gate[0],
                 norm_final_g)
    return out.reshape(batch, seq, d)
```

```python
import functools

import numpy as np
import jax
import jax.numpy as jnp
from jax import lax
from jax.experimental import pallas as pl
from jax.experimental.pallas import tpu as pltpu

F32 = jnp.float32
BF16 = jnp.bfloat16

D_MODEL = 1024
CHUNK = 64
PLE_DIM = 256
EPS = 1e-6
GMLP_GROUPS = 8
GMLP_BLOCK = 128
GMLP_WIDTH = 1024
FOX_HEADS = 16
FOX_HEAD_DIM = 64
FOX_WIDTH = 1024
D_FF = 2816
CONV_WIDTH = 3

LANES = 128
SUBLANES = 8
BF16_ROWS = 16
HEAD_PAIRS = FOX_HEADS // 2
N_PARTS = 3
LOG2E = 1.4426950408889634
Q_SCALE = FOX_HEAD_DIM ** -0.5 * LOG2E
PV_ROWS = FOX_HEAD_DIM + BF16_ROWS
BIAS_LANES = 4 * N_PARTS
Q_BIAS_OFFSET = 16

ZB_U, ZB_V, ZB_Q, ZB_K, ZB_VA, ZB_GA, ZB_GB = range(7)
Z_COLS = 7 * D_MODEL

TM_IN = 2048
RC_IN = 256
CUM_BLK = 256
TQ = 256
ATTN_PAIRS = 2
TM_MIX = 512
TM_FFN = 512
FFN_CHUNKS = (1536, 1280)
HALO = BF16_ROWS

VMEM_CAPACITY = 64 * 1024 * 1024
VMEM_LIMIT = VMEM_CAPACITY - 8 * 1024 * 1024


def _rms(x, g):
    return x * lax.rsqrt(jnp.mean(x * x, axis=-1, keepdims=True) + EPS) * g


def _split3(x):
    hi = x.astype(BF16)
    r1 = x - hi.astype(F32)
    mid = r1.astype(BF16)
    lo = (r1 - mid.astype(F32)).astype(BF16)
    return hi, mid, lo


def _pack_parts(x):
    hi, mid, lo = (part.astype(F32) for part in _split3(x))
    lane = lax.broadcasted_iota(jnp.int32, x.shape, 1)
    packed = jnp.where(lane < FOX_HEADS, hi,
                       jnp.where(lane < 2 * FOX_HEADS, pltpu.roll(mid, FOX_HEADS, 1),
                                 pltpu.roll(lo, 2 * FOX_HEADS, 1)))
    return packed.astype(BF16)


def _gelu(x):
    a = -2.0 * np.sqrt(2.0 / np.pi) * LOG2E
    e = jnp.exp2(x * (a + (a * 0.044715) * (x * x)))
    return x * (1.0 / (1.0 + e))


def _gelu_times(x, half_y):
    c = np.sqrt(2.0 / np.pi)
    t = jnp.tanh(x * (c + (c * 0.044715) * (x * x)))
    xy = x * half_y
    return xy + xy * t


def _log_sigmoid(x):
    return jnp.minimum(x, 0.0) - jnp.log1p(jnp.exp(-jnp.abs(x)))


def _inproj_kernel(tiles_per_batch, x_ref, g_ref, w_ref, wg_ref, wf_ref, bf_ref, tri_ref,
                   z_ref, cp_ref, h_scr, carry_scr):
    i = pl.program_id(0)
    j = pl.program_id(1)

    @pl.when(j == 0)
    def _():
        hb = _rms(x_ref[...], g_ref[...]).astype(BF16)
        h_scr[...] = hb
        zf = jnp.dot(hb, wf_ref[...], preferred_element_type=F32)
        logf = _log_sigmoid(zf + bf_ref[...])

        @pl.when(i % tiles_per_batch == 0)
        def _():
            carry_scr[...] = jnp.zeros_like(carry_scr)

        carry = carry_scr[0:1, :]
        for r in range(TM_IN // CUM_BLK):
            rows = slice(r * CUM_BLK, (r + 1) * CUM_BLK)
            parts = jnp.concatenate(_split3(logf[rows]), axis=1)
            cs = jnp.dot(tri_ref[...], parts, preferred_element_type=F32)
            cum = (cs[:, :LANES] + cs[:, LANES:2 * LANES]) + cs[:, 2 * LANES:] + carry
            carry = cum[CUM_BLK - 1:CUM_BLK, :]
            cp_ref[rows, :] = _pack_parts(cum * LOG2E)
        carry_scr[...] = jnp.broadcast_to(carry, carry_scr.shape)

    def project(fn, weights=w_ref, first_block=0):
        cols = pl.ds(pl.multiple_of((j - first_block) * D_MODEL, D_MODEL), D_MODEL)
        for r in range(TM_IN // RC_IN):
            rows = slice(r * RC_IN, (r + 1) * RC_IN)
            acc = jnp.dot(h_scr[rows, :], weights[:, cols], preferred_element_type=F32)
            z_ref[rows, :] = fn(acc).astype(BF16)

    @pl.when(j <= ZB_V)
    def _():
        project(_gelu)

    @pl.when(j == ZB_Q)
    def _():
        project(lambda a: a * Q_SCALE)

    @pl.when(jnp.logical_or(j == ZB_K, j == ZB_VA))
    def _():
        project(lambda a: a)

    @pl.when(j >= ZB_GA)
    def _():
        project(jax.nn.sigmoid, wg_ref, ZB_GA)


def _inproj(x2, g, w_all, w_gates, w_f, b_f, tri, seq):
    n = x2.shape[0]
    grid = (n // TM_IN, Z_COLS // D_MODEL)
    return pl.pallas_call(
        functools.partial(_inproj_kernel, seq // TM_IN),
        grid=grid,
        in_specs=[
            pl.BlockSpec((TM_IN, D_MODEL), lambda i, j: (i, 0)),
            pl.BlockSpec((1, D_MODEL), lambda i, j: (0, 0)),
            pl.BlockSpec(w_all.shape, lambda i, j: (0, 0), pipeline_mode=pl.Buffered(1)),
            pl.BlockSpec(w_gates.shape, lambda i, j: (0, 0), pipeline_mode=pl.Buffered(1)),
            pl.BlockSpec((D_MODEL, LANES), lambda i, j: (0, 0)),
            pl.BlockSpec((1, LANES), lambda i, j: (0, 0)),
            pl.BlockSpec((CUM_BLK, CUM_BLK), lambda i, j: (0, 0)),
        ],
        out_specs=[
            pl.BlockSpec((TM_IN, D_MODEL), lambda i, j: (i, j)),
            pl.BlockSpec((TM_IN, LANES), lambda i, j: (i, 0)),
        ],
        out_shape=[
            jax.ShapeDtypeStruct((n, Z_COLS), BF16),
            jax.ShapeDtypeStruct((n, LANES), BF16),
        ],
        scratch_shapes=[
            pltpu.VMEM((TM_IN, D_MODEL), BF16),
            pltpu.VMEM((8, LANES), F32),
        ],
        compiler_params=pltpu.CompilerParams(
            dimension_semantics=("arbitrary", "arbitrary"),
            vmem_limit_bytes=VMEM_LIMIT),
        name="inproj",
    )(x2, g, w_all, w_gates, w_f, b_f, tri)


def _attn_kernel(n_qt, n_cast, q_ref, k_ref, v_ref, cp_ref, place_ref, crow_ref, eye_ref, *rest):
    cast_in, (o_ref, *cast_out) = rest[:n_cast], rest[n_cast:2 * n_cast + 1]
    kaug_scr, vt_scr, qr_scr, s_scr = rest[2 * n_cast + 1:]
    for src, dst in zip(cast_in, cast_out):
        dst[...] = src[...].astype(BF16)

    nt = (((1,), (1,)), ((), ()))
    chains = range(ATTN_PAIRS)
    lane = lax.broadcasted_iota(jnp.int32, (TQ, LANES), 1)

    for c in chains:
        lanes = slice(c * LANES, (c + 1) * LANES)
        aug = jnp.dot(cp_ref[...], place_ref[c], preferred_element_type=F32) + crow_ref[0:1, :]
        kaug_scr[c, :, :LANES] = k_ref[:, lanes]
        kaug_scr[c, :, LANES:] = aug.astype(BF16)
        bias_lane = lax.broadcasted_iota(jnp.int32, aug.shape, 1) < BIAS_LANES
        augq = [jnp.where(bias_lane, pltpu.roll(aug, LANES - Q_BIAS_OFFSET * (1 + hh), 1), 0.0)
                .astype(BF16) for hh in range(2)]
        ones = jnp.ones((PV_ROWS - FOX_HEAD_DIM, TQ), BF16)
        for t in range(n_qt):
            rows = slice(t * TQ, (t + 1) * TQ)
            vt = lax.dot_general(eye_ref[...], v_ref[rows, lanes], nt,
                                 preferred_element_type=F32).astype(BF16)
            for hh in range(2):
                vt_scr[c, t, hh, :FOX_HEAD_DIM, :] = vt[hh * FOX_HEAD_DIM:(hh + 1) * FOX_HEAD_DIM]
                vt_scr[c, t, hh, FOX_HEAD_DIM:, :] = ones
            q = q_ref[rows, lanes]
            zero = jnp.zeros_like(q)
            for hh in range(2):
                own = (lane >= FOX_HEAD_DIM) if hh else (lane < FOX_HEAD_DIM)
                qrows = slice(hh * TQ, (hh + 1) * TQ)
                qr_scr[c, t, qrows, :LANES] = jnp.where(own, q, zero)
                qr_scr[c, t, qrows, LANES:] = augq[hh][rows]

    kpos = lax.broadcasted_iota(jnp.int32, (TQ, 2 * TQ), 0)
    col = lax.broadcasted_iota(jnp.int32, (TQ, 2 * TQ), 1)
    causal = kpos <= jnp.where(col >= TQ, col - TQ, col)

    def pass1(qi):
        nk = (qi + 1) * TQ
        smax = []
        for c in chains:
            s = lax.dot_general(kaug_scr[c, :nk, :], qr_scr[c, qi], nt,
                                preferred_element_type=F32)
            diag = jnp.where(causal, s[nk - TQ:], -1e30)
            m = jnp.max(diag, axis=0, keepdims=True)
            if qi:
                s_scr[qi % 2, c, :nk - TQ, :] = s[:nk - TQ]
                m = jnp.maximum(m, jnp.max(s[:nk - TQ], axis=0, keepdims=True))
            s_scr[qi % 2, c, nk - TQ:nk, :] = diag
            smax.append(m)
        return smax

    def pass2(qi, smax):
        for c in chains:
            acc = [None, None]
            for t in range(qi + 1):
                p = jnp.exp2(s_scr[qi % 2, c, t * TQ:(t + 1) * TQ, :] - smax[c]).astype(BF16)
                for hh in range(2):
                    d = jnp.dot(vt_scr[c, t, hh], p[:, hh * TQ:(hh + 1) * TQ],
                                preferred_element_type=F32)
                    acc[hh] = d if acc[hh] is None else acc[hh] + d
            heads = [a[:FOX_HEAD_DIM] * (1.0 / a[FOX_HEAD_DIM:FOX_HEAD_DIM + 1]) for a in acc]
            ot = jnp.concatenate(heads, axis=0)
            o_ref[qi * TQ:(qi + 1) * TQ, c * LANES:(c + 1) * LANES] = ot.T.astype(BF16)

    smax = pass1(0)
    for qi in range(n_qt):
        nxt = pass1(qi + 1) if qi + 1 < n_qt else None
        pass2(qi, smax)
        smax = nxt


def _attn_consts():
    place = np.zeros((HEAD_PAIRS, LANES, LANES), np.float32)
    crow = np.zeros((8, LANES), np.float32)
    q0, q1 = Q_BIAS_OFFSET, 2 * Q_BIAS_OFFSET
    for p in range(HEAD_PAIRS):
        for a in range(N_PARTS):
            place[p, a * FOX_HEADS + 2 * p, 0 + a] = -1.0
            place[p, a * FOX_HEADS + 2 * p + 1, 6 + a] = -1.0
            place[p, a * FOX_HEADS + 2 * p, q0 + 3 + a] = 1.0
            place[p, a * FOX_HEADS + 2 * p + 1, q1 + 9 + a] = 1.0
    crow[0, 3:6] = 1.0
    crow[0, 9:12] = 1.0
    crow[0, q0 + 0:q0 + 3] = 1.0
    crow[0, q1 + 6:q1 + 9] = 1.0
    eye = np.eye(LANES, dtype=np.float32)
    return jnp.asarray(place, BF16), jnp.asarray(crow, F32), jnp.asarray(eye, BF16)


def _attn(z, cparts, batch, seq, weights):
    n = z.shape[0]
    n_qt = seq // TQ
    place, crow, eye = _attn_consts()
    n_groups = HEAD_PAIRS // ATTN_PAIRS
    cast_in_specs, cast_out_specs, cast_out_shapes = [], [], []
    for w in weights:
        rows, cols = w.shape
        n_slabs = max(s for s in range(1, batch * n_groups + 1)
                      if rows % s == 0 and (rows // s) % BF16_ROWS == 0)
        slab = rows // n_slabs
        index = lambda b, g, last=n_slabs - 1: (jnp.minimum(b * n_groups + g, last), 0)
        cast_in_specs.append(pl.BlockSpec((slab, cols), index))
        cast_out_specs.append(pl.BlockSpec((slab, cols), index))
        cast_out_shapes.append(jax.ShapeDtypeStruct(w.shape, BF16))
    width = ATTN_PAIRS * LANES
    qcol = ZB_Q * D_MODEL // width
    kcol = ZB_K * D_MODEL // width
    vcol = ZB_VA * D_MODEL // width
    outs = pl.pallas_call(
        functools.partial(_attn_kernel, n_qt, len(weights)),
        grid=(batch, n_groups),
        in_specs=[
            pl.BlockSpec((seq, width), lambda b, g: (b, qcol + g)),
            pl.BlockSpec((seq, width), lambda b, g: (b, kcol + g)),
            pl.BlockSpec((seq, width), lambda b, g: (b, vcol + g)),
            pl.BlockSpec((seq, LANES), lambda b, g: (b, 0)),
            pl.BlockSpec((ATTN_PAIRS, LANES, LANES), lambda b, g: (g, 0, 0)),
            pl.BlockSpec((8, LANES), lambda b, g: (0, 0)),
            pl.BlockSpec((LANES, LANES), lambda b, g: (0, 0)),
        ] + cast_in_specs,
        out_specs=[pl.BlockSpec((seq, width), lambda b, g: (b, g))] + cast_out_specs,
        out_shape=[jax.ShapeDtypeStruct((n, FOX_WIDTH), BF16)] + cast_out_shapes,
        scratch_shapes=[
            pltpu.VMEM((ATTN_PAIRS, seq, 2 * LANES), BF16),
            pltpu.VMEM((ATTN_PAIRS, n_qt, 2, PV_ROWS, TQ), BF16),
            pltpu.VMEM((ATTN_PAIRS, n_qt, 2 * TQ, 2 * LANES), BF16),
            pltpu.VMEM((2, ATTN_PAIRS, seq, 2 * TQ), F32),
        ],
        compiler_params=pltpu.CompilerParams(
            dimension_semantics=("arbitrary", "arbitrary"),
            vmem_limit_bytes=VMEM_LIMIT),
        name="attn",
    )(z, z, z, cparts, place, crow, eye, *weights)
    return outs[0], outs[1:]


def _mix_kernel(u_ref, v_ref, ga_ref, gb_ref, o_ref, x_ref, lng_ref, lnb_ref, ws_ref,
                bias_ref, wa_ref, wb_ref, wo_ref, x1_ref, vn_scr, a_scr, yb_scr):
    yb_scr[...] = gb_ref[...].astype(F32) * jnp.dot(o_ref[...], wb_ref[...],
                                                     preferred_element_type=F32)

    v = v_ref[...].astype(F32)
    mu = jnp.mean(v, axis=-1, keepdims=True)
    vc = v - mu
    var = jnp.mean(vc * vc, axis=-1, keepdims=True)
    vn_scr[...] = (vc * lax.rsqrt(var + EPS) * lng_ref[...] + lnb_ref[...]).astype(BF16)

    t_idx = lax.broadcasted_iota(jnp.int32, (GMLP_BLOCK, GMLP_BLOCK), 0)
    s_idx = lax.broadcasted_iota(jnp.int32, (GMLP_BLOCK, GMLP_BLOCK), 1)
    causal = (s_idx // CHUNK) <= (t_idx // CHUNK)
    for g in range(GMLP_GROUPS):
        cols = slice(g * GMLP_BLOCK, (g + 1) * GMLP_BLOCK)
        wg = jnp.where(causal, ws_ref[g], jnp.zeros_like(ws_ref[g]))
        for r in range(TM_MIX // GMLP_BLOCK):
            rows = slice(r * GMLP_BLOCK, (r + 1) * GMLP_BLOCK)
            mixed = jnp.dot(wg, vn_scr[rows, cols], preferred_element_type=F32) + bias_ref[:, cols]
            a_scr[rows, cols] = (u_ref[rows, cols].astype(F32) * mixed).astype(BF16)

    ya = jnp.dot(a_scr[...], wa_ref[...], preferred_element_type=F32)
    merged = ga_ref[...].astype(F32) * ya + yb_scr[...]
    x1_ref[...] = x_ref[...] + jnp.dot(merged.astype(BF16), wo_ref[...],
                                       preferred_element_type=F32)


def _mix(z, o, x2, ln_g, ln_b, w_s, bias_full, w_a, w_b, w_o):
    n = x2.shape[0]
    row = lambda c: pl.BlockSpec((TM_MIX, D_MODEL), lambda i, c=c: (i, c))
    const = lambda shape: pl.BlockSpec(shape, lambda i: (0,) * len(shape))
    return pl.pallas_call(
        _mix_kernel,
        grid=(n // TM_MIX,),
        in_specs=[
            row(ZB_U), row(ZB_V), row(ZB_GA), row(ZB_GB), row(0), row(0),
            const((1, GMLP_WIDTH)), const((1, GMLP_WIDTH)),
            const((GMLP_GROUPS, GMLP_BLOCK, GMLP_BLOCK)),
            const((GMLP_BLOCK, GMLP_WIDTH)),
            const((GMLP_WIDTH, D_MODEL)), const((FOX_WIDTH, D_MODEL)),
            const((D_MODEL, D_MODEL)),
        ],
        out_specs=row(0),
        out_shape=jax.ShapeDtypeStruct((n, D_MODEL), F32),
        scratch_shapes=[
            pltpu.VMEM((TM_MIX, GMLP_WIDTH), BF16),
            pltpu.VMEM((TM_MIX, GMLP_WIDTH), BF16),
            pltpu.VMEM((TM_MIX, D_MODEL), F32),
        ],
        compiler_params=pltpu.CompilerParams(
            dimension_semantics=("arbitrary",),
            vmem_limit_bytes=VMEM_LIMIT),
        name="mix",
    )(z, z, z, z, o, x2, ln_g, ln_b, w_s, bias_full, w_a, w_b, w_o)


def _ffn_kernel(tiles_per_batch, halo_ref, x1_ref, p_ref, g2_ref, perm_ref, wup_ref, cw_ref, cb_ref,
                wd_ref, g3_ref, wpg_ref, wple_ref, gf_ref, out_ref,
                hext_scr, ug_scr, ul_scr, act_scr, acc_scr):
    i = pl.program_id(0)
    n_vregs = TM_FFN // SUBLANES

    @pl.when(pl.program_id(1) == 0)
    def _():
        halo = _rms(halo_ref[...], g2_ref[...]).astype(BF16)
        hext_scr[:HALO, :] = jnp.where(i % tiles_per_batch == 0, jnp.zeros_like(halo), halo)
        h = _rms(x1_ref[...], g2_ref[...]).astype(BF16)
        hext_scr[HALO:, :] = jnp.dot(perm_ref[...], h, preferred_element_type=F32).astype(BF16)

    def conv(scr, first_col, lanes, scale=None):
        cols = slice(first_col + lanes.start, first_col + lanes.stop)
        fold = (lambda w: w) if scale is None else (lambda w: scale * w)
        u = scr[:, lanes]
        body = u[HALO:]
        sublane = lax.broadcasted_iota(jnp.int32, (SUBLANES, LANES), 0)
        before = []
        for back in range(1, CONV_WIDTH):
            tile = body[(n_vregs - back) * SUBLANES:(n_vregs - back + 1) * SUBLANES]
            halo_row = jnp.broadcast_to(u[HALO - back:HALO - back + 1], (SUBLANES, LANES))
            before.insert(0, jnp.where(sublane == 0, halo_row, pltpu.roll(tile, 1, 0)))
        ext = jnp.concatenate(before + [body], axis=0)
        out = fold(cb_ref[:, cols])
        for t in range(CONV_WIDTH):
            out = out + fold(cw_ref[t:t + 1, cols]) * ext[t * SUBLANES:t * SUBLANES + TM_FFN]
        return out

    def chunk(start, width):
        ug_scr[:, :width] = jnp.dot(hext_scr[...], wup_ref[:, start:start + width],
                                    preferred_element_type=F32)
        ul_scr[:, :width] = jnp.dot(hext_scr[...], wup_ref[:, D_FF + start:D_FF + start + width],
                                    preferred_element_type=F32)
        for c in range(width // LANES):
            lanes = slice(c * LANES, (c + 1) * LANES)
            gate = conv(ug_scr, start, lanes)
            half_lin = conv(ul_scr, D_FF + start, lanes, scale=0.5)
            act_scr[:, lanes] = _gelu_times(gate, half_lin).astype(BF16)
        down = jnp.dot(act_scr[:, :width], wd_ref[start:start + width, :],
                       preferred_element_type=F32)
        for c in range(D_MODEL // LANES):
            piece = down[:, c * LANES:(c + 1) * LANES]
            if start == 0:
                acc_scr[c] = piece
            else:
                acc_scr[c] += piece

    j = pl.program_id(1)
    start = 0
    for step, width in enumerate(FFN_CHUNKS):
        pl.when(j == step)(functools.partial(chunk, start, width))
        start += width

    @pl.when(j == len(FFN_CHUNKS) - 1)
    def _():
        def rows_in_order(c):
            group = n_vregs // SUBLANES
            tiles = []
            for n in range(n_vregs):
                base = (n % group) * SUBLANES * SUBLANES + n // group
                tiles.append(acc_scr[c, pl.ds(base, SUBLANES, stride=SUBLANES), :])
            return jnp.concatenate(tiles, axis=0)

        acc = jnp.concatenate([rows_in_order(c) for c in range(D_MODEL // LANES)], axis=1)
        x2 = x1_ref[...] + acc
        h3 = _rms(x2, g3_ref[...]).astype(BF16)
        gate = jax.nn.sigmoid(jnp.dot(h3, wpg_ref[...], preferred_element_type=F32))
        ple = jnp.dot(p_ref[...].astype(BF16), wple_ref[...], preferred_element_type=F32)
        x3 = x2 + ple * gate
        out_ref[...] = _rms(x3, gf_ref[...])


def _ffn(x1, p2, g2, w_up, conv_w, conv_b, w_down, g3, w_pg, w_ple, gf, seq):
    n = x1.shape[0]
    halo_blocks = TM_FFN // HALO
    widest = max(FFN_CHUNKS)
    row = lambda width: pl.BlockSpec((TM_FFN, width), lambda i, j: (i, 0))
    const = lambda shape: pl.BlockSpec(shape, lambda i, j: (0,) * len(shape),
                                       pipeline_mode=pl.Buffered(1))
    r = np.arange(TM_FFN)
    n_vregs = TM_FFN // SUBLANES
    perm = np.zeros((TM_FFN, TM_FFN), np.float32)
    perm[SUBLANES * (r % n_vregs) + r // n_vregs, r] = 1.0
    return pl.pallas_call(
        functools.partial(_ffn_kernel, seq // TM_FFN),
        grid=(n // TM_FFN, len(FFN_CHUNKS)),
        in_specs=[
            pl.BlockSpec((HALO, D_MODEL),
                         lambda i, j: (jnp.maximum(i * halo_blocks - 1, 0), 0)),
            row(D_MODEL), row(PLE_DIM), const((1, D_MODEL)), const((TM_FFN, TM_FFN)),
            const((D_MODEL, 2 * D_FF)), const((CONV_WIDTH, 2 * D_FF)), const((1, 2 * D_FF)),
            const((D_FF, D_MODEL)), const((1, D_MODEL)), const((D_MODEL, D_MODEL)),
            const((PLE_DIM, D_MODEL)), const((1, D_MODEL)),
        ],
        out_specs=row(D_MODEL),
        out_shape=jax.ShapeDtypeStruct((n, D_MODEL), F32),
        scratch_shapes=[
            pltpu.VMEM((HALO + TM_FFN, D_MODEL), BF16),
            pltpu.VMEM((HALO + TM_FFN, widest), F32),
            pltpu.VMEM((HALO + TM_FFN, widest), F32),
            pltpu.VMEM((TM_FFN, widest), BF16),
            pltpu.VMEM((D_MODEL // LANES, TM_FFN, LANES), F32),
        ],
        compiler_params=pltpu.CompilerParams(
            dimension_semantics=("arbitrary", "arbitrary"),
            vmem_limit_bytes=VMEM_LIMIT),
        name="ffn",
    )(x1, x1, p2, g2, jnp.asarray(perm, BF16), w_up, conv_w, conv_b, w_down, g3, w_pg, w_ple, gf)


def _layer(x2, p2, batch, seq, norm_mix_g, w_in, b_f, gmlp_ln_g, gmlp_ln_b, gmlp_w_s, gmlp_b_s,
           w_branch_a, w_branch_b, w_out, norm_ffn_g, w_up, conv_w, conv_b, w_down,
           norm_ple_g, w_ple, w_ple_gate, out_g):
    o_f = 2 * GMLP_WIDTH + 3 * FOX_WIDTH
    w_all = w_in.astype(BF16)
    w_gates = w_all[:, o_f + FOX_HEADS:]
    w_f = jnp.pad(w_all[:, o_f:o_f + FOX_HEADS], ((0, 0), (0, LANES - FOX_HEADS)))
    b_fp = jnp.pad(b_f, (0, LANES - FOX_HEADS)).reshape(1, LANES)
    tri = jnp.asarray(np.tril(np.ones((CUM_BLK, CUM_BLK), np.float32)), BF16)

    z, cparts = _inproj(x2, norm_mix_g.reshape(1, -1), w_all, w_gates, w_f, b_fp, tri, seq)
    o, (w_a, w_b, w_o, w_u, w_d, w_pg, w_pl) = _attn(
        z, cparts, batch, seq,
        (w_branch_a, w_branch_b, w_out, w_up, w_down, w_ple_gate, w_ple))

    bias_full = jnp.repeat(gmlp_b_s.T, GMLP_BLOCK, axis=1)
    x1 = _mix(z, o, x2, gmlp_ln_g.reshape(1, -1), gmlp_ln_b.reshape(1, -1),
              gmlp_w_s.astype(BF16), bias_full, w_a, w_b, w_o)

    return _ffn(x1, p2, norm_ffn_g.reshape(1, -1), w_u, conv_w, conv_b.reshape(1, -1), w_d,
                norm_ple_g.reshape(1, -1), w_pg, w_pl, out_g.reshape(1, -1), seq)


def kernel(x, p, norm_mix_g, w_in, b_f, gmlp_ln_g, gmlp_ln_b, gmlp_w_s, gmlp_b_s, w_branch_a,
           w_branch_b, w_out, norm_ffn_g, w_up, conv_w, conv_b, w_down, norm_ple_g, w_ple,
           w_ple_gate, norm_final_g):
    batch, seq, d = x.shape
    depth = p.shape[0]
    assert d == D_MODEL and depth == 1, "the fused final norm assumes a single layer"
    assert seq % TM_IN == 0 and seq % TM_FFN == 0 and seq % TQ == 0
    x2 = x.reshape(batch * seq, d)
    out = _layer(x2, p[0].reshape(batch * seq, PLE_DIM), batch, seq,
                 norm_mix_g[0], w_in[0], b_f[0], gmlp_ln_g[0], gmlp_ln_b[0], gmlp_w_s[0],
                 gmlp_b_s[0], w_branch_a[0], w_branch_b[0], w_out[0], norm_ffn_g[0], w_up[0],
                 conv_w[0], conv_b[0], w_down[0], norm_ple_g[0], w_ple[0], w_ple_gate[0],
                 norm_final_g)
    return out.reshape(batch, seq, d)
```

```python
import functools

import numpy as np
import jax
import jax.numpy as jnp
from jax import lax
from jax.experimental import pallas as pl
from jax.experimental.pallas import tpu as pltpu

F32 = jnp.float32
BF16 = jnp.bfloat16

D_MODEL = 1024
CHUNK = 64
PLE_DIM = 256
EPS = 1e-6
GMLP_GROUPS = 8
GMLP_BLOCK = 128
GMLP_WIDTH = 1024
FOX_HEADS = 16
FOX_HEAD_DIM = 64
FOX_WIDTH = 1024
D_FF = 2816
CONV_WIDTH = 3

LANES = 128
SUBLANES = 8
BF16_ROWS = 16
HEAD_PAIRS = FOX_HEADS // 2
N_PARTS = 3
LOG2E = 1.4426950408889634
Q_SCALE = FOX_HEAD_DIM ** -0.5 * LOG2E
PV_ROWS = FOX_HEAD_DIM + BF16_ROWS
BIAS_LANES = 4 * N_PARTS
Q_BIAS_OFFSET = 16

ZB_U, ZB_V, ZB_Q, ZB_K, ZB_VA, ZB_GA, ZB_GB = range(7)
Z_COLS = 7 * D_MODEL

TM_IN = 2048
RC_IN = 256
CUM_BLK = 256
TQ = 256
ATTN_PAIRS = 2
TM_MIX = 512
TM_FFN = 512
FFN_CHUNKS = (2816,)
HALO = BF16_ROWS

VMEM_CAPACITY = 64 * 1024 * 1024
VMEM_LIMIT = VMEM_CAPACITY - 8 * 1024 * 1024


def _rms(x, g):
    return x * lax.rsqrt(jnp.mean(x * x, axis=-1, keepdims=True) + EPS) * g


def _split3(x):
    hi = x.astype(BF16)
    r1 = x - hi.astype(F32)
    mid = r1.astype(BF16)
    lo = (r1 - mid.astype(F32)).astype(BF16)
    return hi, mid, lo


def _pack_parts(x):
    hi, mid, lo = (part.astype(F32) for part in _split3(x))
    lane = lax.broadcasted_iota(jnp.int32, x.shape, 1)
    packed = jnp.where(lane < FOX_HEADS, hi,
                       jnp.where(lane < 2 * FOX_HEADS, pltpu.roll(mid, FOX_HEADS, 1),
                                 pltpu.roll(lo, 2 * FOX_HEADS, 1)))
    return packed.astype(BF16)


def _gelu(x):
    a = -2.0 * np.sqrt(2.0 / np.pi) * LOG2E
    e = jnp.exp2(x * (a + (a * 0.044715) * (x * x)))
    return x * (1.0 / (1.0 + e))


def _gelu_times(x, half_y):
    c = np.sqrt(2.0 / np.pi)
    t = jnp.tanh(x * (c + (c * 0.044715) * (x * x)))
    xy = x * half_y
    return xy + xy * t


def _log_sigmoid(x):
    return jnp.minimum(x, 0.0) - jnp.log1p(jnp.exp(-jnp.abs(x)))


def _inproj_kernel(tiles_per_batch, x_ref, g_ref, w_ref, wg_ref, wf_ref, bf_ref, tri_ref,
                   z_ref, cp_ref, h_scr, carry_scr):
    i = pl.program_id(0)
    j = pl.program_id(1)

    @pl.when(j == 0)
    def _():
        hb = _rms(x_ref[...], g_ref[...]).astype(BF16)
        h_scr[...] = hb
        zf = jnp.dot(hb, wf_ref[...], preferred_element_type=F32)
        logf = _log_sigmoid(zf + bf_ref[...])

        @pl.when(i % tiles_per_batch == 0)
        def _():
            carry_scr[...] = jnp.zeros_like(carry_scr)

        carry = carry_scr[0:1, :]
        for r in range(TM_IN // CUM_BLK):
            rows = slice(r * CUM_BLK, (r + 1) * CUM_BLK)
            parts = jnp.concatenate(_split3(logf[rows]), axis=1)
            cs = jnp.dot(tri_ref[...], parts, preferred_element_type=F32)
            cum = (cs[:, :LANES] + cs[:, LANES:2 * LANES]) + cs[:, 2 * LANES:] + carry
            carry = cum[CUM_BLK - 1:CUM_BLK, :]
            cp_ref[rows, :] = _pack_parts(cum * LOG2E)
        carry_scr[...] = jnp.broadcast_to(carry, carry_scr.shape)

    def project(fn, weights=w_ref, first_block=0):
        cols = pl.ds(pl.multiple_of((j - first_block) * D_MODEL, D_MODEL), D_MODEL)
        for r in range(TM_IN // RC_IN):
            rows = slice(r * RC_IN, (r + 1) * RC_IN)
            acc = jnp.dot(h_scr[rows, :], weights[:, cols], preferred_element_type=F32)
            z_ref[rows, :] = fn(acc).astype(BF16)

    @pl.when(j <= ZB_V)
    def _():
        project(_gelu)

    @pl.when(j == ZB_Q)
    def _():
        project(lambda a: a * Q_SCALE)

    @pl.when(jnp.logical_or(j == ZB_K, j == ZB_VA))
    def _():
        project(lambda a: a)

    @pl.when(j >= ZB_GA)
    def _():
        project(jax.nn.sigmoid, wg_ref, ZB_GA)


def _inproj(x2, g, w_all, w_gates, w_f, b_f, tri, seq):
    n = x2.shape[0]
    grid = (n // TM_IN, Z_COLS // D_MODEL)
    return pl.pallas_call(
        functools.partial(_inproj_kernel, seq // TM_IN),
        grid=grid,
        in_specs=[
            pl.BlockSpec((TM_IN, D_MODEL), lambda i, j: (i, 0)),
            pl.BlockSpec((1, D_MODEL), lambda i, j: (0, 0)),
            pl.BlockSpec(w_all.shape, lambda i, j: (0, 0), pipeline_mode=pl.Buffered(1)),
            pl.BlockSpec(w_gates.shape, lambda i, j: (0, 0), pipeline_mode=pl.Buffered(1)),
            pl.BlockSpec((D_MODEL, LANES), lambda i, j: (0, 0)),
            pl.BlockSpec((1, LANES), lambda i, j: (0, 0)),
            pl.BlockSpec((CUM_BLK, CUM_BLK), lambda i, j: (0, 0)),
        ],
        out_specs=[
            pl.BlockSpec((TM_IN, D_MODEL), lambda i, j: (i, j)),
            pl.BlockSpec((TM_IN, LANES), lambda i, j: (i, 0)),
        ],
        out_shape=[
            jax.ShapeDtypeStruct((n, Z_COLS), BF16),
            jax.ShapeDtypeStruct((n, LANES), BF16),
        ],
        scratch_shapes=[
            pltpu.VMEM((TM_IN, D_MODEL), BF16),
            pltpu.VMEM((8, LANES), F32),
        ],
        compiler_params=pltpu.CompilerParams(
            dimension_semantics=("arbitrary", "arbitrary"),
            vmem_limit_bytes=VMEM_LIMIT),
        name="inproj",
    )(x2, g, w_all, w_gates, w_f, b_f, tri)


def _attn_kernel(n_qt, n_cast, q_ref, k_ref, v_ref, cp_ref, place_ref, crow_ref, eye_ref, *rest):
    cast_in, (o_ref, *cast_out) = rest[:n_cast], rest[n_cast:2 * n_cast + 1]
    kaug_scr, vt_scr, qr_scr, s_scr = rest[2 * n_cast + 1:]
    for src, dst in zip(cast_in, cast_out):
        dst[...] = src[...].astype(BF16)

    nt = (((1,), (1,)), ((), ()))
    chains = range(ATTN_PAIRS)
    lane = lax.broadcasted_iota(jnp.int32, (TQ, LANES), 1)

    for c in chains:
        lanes = slice(c * LANES, (c + 1) * LANES)
        aug = jnp.dot(cp_ref[...], place_ref[c], preferred_element_type=F32) + crow_ref[0:1, :]
        kaug_scr[c, :, :LANES] = k_ref[:, lanes]
        kaug_scr[c, :, LANES:] = aug.astype(BF16)
        bias_lane = lax.broadcasted_iota(jnp.int32, aug.shape, 1) < BIAS_LANES
        augq = [jnp.where(bias_lane, pltpu.roll(aug, LANES - Q_BIAS_OFFSET * (1 + hh), 1), 0.0)
                .astype(BF16) for hh in range(2)]
        ones = jnp.ones((PV_ROWS - FOX_HEAD_DIM, TQ), BF16)
        for t in range(n_qt):
            rows = slice(t * TQ, (t + 1) * TQ)
            vt = lax.dot_general(eye_ref[...], v_ref[rows, lanes], nt,
                                 preferred_element_type=F32).astype(BF16)
            for hh in range(2):
                vt_scr[c, t, hh, :FOX_HEAD_DIM, :] = vt[hh * FOX_HEAD_DIM:(hh + 1) * FOX_HEAD_DIM]
                vt_scr[c, t, hh, FOX_HEAD_DIM:, :] = ones
            q = q_ref[rows, lanes]
            zero = jnp.zeros_like(q)
            for hh in range(2):
                own = (lane >= FOX_HEAD_DIM) if hh else (lane < FOX_HEAD_DIM)
                qrows = slice(hh * TQ, (hh + 1) * TQ)
                qr_scr[c, t, qrows, :LANES] = jnp.where(own, q, zero)
                qr_scr[c, t, qrows, LANES:] = augq[hh][rows]

    kpos = lax.broadcasted_iota(jnp.int32, (TQ, 2 * TQ), 0)
    col = lax.broadcasted_iota(jnp.int32, (TQ, 2 * TQ), 1)
    causal = kpos <= jnp.where(col >= TQ, col - TQ, col)

    def pass1(qi):
        nk = (qi + 1) * TQ
        smax = []
        for c in chains:
            s = lax.dot_general(kaug_scr[c, :nk, :], qr_scr[c, qi], nt,
                                preferred_element_type=F32)
            diag = jnp.where(causal, s[nk - TQ:], -1e30)
            m = jnp.max(diag, axis=0, keepdims=True)
            if qi:
                s_scr[qi % 2, c, :nk - TQ, :] = s[:nk - TQ]
                m = jnp.maximum(m, jnp.max(s[:nk - TQ], axis=0, keepdims=True))
            s_scr[qi % 2, c, nk - TQ:nk, :] = diag
            smax.append(m)
        return smax

    def pass2(qi, smax):
        for c in chains:
            acc = [None, None]
            for t in range(qi + 1):
                p = jnp.exp2(s_scr[qi % 2, c, t * TQ:(t + 1) * TQ, :] - smax[c]).astype(BF16)
                for hh in range(2):
                    d = jnp.dot(vt_scr[c, t, hh], p[:, hh * TQ:(hh + 1) * TQ],
                                preferred_element_type=F32)
                    acc[hh] = d if acc[hh] is None else acc[hh] + d
            heads = [a[:FOX_HEAD_DIM] * (1.0 / a[FOX_HEAD_DIM:FOX_HEAD_DIM + 1]) for a in acc]
            ot = jnp.concatenate(heads, axis=0)
            o_ref[qi * TQ:(qi + 1) * TQ, c * LANES:(c + 1) * LANES] = ot.T.astype(BF16)

    smax = pass1(0)
    for qi in range(n_qt):
        nxt = pass1(qi + 1) if qi + 1 < n_qt else None
        pass2(qi, smax)
        smax = nxt


def _attn_consts():
    place = np.zeros((HEAD_PAIRS, LANES, LANES), np.float32)
    crow = np.zeros((8, LANES), np.float32)
    q0, q1 = Q_BIAS_OFFSET, 2 * Q_BIAS_OFFSET
    for p in range(HEAD_PAIRS):
        for a in range(N_PARTS):
            place[p, a * FOX_HEADS + 2 * p, 0 + a] = -1.0
            place[p, a * FOX_HEADS + 2 * p + 1, 6 + a] = -1.0
            place[p, a * FOX_HEADS + 2 * p, q0 + 3 + a] = 1.0
            place[p, a * FOX_HEADS + 2 * p + 1, q1 + 9 + a] = 1.0
    crow[0, 3:6] = 1.0
    crow[0, 9:12] = 1.0
    crow[0, q0 + 0:q0 + 3] = 1.0
    crow[0, q1 + 6:q1 + 9] = 1.0
    eye = np.eye(LANES, dtype=np.float32)
    return jnp.asarray(place, BF16), jnp.asarray(crow, F32), jnp.asarray(eye, BF16)


def _attn(z, cparts, batch, seq, weights):
    n = z.shape[0]
    n_qt = seq // TQ
    place, crow, eye = _attn_consts()
    n_groups = HEAD_PAIRS // ATTN_PAIRS
    cast_in_specs, cast_out_specs, cast_out_shapes = [], [], []
    for w in weights:
        rows, cols = w.shape
        n_slabs = max(s for s in range(1, batch * n_groups + 1)
                      if rows % s == 0 and (rows // s) % BF16_ROWS == 0)
        slab = rows // n_slabs
        index = lambda b, g, last=n_slabs - 1: (jnp.minimum(b * n_groups + g, last), 0)
        cast_in_specs.append(pl.BlockSpec((slab, cols), index))
        cast_out_specs.append(pl.BlockSpec((slab, cols), index))
        cast_out_shapes.append(jax.ShapeDtypeStruct(w.shape, BF16))
    width = ATTN_PAIRS * LANES
    qcol = ZB_Q * D_MODEL // width
    kcol = ZB_K * D_MODEL // width
    vcol = ZB_VA * D_MODEL // width
    outs = pl.pallas_call(
        functools.partial(_attn_kernel, n_qt, len(weights)),
        grid=(batch, n_groups),
        in_specs=[
            pl.BlockSpec((seq, width), lambda b, g: (b, qcol + g)),
            pl.BlockSpec((seq, width), lambda b, g: (b, kcol + g)),
            pl.BlockSpec((seq, width), lambda b, g: (b, vcol + g)),
            pl.BlockSpec((seq, LANES), lambda b, g: (b, 0)),
            pl.BlockSpec((ATTN_PAIRS, LANES, LANES), lambda b, g: (g, 0, 0)),
            pl.BlockSpec((8, LANES), lambda b, g: (0, 0)),
            pl.BlockSpec((LANES, LANES), lambda b, g: (0, 0)),
        ] + cast_in_specs,
        out_specs=[pl.BlockSpec((seq, width), lambda b, g: (b, g))] + cast_out_specs,
        out_shape=[jax.ShapeDtypeStruct((n, FOX_WIDTH), BF16)] + cast_out_shapes,
        scratch_shapes=[
            pltpu.VMEM((ATTN_PAIRS, seq, 2 * LANES), BF16),
            pltpu.VMEM((ATTN_PAIRS, n_qt, 2, PV_ROWS, TQ), BF16),
            pltpu.VMEM((ATTN_PAIRS, n_qt, 2 * TQ, 2 * LANES), BF16),
            pltpu.VMEM((2, ATTN_PAIRS, seq, 2 * TQ), F32),
        ],
        compiler_params=pltpu.CompilerParams(
            dimension_semantics=("arbitrary", "arbitrary"),
            vmem_limit_bytes=VMEM_LIMIT),
        name="attn",
    )(z, z, z, cparts, place, crow, eye, *weights)
    return outs[0], outs[1:]


def _mix_kernel(u_ref, v_ref, ga_ref, gb_ref, o_ref, x_ref, lng_ref, lnb_ref, ws_ref,
                bias_ref, wa_ref, wb_ref, wo_ref, x1_ref, vn_scr, a_scr, yb_scr):
    yb_scr[...] = gb_ref[...].astype(F32) * jnp.dot(o_ref[...], wb_ref[...],
                                                     preferred_element_type=F32)

    v = v_ref[...].astype(F32)
    mu = jnp.mean(v, axis=-1, keepdims=True)
    vc = v - mu
    var = jnp.mean(vc * vc, axis=-1, keepdims=True)
    vn_scr[...] = (vc * lax.rsqrt(var + EPS) * lng_ref[...] + lnb_ref[...]).astype(BF16)

    t_idx = lax.broadcasted_iota(jnp.int32, (GMLP_BLOCK, GMLP_BLOCK), 0)
    s_idx = lax.broadcasted_iota(jnp.int32, (GMLP_BLOCK, GMLP_BLOCK), 1)
    causal = (s_idx // CHUNK) <= (t_idx // CHUNK)
    for g in range(GMLP_GROUPS):
        cols = slice(g * GMLP_BLOCK, (g + 1) * GMLP_BLOCK)
        wg = jnp.where(causal, ws_ref[g], jnp.zeros_like(ws_ref[g]))
        for r in range(TM_MIX // GMLP_BLOCK):
            rows = slice(r * GMLP_BLOCK, (r + 1) * GMLP_BLOCK)
            mixed = jnp.dot(wg, vn_scr[rows, cols], preferred_element_type=F32) + bias_ref[:, cols]
            a_scr[rows, cols] = (u_ref[rows, cols].astype(F32) * mixed).astype(BF16)

    ya = jnp.dot(a_scr[...], wa_ref[...], preferred_element_type=F32)
    merged = ga_ref[...].astype(F32) * ya + yb_scr[...]
    x1_ref[...] = x_ref[...] + jnp.dot(merged.astype(BF16), wo_ref[...],
                                       preferred_element_type=F32)


def _mix(z, o, x2, ln_g, ln_b, w_s, bias_full, w_a, w_b, w_o):
    n = x2.shape[0]
    row = lambda c: pl.BlockSpec((TM_MIX, D_MODEL), lambda i, c=c: (i, c))
    const = lambda shape: pl.BlockSpec(shape, lambda i: (0,) * len(shape))
    return pl.pallas_call(
        _mix_kernel,
        grid=(n // TM_MIX,),
        in_specs=[
            row(ZB_U), row(ZB_V), row(ZB_GA), row(ZB_GB), row(0), row(0),
            const((1, GMLP_WIDTH)), const((1, GMLP_WIDTH)),
            const((GMLP_GROUPS, GMLP_BLOCK, GMLP_BLOCK)),
            const((GMLP_BLOCK, GMLP_WIDTH)),
            const((GMLP_WIDTH, D_MODEL)), const((FOX_WIDTH, D_MODEL)),
            const((D_MODEL, D_MODEL)),
        ],
        out_specs=row(0),
        out_shape=jax.ShapeDtypeStruct((n, D_MODEL), F32),
        scratch_shapes=[
            pltpu.VMEM((TM_MIX, GMLP_WIDTH), BF16),
            pltpu.VMEM((TM_MIX, GMLP_WIDTH), BF16),
            pltpu.VMEM((TM_MIX, D_MODEL), F32),
        ],
        compiler_params=pltpu.CompilerParams(
            dimension_semantics=("arbitrary",),
            vmem_limit_bytes=VMEM_LIMIT),
        name="mix",
    )(z, z, z, z, o, x2, ln_g, ln_b, w_s, bias_full, w_a, w_b, w_o)


def _ffn_kernel(tiles_per_batch, halo_ref, x1_ref, p_ref, g2_ref, perm_ref, wup_ref, cw_ref, cb_ref,
                wd_ref, g3_ref, wpg_ref, wple_ref, gf_ref, out_ref,
                hext_scr, ug_scr, ul_scr, act_scr, acc_scr):
    i = pl.program_id(0)
    n_vregs = TM_FFN // SUBLANES

    @pl.when(pl.program_id(1) == 0)
    def _():
        halo = _rms(halo_ref[...], g2_ref[...]).astype(BF16)
        hext_scr[:HALO, :] = jnp.where(i % tiles_per_batch == 0, jnp.zeros_like(halo), halo)
        h = _rms(x1_ref[...], g2_ref[...]).astype(BF16)
        hext_scr[HALO:, :] = jnp.dot(perm_ref[...], h, preferred_element_type=F32).astype(BF16)

    def conv(scr, first_col, lanes, scale=None):
        cols = slice(first_col + lanes.start, first_col + lanes.stop)
        fold = (lambda w: w) if scale is None else (lambda w: scale * w)
        u = scr[:, lanes]
        body = u[HALO:]
        sublane = lax.broadcasted_iota(jnp.int32, (SUBLANES, LANES), 0)
        before = []
        for back in range(1, CONV_WIDTH):
            tile = body[(n_vregs - back) * SUBLANES:(n_vregs - back + 1) * SUBLANES]
            halo_row = jnp.broadcast_to(u[HALO - back:HALO - back + 1], (SUBLANES, LANES))
            before.insert(0, jnp.where(sublane == 0, halo_row, pltpu.roll(tile, 1, 0)))
        ext = jnp.concatenate(before + [body], axis=0)
        out = fold(cb_ref[:, cols])
        for t in range(CONV_WIDTH):
            out = out + fold(cw_ref[t:t + 1, cols]) * ext[t * SUBLANES:t * SUBLANES + TM_FFN]
        return out

    def chunk(start, width):
        ug_scr[:, :width] = jnp.dot(hext_scr[...], wup_ref[:, start:start + width],
                                    preferred_element_type=F32)
        ul_scr[:, :width] = jnp.dot(hext_scr[...], wup_ref[:, D_FF + start:D_FF + start + width],
                                    preferred_element_type=F32)
        for c in range(width // LANES):
            lanes = slice(c * LANES, (c + 1) * LANES)
            gate = conv(ug_scr, start, lanes)
            half_lin = conv(ul_scr, D_FF + start, lanes, scale=0.5)
            act_scr[:, lanes] = _gelu_times(gate, half_lin).astype(BF16)
        down = jnp.dot(act_scr[:, :width], wd_ref[start:start + width, :],
                       preferred_element_type=F32)
        for c in range(D_MODEL // LANES):
            piece = down[:, c * LANES:(c + 1) * LANES]
            if start == 0:
                acc_scr[c] = piece
            else:
                acc_scr[c] += piece

    j = pl.program_id(1)
    start = 0
    for step, width in enumerate(FFN_CHUNKS):
        pl.when(j == step)(functools.partial(chunk, start, width))
        start += width

    @pl.when(j == len(FFN_CHUNKS) - 1)
    def _():
        def rows_in_order(c):
            group = n_vregs // SUBLANES
            tiles = []
            for n in range(n_vregs):
                base = (n % group) * SUBLANES * SUBLANES + n // group
                tiles.append(acc_scr[c, pl.ds(base, SUBLANES, stride=SUBLANES), :])
            return jnp.concatenate(tiles, axis=0)

        acc = jnp.concatenate([rows_in_order(c) for c in range(D_MODEL // LANES)], axis=1)
        x2 = x1_ref[...] + acc
        h3 = _rms(x2, g3_ref[...]).astype(BF16)
        gate = jax.nn.sigmoid(jnp.dot(h3, wpg_ref[...], preferred_element_type=F32))
        ple = jnp.dot(p_ref[...].astype(BF16), wple_ref[...], preferred_element_type=F32)
        x3 = x2 + ple * gate
        out_ref[...] = _rms(x3, gf_ref[...])


def _ffn(x1, p2, g2, w_up, conv_w, conv_b, w_down, g3, w_pg, w_ple, gf, seq):
    n = x1.shape[0]
    halo_blocks = TM_FFN // HALO
    widest = max(FFN_CHUNKS)
    row = lambda width: pl.BlockSpec((TM_FFN, width), lambda i, j: (i, 0))
    const = lambda shape: pl.BlockSpec(shape, lambda i, j: (0,) * len(shape),
                                       pipeline_mode=pl.Buffered(1))
    r = np.arange(TM_FFN)
    n_vregs = TM_FFN // SUBLANES
    perm = np.zeros((TM_FFN, TM_FFN), np.float32)
    perm[SUBLANES * (r % n_vregs) + r // n_vregs, r] = 1.0
    return pl.pallas_call(
        functools.partial(_ffn_kernel, seq // TM_FFN),
        grid=(n // TM_FFN, len(FFN_CHUNKS)),
        in_specs=[
            pl.BlockSpec((HALO, D_MODEL),
                         lambda i, j: (jnp.maximum(i * halo_blocks - 1, 0), 0)),
            row(D_MODEL), row(PLE_DIM), const((1, D_MODEL)), const((TM_FFN, TM_FFN)),
            const((D_MODEL, 2 * D_FF)), const((CONV_WIDTH, 2 * D_FF)), const((1, 2 * D_FF)),
            const((D_FF, D_MODEL)), const((1, D_MODEL)), const((D_MODEL, D_MODEL)),
            const((PLE_DIM, D_MODEL)), const((1, D_MODEL)),
        ],
        out_specs=row(D_MODEL),
        out_shape=jax.ShapeDtypeStruct((n, D_MODEL), F32),
        scratch_shapes=[
            pltpu.VMEM((HALO + TM_FFN, D_MODEL), BF16),
            pltpu.VMEM((HALO + TM_FFN, widest), F32),
            pltpu.VMEM((HALO + TM_FFN, widest), F32),
            pltpu.VMEM((TM_FFN, widest), BF16),
            pltpu.VMEM((D_MODEL // LANES, TM_FFN, LANES), F32),
        ],
        compiler_params=pltpu.CompilerParams(
            dimension_semantics=("arbitrary", "arbitrary"),
            vmem_limit_bytes=VMEM_LIMIT),
        name="ffn",
    )(x1, x1, p2, g2, jnp.asarray(perm, BF16), w_up, conv_w, conv_b, w_down, g3, w_pg, w_ple, gf)


def _layer(x2, p2, batch, seq, norm_mix_g, w_in, b_f, gmlp_ln_g, gmlp_ln_b, gmlp_w_s, gmlp_b_s,
           w_branch_a, w_branch_b, w_out, norm_ffn_g, w_up, conv_w, conv_b, w_down,
           norm_ple_g, w_ple, w_ple_gate, out_g):
    o_f = 2 * GMLP_WIDTH + 3 * FOX_WIDTH
    w_all = w_in.astype(BF16)
    w_gates = w_all[:, o_f + FOX_HEADS:]
    w_f = jnp.pad(w_all[:, o_f:o_f + FOX_HEADS], ((0, 0), (0, LANES - FOX_HEADS)))
    b_fp = jnp.pad(b_f, (0, LANES - FOX_HEADS)).reshape(1, LANES)
    tri = jnp.asarray(np.tril(np.ones((CUM_BLK, CUM_BLK), np.float32)), BF16)

    z, cparts = _inproj(x2, norm_mix_g.reshape(1, -1), w_all, w_gates, w_f, b_fp, tri, seq)
    o, (w_a, w_b, w_o, w_u, w_d, w_pg, w_pl) = _attn(
        z, cparts, batch, seq,
        (w_branch_a, w_branch_b, w_out, w_up, w_down, w_ple_gate, w_ple))

    bias_full = jnp.repeat(gmlp_b_s.T, GMLP_BLOCK, axis=1)
    x1 = _mix(z, o, x2, gmlp_ln_g.reshape(1, -1), gmlp_ln_b.reshape(1, -1),
              gmlp_w_s.astype(BF16), bias_full, w_a, w_b, w_o)

    return _ffn(x1, p2, norm_ffn_g.reshape(1, -1), w_u, conv_w, conv_b.reshape(1, -1), w_d,
                norm_ple_g.reshape(1, -1), w_pg, w_pl, out_g.reshape(1, -1), seq)


def kernel(x, p, norm_mix_g, w_in, b_f, gmlp_ln_g, gmlp_ln_b, gmlp_w_s, gmlp_b_s, w_branch_a,
           w_branch_b, w_out, norm_ffn_g, w_up, conv_w, conv_b, w_down, norm_ple_g, w_ple,
           w_ple_gate, norm_final_g):
    batch, seq, d = x.shape
    depth = p.shape[0]
    assert d == D_MODEL and depth == 1, "the fused final norm assumes a single layer"
    assert seq % TM_IN == 0 and seq % TM_FFN == 0 and seq % TQ == 0
    x2 = x.reshape(batch * seq, d)
    out = _layer(x2, p[0].reshape(batch * seq, PLE_DIM), batch, seq,
                 norm_mix_g[0], w_in[0], b_f[0], gmlp_ln_g[0], gmlp_ln_b[0], gmlp_w_s[0],
                 gmlp_b_s[0], w_branch_a[0], w_branch_b[0], w_out[0], norm_ffn_g[0], w_up[0],
                 conv_w[0], conv_b[0], w_down[0], norm_ple_g[0], w_ple[0], w_ple_gate[0],
                 norm_final_g)
    return out.reshape(batch, seq, d)
```

```python
import functools

import numpy as np
import jax
import jax.numpy as jnp
from jax import lax
from jax.experimental import pallas as pl
from jax.experimental.pallas import tpu as pltpu

F32 = jnp.float32
BF16 = jnp.bfloat16

D_MODEL = 1024
CHUNK = 64
PLE_DIM = 256
EPS = 1e-6
GMLP_GROUPS = 8
GMLP_BLOCK = 128
GMLP_WIDTH = 1024
FOX_HEADS = 16
FOX_HEAD_DIM = 64
FOX_WIDTH = 1024
D_FF = 2816
CONV_WIDTH = 3

LANES = 128
SUBLANES = 8
BF16_ROWS = 16
HEAD_PAIRS = FOX_HEADS // 2
N_PARTS = 3
LOG2E = 1.4426950408889634
Q_SCALE = FOX_HEAD_DIM ** -0.5 * LOG2E
PV_ROWS = FOX_HEAD_DIM + BF16_ROWS
BIAS_LANES = 4 * N_PARTS
Q_BIAS_OFFSET = 16

ZB_U, ZB_V, ZB_Q, ZB_K, ZB_VA, ZB_GA, ZB_GB = range(7)
Z_COLS = 7 * D_MODEL

TM_IN = 2048
RC_IN = 256
CUM_BLK = 256
TQ = 256
ATTN_PAIRS = 2
TM_MIX = 512
TM_FFN = 512
FFN_CHUNKS = (1024, 1024, 768)
HALO = BF16_ROWS

VMEM_CAPACITY = 64 * 1024 * 1024
VMEM_LIMIT = VMEM_CAPACITY - 8 * 1024 * 1024


def _rms(x, g):
    return x * lax.rsqrt(jnp.mean(x * x, axis=-1, keepdims=True) + EPS) * g


def _split3(x):
    hi = x.astype(BF16)
    r1 = x - hi.astype(F32)
    mid = r1.astype(BF16)
    lo = (r1 - mid.astype(F32)).astype(BF16)
    return hi, mid, lo


def _pack_parts(x):
    hi, mid, lo = (part.astype(F32) for part in _split3(x))
    lane = lax.broadcasted_iota(jnp.int32, x.shape, 1)
    packed = jnp.where(lane < FOX_HEADS, hi,
                       jnp.where(lane < 2 * FOX_HEADS, pltpu.roll(mid, FOX_HEADS, 1),
                                 pltpu.roll(lo, 2 * FOX_HEADS, 1)))
    return packed.astype(BF16)


def _gelu(x):
    a = -2.0 * np.sqrt(2.0 / np.pi) * LOG2E
    e = jnp.exp2(x * (a + (a * 0.044715) * (x * x)))
    return x * (1.0 / (1.0 + e))


def _gelu_times(x, half_y):
    c = np.sqrt(2.0 / np.pi)
    t = jnp.tanh(x * (c + (c * 0.044715) * (x * x)))
    xy = x * half_y
    return xy + xy * t


def _log_sigmoid(x):
    return jnp.minimum(x, 0.0) - jnp.log1p(jnp.exp(-jnp.abs(x)))


def _inproj_kernel(tiles_per_batch, x_ref, g_ref, w_ref, wg_ref, wf_ref, bf_ref, tri_ref,
                   z_ref, cp_ref, h_scr, carry_scr):
    i = pl.program_id(0)
    j = pl.program_id(1)

    @pl.when(j == 0)
    def _():
        hb = _rms(x_ref[...], g_ref[...]).astype(BF16)
        h_scr[...] = hb
        zf = jnp.dot(hb, wf_ref[...], preferred_element_type=F32)
        logf = _log_sigmoid(zf + bf_ref[...])

        @pl.when(i % tiles_per_batch == 0)
        def _():
            carry_scr[...] = jnp.zeros_like(carry_scr)

        carry = carry_scr[0:1, :]
        for r in range(TM_IN // CUM_BLK):
            rows = slice(r * CUM_BLK, (r + 1) * CUM_BLK)
            parts = jnp.concatenate(_split3(logf[rows]), axis=1)
            cs = jnp.dot(tri_ref[...], parts, preferred_element_type=F32)
            cum = (cs[:, :LANES] + cs[:, LANES:2 * LANES]) + cs[:, 2 * LANES:] + carry
            carry = cum[CUM_BLK - 1:CUM_BLK, :]
            cp_ref[rows, :] = _pack_parts(cum * LOG2E)
        carry_scr[...] = jnp.broadcast_to(carry, carry_scr.shape)

    def project(fn, weights=w_ref, first_block=0):
        cols = pl.ds(pl.multiple_of((j - first_block) * D_MODEL, D_MODEL), D_MODEL)
        for r in range(TM_IN // RC_IN):
            rows = slice(r * RC_IN, (r + 1) * RC_IN)
            acc = jnp.dot(h_scr[rows, :], weights[:, cols], preferred_element_type=F32)
            z_ref[rows, :] = fn(acc).astype(BF16)

    @pl.when(j <= ZB_V)
    def _():
        project(_gelu)

    @pl.when(j == ZB_Q)
    def _():
        project(lambda a: a * Q_SCALE)

    @pl.when(jnp.logical_or(j == ZB_K, j == ZB_VA))
    def _():
        project(lambda a: a)

    @pl.when(j >= ZB_GA)
    def _():
        project(jax.nn.sigmoid, wg_ref, ZB_GA)


def _inproj(x2, g, w_all, w_gates, w_f, b_f, tri, seq):
    n = x2.shape[0]
    grid = (n // TM_IN, Z_COLS // D_MODEL)
    return pl.pallas_call(
        functools.partial(_inproj_kernel, seq // TM_IN),
        grid=grid,
        in_specs=[
            pl.BlockSpec((TM_IN, D_MODEL), lambda i, j: (i, 0)),
            pl.BlockSpec((1, D_MODEL), lambda i, j: (0, 0)),
            pl.BlockSpec(w_all.shape, lambda i, j: (0, 0), pipeline_mode=pl.Buffered(1)),
            pl.BlockSpec(w_gates.shape, lambda i, j: (0, 0), pipeline_mode=pl.Buffered(1)),
            pl.BlockSpec((D_MODEL, LANES), lambda i, j: (0, 0)),
            pl.BlockSpec((1, LANES), lambda i, j: (0, 0)),
            pl.BlockSpec((CUM_BLK, CUM_BLK), lambda i, j: (0, 0)),
        ],
        out_specs=[
            pl.BlockSpec((TM_IN, D_MODEL), lambda i, j: (i, j)),
            pl.BlockSpec((TM_IN, LANES), lambda i, j: (i, 0)),
        ],
        out_shape=[
            jax.ShapeDtypeStruct((n, Z_COLS), BF16),
            jax.ShapeDtypeStruct((n, LANES), BF16),
        ],
        scratch_shapes=[
            pltpu.VMEM((TM_IN, D_MODEL), BF16),
            pltpu.VMEM((8, LANES), F32),
        ],
        compiler_params=pltpu.CompilerParams(
            dimension_semantics=("arbitrary", "arbitrary"),
            vmem_limit_bytes=VMEM_LIMIT),
        name="inproj",
    )(x2, g, w_all, w_gates, w_f, b_f, tri)


def _attn_kernel(n_qt, n_cast, q_ref, k_ref, v_ref, cp_ref, place_ref, crow_ref, eye_ref, *rest):
    cast_in, (o_ref, *cast_out) = rest[:n_cast], rest[n_cast:2 * n_cast + 1]
    kaug_scr, vt_scr, qr_scr, s_scr = rest[2 * n_cast + 1:]
    for src, dst in zip(cast_in, cast_out):
        dst[...] = src[...].astype(BF16)

    nt = (((1,), (1,)), ((), ()))
    chains = range(ATTN_PAIRS)
    lane = lax.broadcasted_iota(jnp.int32, (TQ, LANES), 1)

    for c in chains:
        lanes = slice(c * LANES, (c + 1) * LANES)
        aug = jnp.dot(cp_ref[...], place_ref[c], preferred_element_type=F32) + crow_ref[0:1, :]
        kaug_scr[c, :, :LANES] = k_ref[:, lanes]
        kaug_scr[c, :, LANES:] = aug.astype(BF16)
        bias_lane = lax.broadcasted_iota(jnp.int32, aug.shape, 1) < BIAS_LANES
        augq = [jnp.where(bias_lane, pltpu.roll(aug, LANES - Q_BIAS_OFFSET * (1 + hh), 1), 0.0)
                .astype(BF16) for hh in range(2)]
        ones = jnp.ones((PV_ROWS - FOX_HEAD_DIM, TQ), BF16)
        for t in range(n_qt):
            rows = slice(t * TQ, (t + 1) * TQ)
            vt = lax.dot_general(eye_ref[...], v_ref[rows, lanes], nt,
                                 preferred_element_type=F32).astype(BF16)
            for hh in range(2):
                vt_scr[c, t, hh, :FOX_HEAD_DIM, :] = vt[hh * FOX_HEAD_DIM:(hh + 1) * FOX_HEAD_DIM]
                vt_scr[c, t, hh, FOX_HEAD_DIM:, :] = ones
            q = q_ref[rows, lanes]
            zero = jnp.zeros_like(q)
            for hh in range(2):
                own = (lane >= FOX_HEAD_DIM) if hh else (lane < FOX_HEAD_DIM)
                qrows = slice(hh * TQ, (hh + 1) * TQ)
                qr_scr[c, t, qrows, :LANES] = jnp.where(own, q, zero)
                qr_scr[c, t, qrows, LANES:] = augq[hh][rows]

    kpos = lax.broadcasted_iota(jnp.int32, (TQ, 2 * TQ), 0)
    col = lax.broadcasted_iota(jnp.int32, (TQ, 2 * TQ), 1)
    causal = kpos <= jnp.where(col >= TQ, col - TQ, col)

    def pass1(qi):
        nk = (qi + 1) * TQ
        smax = []
        for c in chains:
            s = lax.dot_general(kaug_scr[c, :nk, :], qr_scr[c, qi], nt,
                                preferred_element_type=F32)
            diag = jnp.where(causal, s[nk - TQ:], -1e30)
            m = jnp.max(diag, axis=0, keepdims=True)
            if qi:
                s_scr[qi % 2, c, :nk - TQ, :] = s[:nk - TQ]
                m = jnp.maximum(m, jnp.max(s[:nk - TQ], axis=0, keepdims=True))
            s_scr[qi % 2, c, nk - TQ:nk, :] = diag
            smax.append(m)
        return smax

    def pass2(qi, smax):
        for c in chains:
            acc = [None, None]
            for t in range(qi + 1):
                p = jnp.exp2(s_scr[qi % 2, c, t * TQ:(t + 1) * TQ, :] - smax[c]).astype(BF16)
                for hh in range(2):
                    d = jnp.dot(vt_scr[c, t, hh], p[:, hh * TQ:(hh + 1) * TQ],
                                preferred_element_type=F32)
                    acc[hh] = d if acc[hh] is None else acc[hh] + d
            heads = [a[:FOX_HEAD_DIM] * (1.0 / a[FOX_HEAD_DIM:FOX_HEAD_DIM + 1]) for a in acc]
            ot = jnp.concatenate(heads, axis=0)
            o_ref[qi * TQ:(qi + 1) * TQ, c * LANES:(c + 1) * LANES] = ot.T.astype(BF16)

    smax = pass1(0)
    for qi in range(n_qt):
        nxt = pass1(qi + 1) if qi + 1 < n_qt else None
        pass2(qi, smax)
        smax = nxt


def _attn_consts():
    place = np.zeros((HEAD_PAIRS, LANES, LANES), np.float32)
    crow = np.zeros((8, LANES), np.float32)
    q0, q1 = Q_BIAS_OFFSET, 2 * Q_BIAS_OFFSET
    for p in range(HEAD_PAIRS):
        for a in range(N_PARTS):
            place[p, a * FOX_HEADS + 2 * p, 0 + a] = -1.0
            place[p, a * FOX_HEADS + 2 * p + 1, 6 + a] = -1.0
            place[p, a * FOX_HEADS + 2 * p, q0 + 3 + a] = 1.0
            place[p, a * FOX_HEADS + 2 * p + 1, q1 + 9 + a] = 1.0
    crow[0, 3:6] = 1.0
    crow[0, 9:12] = 1.0
    crow[0, q0 + 0:q0 + 3] = 1.0
    crow[0, q1 + 6:q1 + 9] = 1.0
    eye = np.eye(LANES, dtype=np.float32)
    return jnp.asarray(place, BF16), jnp.asarray(crow, F32), jnp.asarray(eye, BF16)


def _attn(z, cparts, batch, seq, weights):
    n = z.shape[0]
    n_qt = seq // TQ
    place, crow, eye = _attn_consts()
    n_groups = HEAD_PAIRS // ATTN_PAIRS
    cast_in_specs, cast_out_specs, cast_out_shapes = [], [], []
    for w in weights:
        rows, cols = w.shape
        n_slabs = max(s for s in range(1, batch * n_groups + 1)
                      if rows % s == 0 and (rows // s) % BF16_ROWS == 0)
        slab = rows // n_slabs
        index = lambda b, g, last=n_slabs - 1: (jnp.minimum(b * n_groups + g, last), 0)
        cast_in_specs.append(pl.BlockSpec((slab, cols), index))
        cast_out_specs.append(pl.BlockSpec((slab, cols), index))
        cast_out_shapes.append(jax.ShapeDtypeStruct(w.shape, BF16))
    width = ATTN_PAIRS * LANES
    qcol = ZB_Q * D_MODEL // width
    kcol = ZB_K * D_MODEL // width
    vcol = ZB_VA * D_MODEL // width
    outs = pl.pallas_call(
        functools.partial(_attn_kernel, n_qt, len(weights)),
        grid=(batch, n_groups),
        in_specs=[
            pl.BlockSpec((seq, width), lambda b, g: (b, qcol + g)),
            pl.BlockSpec((seq, width), lambda b, g: (b, kcol + g)),
            pl.BlockSpec((seq, width), lambda b, g: (b, vcol + g)),
            pl.BlockSpec((seq, LANES), lambda b, g: (b, 0)),
            pl.BlockSpec((ATTN_PAIRS, LANES, LANES), lambda b, g: (g, 0, 0)),
            pl.BlockSpec((8, LANES), lambda b, g: (0, 0)),
            pl.BlockSpec((LANES, LANES), lambda b, g: (0, 0)),
        ] + cast_in_specs,
        out_specs=[pl.BlockSpec((seq, width), lambda b, g: (b, g))] + cast_out_specs,
        out_shape=[jax.ShapeDtypeStruct((n, FOX_WIDTH), BF16)] + cast_out_shapes,
        scratch_shapes=[
            pltpu.VMEM((ATTN_PAIRS, seq, 2 * LANES), BF16),
            pltpu.VMEM((ATTN_PAIRS, n_qt, 2, PV_ROWS, TQ), BF16),
            pltpu.VMEM((ATTN_PAIRS, n_qt, 2 * TQ, 2 * LANES), BF16),
            pltpu.VMEM((2, ATTN_PAIRS, seq, 2 * TQ), F32),
        ],
        compiler_params=pltpu.CompilerParams(
            dimension_semantics=("arbitrary", "arbitrary"),
            vmem_limit_bytes=VMEM_LIMIT),
        name="attn",
    )(z, z, z, cparts, place, crow, eye, *weights)
    return outs[0], outs[1:]


def _mix_kernel(u_ref, v_ref, ga_ref, gb_ref, o_ref, x_ref, lng_ref, lnb_ref, ws_ref,
                bias_ref, wa_ref, wb_ref, wo_ref, x1_ref, vn_scr, a_scr, yb_scr):
    yb_scr[...] = gb_ref[...].astype(F32) * jnp.dot(o_ref[...], wb_ref[...],
                                                     preferred_element_type=F32)

    v = v_ref[...].astype(F32)
    mu = jnp.mean(v, axis=-1, keepdims=True)
    vc = v - mu
    var = jnp.mean(vc * vc, axis=-1, keepdims=True)
    vn_scr[...] = (vc * lax.rsqrt(var + EPS) * lng_ref[...] + lnb_ref[...]).astype(BF16)

    t_idx = lax.broadcasted_iota(jnp.int32, (GMLP_BLOCK, GMLP_BLOCK), 0)
    s_idx = lax.broadcasted_iota(jnp.int32, (GMLP_BLOCK, GMLP_BLOCK), 1)
    causal = (s_idx // CHUNK) <= (t_idx // CHUNK)
    for g in range(GMLP_GROUPS):
        cols = slice(g * GMLP_BLOCK, (g + 1) * GMLP_BLOCK)
        wg = jnp.where(causal, ws_ref[g], jnp.zeros_like(ws_ref[g]))
        for r in range(TM_MIX // GMLP_BLOCK):
            rows = slice(r * GMLP_BLOCK, (r + 1) * GMLP_BLOCK)
            mixed = jnp.dot(wg, vn_scr[rows, cols], preferred_element_type=F32) + bias_ref[:, cols]
            a_scr[rows, cols] = (u_ref[rows, cols].astype(F32) * mixed).astype(BF16)

    ya = jnp.dot(a_scr[...], wa_ref[...], preferred_element_type=F32)
    merged = ga_ref[...].astype(F32) * ya + yb_scr[...]
    x1_ref[...] = x_ref[...] + jnp.dot(merged.astype(BF16), wo_ref[...],
                                       preferred_element_type=F32)


def _mix(z, o, x2, ln_g, ln_b, w_s, bias_full, w_a, w_b, w_o):
    n = x2.shape[0]
    row = lambda c: pl.BlockSpec((TM_MIX, D_MODEL), lambda i, c=c: (i, c))
    const = lambda shape: pl.BlockSpec(shape, lambda i: (0,) * len(shape))
    return pl.pallas_call(
        _mix_kernel,
        grid=(n // TM_MIX,),
        in_specs=[
            row(ZB_U), row(ZB_V), row(ZB_GA), row(ZB_GB), row(0), row(0),
            const((1, GMLP_WIDTH)), const((1, GMLP_WIDTH)),
            const((GMLP_GROUPS, GMLP_BLOCK, GMLP_BLOCK)),
            const((GMLP_BLOCK, GMLP_WIDTH)),
            const((GMLP_WIDTH, D_MODEL)), const((FOX_WIDTH, D_MODEL)),
            const((D_MODEL, D_MODEL)),
        ],
        out_specs=row(0),
        out_shape=jax.ShapeDtypeStruct((n, D_MODEL), F32),
        scratch_shapes=[
            pltpu.VMEM((TM_MIX, GMLP_WIDTH), BF16),
            pltpu.VMEM((TM_MIX, GMLP_WIDTH), BF16),
            pltpu.VMEM((TM_MIX, D_MODEL), F32),
        ],
        compiler_params=pltpu.CompilerParams(
            dimension_semantics=("arbitrary",),
            vmem_limit_bytes=VMEM_LIMIT),
        name="mix",
    )(z, z, z, z, o, x2, ln_g, ln_b, w_s, bias_full, w_a, w_b, w_o)


def _ffn_kernel(tiles_per_batch, halo_ref, x1_ref, p_ref, g2_ref, perm_ref, wup_ref, cw_ref, cb_ref,
                wd_ref, g3_ref, wpg_ref, wple_ref, gf_ref, out_ref,
                hext_scr, ug_scr, ul_scr, act_scr, acc_scr):
    i = pl.program_id(0)
    n_vregs = TM_FFN // SUBLANES

    @pl.when(pl.program_id(1) == 0)
    def _():
        halo = _rms(halo_ref[...], g2_ref[...]).astype(BF16)
        hext_scr[:HALO, :] = jnp.where(i % tiles_per_batch == 0, jnp.zeros_like(halo), halo)
        h = _rms(x1_ref[...], g2_ref[...]).astype(BF16)
        hext_scr[HALO:, :] = jnp.dot(perm_ref[...], h, preferred_element_type=F32).astype(BF16)

    def conv(scr, first_col, lanes, scale=None):
        cols = slice(first_col + lanes.start, first_col + lanes.stop)
        fold = (lambda w: w) if scale is None else (lambda w: scale * w)
        u = scr[:, lanes]
        body = u[HALO:]
        sublane = lax.broadcasted_iota(jnp.int32, (SUBLANES, LANES), 0)
        before = []
        for back in range(1, CONV_WIDTH):
            tile = body[(n_vregs - back) * SUBLANES:(n_vregs - back + 1) * SUBLANES]
            halo_row = jnp.broadcast_to(u[HALO - back:HALO - back + 1], (SUBLANES, LANES))
            before.insert(0, jnp.where(sublane == 0, halo_row, pltpu.roll(tile, 1, 0)))
        ext = jnp.concatenate(before + [body], axis=0)
        out = fold(cb_ref[:, cols])
        for t in range(CONV_WIDTH):
            out = out + fold(cw_ref[t:t + 1, cols]) * ext[t * SUBLANES:t * SUBLANES + TM_FFN]
        return out

    def chunk(start, width):
        ug_scr[:, :width] = jnp.dot(hext_scr[...], wup_ref[:, start:start + width],
                                    preferred_element_type=F32)
        ul_scr[:, :width] = jnp.dot(hext_scr[...], wup_ref[:, D_FF + start:D_FF + start + width],
                                    preferred_element_type=F32)
        for c in range(width // LANES):
            lanes = slice(c * LANES, (c + 1) * LANES)
            gate = conv(ug_scr, start, lanes)
            half_lin = conv(ul_scr, D_FF + start, lanes, scale=0.5)
            act_scr[:, lanes] = _gelu_times(gate, half_lin).astype(BF16)
        down = jnp.dot(act_scr[:, :width], wd_ref[start:start + width, :],
                       preferred_element_type=F32)
        for c in range(D_MODEL // LANES):
            piece = down[:, c * LANES:(c + 1) * LANES]
            if start == 0:
                acc_scr[c] = piece
            else:
                acc_scr[c] += piece

    j = pl.program_id(1)
    start = 0
    for step, width in enumerate(FFN_CHUNKS):
        pl.when(j == step)(functools.partial(chunk, start, width))
        start += width

    @pl.when(j == len(FFN_CHUNKS) - 1)
    def _():
        def rows_in_order(c):
            group = n_vregs // SUBLANES
            tiles = []
            for n in range(n_vregs):
                base = (n % group) * SUBLANES * SUBLANES + n // group
                tiles.append(acc_scr[c, pl.ds(base, SUBLANES, stride=SUBLANES), :])
            return jnp.concatenate(tiles, axis=0)

        acc = jnp.concatenate([rows_in_order(c) for c in range(D_MODEL // LANES)], axis=1)
        x2 = x1_ref[...] + acc
        h3 = _rms(x2, g3_ref[...]).astype(BF16)
        gate = jax.nn.sigmoid(jnp.dot(h3, wpg_ref[...], preferred_element_type=F32))
        ple = jnp.dot(p_ref[...].astype(BF16), wple_ref[...], preferred_element_type=F32)
        x3 = x2 + ple * gate
        out_ref[...] = _rms(x3, gf_ref[...])


def _ffn(x1, p2, g2, w_up, conv_w, conv_b, w_down, g3, w_pg, w_ple, gf, seq):
    n = x1.shape[0]
    halo_blocks = TM_FFN // HALO
    widest = max(FFN_CHUNKS)
    row = lambda width: pl.BlockSpec((TM_FFN, width), lambda i, j: (i, 0))
    const = lambda shape: pl.BlockSpec(shape, lambda i, j: (0,) * len(shape),
                                       pipeline_mode=pl.Buffered(1))
    r = np.arange(TM_FFN)
    n_vregs = TM_FFN // SUBLANES
    perm = np.zeros((TM_FFN, TM_FFN), np.float32)
    perm[SUBLANES * (r % n_vregs) + r // n_vregs, r] = 1.0
    return pl.pallas_call(
        functools.partial(_ffn_kernel, seq // TM_FFN),
        grid=(n // TM_FFN, len(FFN_CHUNKS)),
        in_specs=[
            pl.BlockSpec((HALO, D_MODEL),
                         lambda i, j: (jnp.maximum(i * halo_blocks - 1, 0), 0)),
            row(D_MODEL), row(PLE_DIM), const((1, D_MODEL)), const((TM_FFN, TM_FFN)),
            const((D_MODEL, 2 * D_FF)), const((CONV_WIDTH, 2 * D_FF)), const((1, 2 * D_FF)),
            const((D_FF, D_MODEL)), const((1, D_MODEL)), const((D_MODEL, D_MODEL)),
            const((PLE_DIM, D_MODEL)), const((1, D_MODEL)),
        ],
        out_specs=row(D_MODEL),
        out_shape=jax.ShapeDtypeStruct((n, D_MODEL), F32),
        scratch_shapes=[
            pltpu.VMEM((HALO + TM_FFN, D_MODEL), BF16),
            pltpu.VMEM((HALO + TM_FFN, widest), F32),
            pltpu.VMEM((HALO + TM_FFN, widest), F32),
            pltpu.VMEM((TM_FFN, widest), BF16),
            pltpu.VMEM((D_MODEL // LANES, TM_FFN, LANES), F32),
        ],
        compiler_params=pltpu.CompilerParams(
            dimension_semantics=("arbitrary", "arbitrary"),
            vmem_limit_bytes=VMEM_LIMIT),
        name="ffn",
    )(x1, x1, p2, g2, jnp.asarray(perm, BF16), w_up, conv_w, conv_b, w_down, g3, w_pg, w_ple, gf)


def _layer(x2, p2, batch, seq, norm_mix_g, w_in, b_f, gmlp_ln_g, gmlp_ln_b, gmlp_w_s, gmlp_b_s,
           w_branch_a, w_branch_b, w_out, norm_ffn_g, w_up, conv_w, conv_b, w_down,
           norm_ple_g, w_ple, w_ple_gate, out_g):
    o_f = 2 * GMLP_WIDTH + 3 * FOX_WIDTH
    w_all = w_in.astype(BF16)
    w_gates = w_all[:, o_f + FOX_HEADS:]
    w_f = jnp.pad(w_all[:, o_f:o_f + FOX_HEADS], ((0, 0), (0, LANES - FOX_HEADS)))
    b_fp = jnp.pad(b_f, (0, LANES - FOX_HEADS)).reshape(1, LANES)
    tri = jnp.asarray(np.tril(np.ones((CUM_BLK, CUM_BLK), np.float32)), BF16)

    z, cparts = _inproj(x2, norm_mix_g.reshape(1, -1), w_all, w_gates, w_f, b_fp, tri, seq)
    o, (w_a, w_b, w_o, w_u, w_d, w_pg, w_pl) = _attn(
        z, cparts, batch, seq,
        (w_branch_a, w_branch_b, w_out, w_up, w_down, w_ple_gate, w_ple))

    bias_full = jnp.repeat(gmlp_b_s.T, GMLP_BLOCK, axis=1)
    x1 = _mix(z, o, x2, gmlp_ln_g.reshape(1, -1), gmlp_ln_b.reshape(1, -1),
              gmlp_w_s.astype(BF16), bias_full, w_a, w_b, w_o)

    return _ffn(x1, p2, norm_ffn_g.reshape(1, -1), w_u, conv_w, conv_b.reshape(1, -1), w_d,
                norm_ple_g.reshape(1, -1), w_pg, w_pl, out_g.reshape(1, -1), seq)


def kernel(x, p, norm_mix_g, w_in, b_f, gmlp_ln_g, gmlp_ln_b, gmlp_w_s, gmlp_b_s, w_branch_a,
           w_branch_b, w_out, norm_ffn_g, w_up, conv_w, conv_b, w_down, norm_ple_g, w_ple,
           w_ple_gate, norm_final_g):
    batch, seq, d = x.shape
    depth = p.shape[0]
    assert d == D_MODEL and depth == 1, "the fused final norm assumes a single layer"
    assert seq % TM_IN == 0 and seq % TM_FFN == 0 and seq % TQ == 0
    x2 = x.reshape(batch * seq, d)
    out = _layer(x2, p[0].reshape(batch * seq, PLE_DIM), batch, seq,
                 norm_mix_g[0], w_in[0], b_f[0], gmlp_ln_g[0], gmlp_ln_b[0], gmlp_w_s[0],
                 gmlp_b_s[0], w_branch_a[0], w_branch_b[0], w_out[0], norm_ffn_g[0], w_up[0],
                 conv_w[0], conv_b[0], w_down[0], norm_ple_g[0], w_ple[0], w_ple_gate[0],
                 norm_final_g)
    return out.reshape(batch, seq, d)
```
